```python
import jax, jax.numpy as jnp
from jax import lax
import numpy as np

D_MODEL = 1024
BATCH = 8
SEQ = 2048
DEPTH = 2
DEC_BATCH = 128
DEC_SEQ = 8
PAST_LEN = 16384
PAGE_SIZE = 128

H_A = 4
DV_A = D_MODEL // 2 // H_A
DK_A = DV_A // 2
GLA_LOWRANK = 16
GLA_TAU = 16.0
H_B = 4
DV_B = D_MODEL // 2 // H_B
DK_B = 128
H_C = 4
DK_C = D_MODEL // H_C
DV_C = 2 * DK_C
D_FF = ((8 * D_MODEL // 3 + 127) // 128) * 128
CHUNK = 64
ROPE_BASE = 10000.0
EPS = 1e-6
N_EVEN = (DEPTH + 1) // 2
N_ODD = DEPTH // 2
EVEN_SPLITS = (H_A * DK_A, H_A * DK_A, H_A * DV_A, H_A * DV_A, GLA_LOWRANK, H_B * DK_B, H_B * DK_B, H_B * DV_B, H_B * DV_B)
EVEN_IN = sum(EVEN_SPLITS)
ODD_SPLITS = (H_C * DK_C, H_C * DK_C, H_C * DV_C, H_C * DV_C)
ODD_IN = sum(ODD_SPLITS)

kernel_name = 'hybrid_gla_hgrn2_retnet_macaron_decode_step'


def rmsnorm(x, w):
    xf = x.astype(jnp.float32)
    y = xf * lax.rsqrt(jnp.mean(xf * xf, axis=-1, keepdims=True) + EPS)
    return (y * w.astype(jnp.float32)).astype(x.dtype)


def swiglu(x, w_gate, w_up, w_down):
    return (jax.nn.silu(x @ w_gate) * (x @ w_up)) @ w_down


def split_cols(p, sizes):
    return jnp.split(p, np.cumsum(sizes)[:-1].tolist(), axis=-1)


def to_heads(a, n_heads):
    b, t, _ = a.shape
    return a.reshape(b, t, n_heads, -1).transpose(0, 2, 1, 3)


def gated_head_norm(o, gate, w, dtype):
    o = o.astype(jnp.float32)
    y = o * lax.rsqrt(jnp.mean(o * o, axis=-1, keepdims=True) + EPS) * w.astype(jnp.float32) * jax.nn.silu(gate.astype(jnp.float32))
    b, h, t, dv = y.shape
    return y.transpose(0, 2, 1, 3).reshape(b, t, h * dv).astype(dtype)


def rotary(x, pos):
    half = x.shape[-1] // 2
    inv_freq = ROPE_BASE ** (-jnp.arange(half, dtype=jnp.float32) / half)
    ang = pos.astype(jnp.float32)[:, None] * inv_freq[None, :]
    cos, sin = jnp.cos(ang), jnp.sin(ang)
    x1, x2 = x[..., :half], x[..., half:]
    return jnp.concatenate([x1 * cos - x2 * sin, x1 * sin + x2 * cos], axis=-1)


def chunked_decay_attention(q, k, v, log_decay, s0):
    bsz, nh, t, dk = q.shape
    dv = v.shape[-1]
    dg = log_decay.shape[-1]
    c = min(CHUNK, t)
    n = -(-t // c)
    pad = n * c - t
    q, k, v, g = (a.astype(jnp.float32) for a in (q, k, v, log_decay))
    if pad:
        q, k, v, g = (jnp.pad(a, ((0, 0), (0, 0), (0, pad), (0, 0))) for a in (q, k, v, g))

    def to_chunks(a):
        return jnp.moveaxis(a.reshape(bsz, nh, n, c, a.shape[-1]), 2, 0)

    causal = jnp.tril(jnp.ones((c, c), dtype=bool))

    def step(s, blk):
        qc, kc, vc, gc = blk
        b = jnp.cumsum(gc, axis=2)
        diff = b[:, :, :, None, :] - b[:, :, None, :, :]
        dec = jnp.exp(jnp.where(causal[:, :, None], diff, -jnp.inf))
        if dg == 1:
            scores = jnp.einsum('bhtd,bhsd->bhts', qc, kc) * dec[..., 0]
        else:
            scores = jnp.einsum('bhtd,bhsd,bhtsd->bhts', qc, kc, dec)
        o = jnp.einsum('bhts,bhse->bhte', scores, vc) + jnp.einsum('bhtd,bhde->bhte', qc * jnp.exp(b), s)
        b_last = b[:, :, -1:, :]
        s_new = jnp.exp(b_last[:, :, 0, :])[..., None] * s + jnp.einsum('bhsd,bhse->bhde', kc * jnp.exp(b_last - b), vc)
        return s_new, o

    s_fin, o = lax.scan(step, s0.astype(jnp.float32), (to_chunks(q), to_chunks(k), to_chunks(v), to_chunks(g)))
    o = jnp.moveaxis(o, 0, 2).reshape(bsz, nh, n * c, dv)[:, :, :t]
    return o, s_fin.astype(s0.dtype)


def even_mixer(h, s_gla, s_hgrn, w_in, w_gate2, b_gate, gla_norm_w, lb, hgrn_norm_w, w_out):
    q_a, k_a, v_a, r_a, lr_a, q_b, f_b, i_b, g_b = split_cols(h @ w_in, EVEN_SPLITS)
    log_alpha = jax.nn.log_sigmoid((lr_a @ w_gate2 + b_gate).astype(jnp.float32)) / GLA_TAU
    o_a, s_gla_new = chunked_decay_attention(to_heads(q_a, H_A) * (DK_A ** -0.5), to_heads(k_a, H_A), to_heads(v_a, H_A), to_heads(log_alpha, H_A), s_gla)
    y_a = gated_head_norm(o_a, to_heads(r_a, H_A), gla_norm_w, h.dtype)
    lb_h = lb.astype(jnp.float32).reshape(H_B, 1, DK_B)
    f = lb_h + (1.0 - lb_h) * jax.nn.sigmoid(to_heads(f_b, H_B).astype(jnp.float32))
    o_b, s_hgrn_new = chunked_decay_attention(jax.nn.silu(to_heads(q_b, H_B)), 1.0 - f, to_heads(i_b, H_B), jnp.log(f), s_hgrn)
    y_b = gated_head_norm(o_b, to_heads(g_b, H_B), hgrn_norm_w, h.dtype)
    return jnp.concatenate([y_a, y_b], axis=-1) @ w_out, s_gla_new, s_hgrn_new


def odd_mixer(h, pos, s_ret, w_in, ret_norm_w, w_out):
    q, k, v, g = split_cols(h @ w_in, ODD_SPLITS)
    q = rotary(to_heads(q, H_C).astype(jnp.float32), pos)
    k = rotary(to_heads(k, H_C).astype(jnp.float32), pos) * (DK_C ** -0.5)
    b, t, _ = h.shape
    log_gamma = jnp.log(1.0 - 2.0 ** (-5.0 - jnp.arange(H_C, dtype=jnp.float32)))
    ld = jnp.broadcast_to(log_gamma[None, :, None, None], (b, H_C, t, 1))
    o, s_new = chunked_decay_attention(q, k, to_heads(v, H_C), ld, s_ret)
    return gated_head_norm(o, to_heads(g, H_C), ret_norm_w, h.dtype) @ w_out, s_new


def trunk(x, pos, s_gla, s_hgrn, s_ret, norm_w, ffn_w_gate, ffn_w_up, ffn_w_down, even_w_in, gla_w_gate2, gla_b_gate, gla_norm_w, hgrn_lb_table, hgrn_norm_w, even_w_out, odd_w_in, ret_norm_w, odd_w_out):
    lb_all = jnp.cumsum(jax.nn.softmax(hgrn_lb_table.astype(jnp.float32), axis=0), axis=0)
    new_gla, new_hgrn, new_ret = [], [], []
    for l in range(DEPTH):
        nw = norm_w[l]
        x = x + 0.5 * rmsnorm(swiglu(rmsnorm(x, nw[0]), ffn_w_gate[l, 0], ffn_w_up[l, 0], ffn_w_down[l, 0]), nw[1])
        hn = rmsnorm(x, nw[2])
        if l % 2 == 0:
            e = l // 2
            m, sg, sh = even_mixer(hn, s_gla[e], s_hgrn[e], even_w_in[e], gla_w_gate2[e], gla_b_gate[e], gla_norm_w[e], lb_all[l], hgrn_norm_w[e], even_w_out[e])
            new_gla.append(sg)
            new_hgrn.append(sh)
        else:
            o = l // 2
            m, sr = odd_mixer(hn, pos, s_ret[o], odd_w_in[o], ret_norm_w[o], odd_w_out[o])
            new_ret.append(sr)
        x = x + rmsnorm(m, nw[3])
        x = x + 0.5 * rmsnorm(swiglu(rmsnorm(x, nw[4]), ffn_w_gate[l, 1], ffn_w_up[l, 1], ffn_w_down[l, 1]), nw[5])
    return x, jnp.stack(new_gla), jnp.stack(new_hgrn), jnp.stack(new_ret)


def setup_inputs(seed: int = 0) -> dict:
    key = jax.random.key(seed)
    ks = jax.random.split(key, 20)

    def nrm(k, shape, scale):
        return jax.random.normal(k, shape, jnp.float32) * scale

    return {
        'x_prompt': nrm(ks[0], (BATCH, SEQ, D_MODEL), 1.0),
        'x_sample': nrm(ks[1], (DEC_BATCH, DEC_SEQ, D_MODEL), 1.0),
        'state_gla': nrm(ks[2], (N_EVEN, DEC_BATCH, H_A, DK_A, DV_A), 0.5),
        'state_hgrn': nrm(ks[3], (N_EVEN, DEC_BATCH, H_B, DK_B, DV_B), 0.5),
        'state_ret': nrm(ks[4], (N_ODD, DEC_BATCH, H_C, DK_C, DV_C), 0.5),
        'norm_w': 1.0 + nrm(ks[5], (DEPTH, 6, D_MODEL), 0.05),
        'ffn_w_gate': nrm(ks[6], (DEPTH, 2, D_MODEL, D_FF), D_MODEL ** -0.5),
        'ffn_w_up': nrm(ks[7], (DEPTH, 2, D_MODEL, D_FF), D_MODEL ** -0.5),
        'ffn_w_down': nrm(ks[8], (DEPTH, 2, D_FF, D_MODEL), D_FF ** -0.5),
        'even_w_in': nrm(ks[9], (N_EVEN, D_MODEL, EVEN_IN), D_MODEL ** -0.5),
        'gla_w_gate2': nrm(ks[10], (N_EVEN, GLA_LOWRANK, H_A * DK_A), GLA_LOWRANK ** -0.5),
        'gla_b_gate': nrm(ks[11], (N_EVEN, H_A * DK_A), 0.1),
        'gla_norm_w': 1.0 + nrm(ks[12], (N_EVEN, DV_A), 0.05),
        'hgrn_lb_table': nrm(ks[13], (DEPTH + 1, H_B * DK_B), 0.5),
        'hgrn_norm_w': 1.0 + nrm(ks[14], (N_EVEN, DV_B), 0.05),
        'even_w_out': nrm(ks[15], (N_EVEN, H_A * DV_A + H_B * DV_B, D_MODEL), (H_A * DV_A + H_B * DV_B) ** -0.5),
        'odd_w_in': nrm(ks[16], (N_ODD, D_MODEL, ODD_IN), D_MODEL ** -0.5),
        'ret_norm_w': 1.0 + nrm(ks[17], (N_ODD, DV_C), 0.05),
        'odd_w_out': nrm(ks[18], (N_ODD, H_C * DV_C, D_MODEL), (H_C * DV_C) ** -0.5),
    }


def reference(x_prompt, x_sample, state_gla, state_hgrn, state_ret, norm_w, ffn_w_gate, ffn_w_up, ffn_w_down, even_w_in, gla_w_gate2, gla_b_gate, gla_norm_w, hgrn_lb_table, hgrn_norm_w, even_w_out, odd_w_in, ret_norm_w, odd_w_out):
    bp, tp, _ = x_prompt.shape
    ts = x_sample.shape[1]
    pos_p = jnp.arange(tp, dtype=jnp.int32)
    pos_s = PAST_LEN + jnp.arange(ts, dtype=jnp.int32)
    z_gla = jnp.zeros((N_EVEN, bp, H_A, DK_A, DV_A), x_prompt.dtype)
    z_hgrn = jnp.zeros((N_EVEN, bp, H_B, DK_B, DV_B), x_prompt.dtype)
    z_ret = jnp.zeros((N_ODD, bp, H_C, DK_C, DV_C), x_prompt.dtype)
    y_prompt, gla_p, hgrn_p, ret_p = trunk(x_prompt, pos_p, z_gla, z_hgrn, z_ret, norm_w, ffn_w_gate, ffn_w_up, ffn_w_down, even_w_in, gla_w_gate2, gla_b_gate, gla_norm_w, hgrn_lb_table, hgrn_norm_w, even_w_out, odd_w_in, ret_norm_w, odd_w_out)
    y_sample, gla_s, hgrn_s, ret_s = trunk(x_sample, pos_s, state_gla, state_hgrn, state_ret, norm_w, ffn_w_gate, ffn_w_up, ffn_w_down, even_w_in, gla_w_gate2, gla_b_gate, gla_norm_w, hgrn_lb_table, hgrn_norm_w, even_w_out, odd_w_in, ret_norm_w, odd_w_out)
    return (y_prompt, y_sample, gla_p, hgrn_p, ret_p, gla_s, hgrn_s, ret_s)
```

```python
import functools
import math

import jax
import jax.numpy as jnp
from jax import lax
from jax.experimental import pallas as pl
from jax.experimental.pallas import tpu as pltpu

f32 = jnp.float32
bf16 = jnp.bfloat16

D_MODEL = 1024
D_FF = 2816
EPS = 1e-6
PAST_LEN = 16384
ROPE_BASE = 10000.0
GLA_TAU = 16.0
N_HEADS = 4
GLA_DK = 64
EVEN_DK = 128
EVEN_DV = 128
GLA_LOWRANK = 16
RET_DK = 256
RET_DV = 512
LANES = 128
SUBLANES = 8
EVEN_P = 33 * LANES
ODD_P = 2 * N_HEADS * RET_DK + 2 * N_HEADS * RET_DV
VMEM_LIMIT = 56 * 1024 * 1024

_NT = (((1,), (1,)), ((), ()))
_TN = (((0,), (0,)), ((), ()))


def _rms(x, w):
    return x * lax.rsqrt(jnp.mean(x * x, axis=-1, keepdims=True) + EPS) * w


def _sigmoid(x):
    return 1.0 / (1.0 + jnp.exp(-x))


def _const_spec(shape):
    return pl.BlockSpec(shape, lambda *_: (0,) * len(shape), pipeline_mode=pl.Buffered(1))


def _params(*sem):
    return pltpu.CompilerParams(dimension_semantics=sem, vmem_limit_bytes=VMEM_LIMIT)


def _ffn_body(x_ref, nwa_ref, nwb_ref, wg_ref, wu_ref, wd_ref, o_ref, acc_ref, *, tf):
    x = x_ref[...]
    hn = _rms(x, nwa_ref[...]).astype(bf16)
    for j in range(D_FF // tf):
        cols = slice(j * tf, (j + 1) * tf)
        g = jnp.dot(hn, wg_ref[:, cols], preferred_element_type=f32)
        u = jnp.dot(hn, wu_ref[:, cols], preferred_element_type=f32)
        a = (g * _sigmoid(g) * u).astype(bf16)
        part = jnp.dot(a, wd_ref[cols, :], preferred_element_type=f32)
        if j == 0:
            acc_ref[...] = part
        else:
            acc_ref[...] += part
    o_ref[...] = x + 0.5 * _rms(acc_ref[...], nwb_ref[...])


def _ffn(x, nwa, nwb, wg, wu, wd, tm=512, tf=256):
    n = x.shape[0]
    return pl.pallas_call(
        functools.partial(_ffn_body, tf=tf),
        out_shape=jax.ShapeDtypeStruct((n, D_MODEL), f32),
        grid=(n // tm,),
        in_specs=[
            pl.BlockSpec((tm, D_MODEL), lambda i: (i, 0)),
            _const_spec((1, D_MODEL)),
            _const_spec((1, D_MODEL)),
            _const_spec((D_MODEL, D_FF)),
            _const_spec((D_MODEL, D_FF)),
            _const_spec((D_FF, D_MODEL)),
        ],
        out_specs=pl.BlockSpec((tm, D_MODEL), lambda i: (i, 0)),
        scratch_shapes=[pltpu.VMEM((tm, D_MODEL), f32)],
        compiler_params=_params("arbitrary"),
        name="ffn",
    )(x, nwa, nwb, wg, wu, wd)


def _proj_body(x_ref, nw_ref, w_ref, o_ref, *, tn):
    hn = _rms(x_ref[...], nw_ref[...]).astype(bf16)
    for j in range(w_ref.shape[1] // tn):
        cols = slice(j * tn, (j + 1) * tn)
        o_ref[:, cols] = jnp.dot(hn, w_ref[:, cols], preferred_element_type=f32)


def _proj(x, nw, w, tm, tn):
    n, p = x.shape[0], w.shape[1]
    return pl.pallas_call(
        functools.partial(_proj_body, tn=tn),
        out_shape=jax.ShapeDtypeStruct((n, p), f32),
        grid=(n // tm,),
        in_specs=[
            pl.BlockSpec((tm, D_MODEL), lambda i: (i, 0)),
            _const_spec((1, D_MODEL)),
            _const_spec((D_MODEL, p)),
        ],
        out_specs=pl.BlockSpec((tm, p), lambda i: (i, 0)),
        compiler_params=_params("arbitrary"),
        name="proj",
    )(x, nw, w)


def _out_body(y_ref, x_ref, w_ref, nw_ref, o_ref):
    m = jnp.dot(y_ref[...], w_ref[...], preferred_element_type=f32)
    o_ref[...] = x_ref[...] + _rms(m, nw_ref[...])


def _out_block(y, x, w, nw, tm=512):
    n, dy = y.shape
    return pl.pallas_call(
        _out_body,
        out_shape=jax.ShapeDtypeStruct((n, D_MODEL), f32),
        grid=(n // tm,),
        in_specs=[
            pl.BlockSpec((tm, dy), lambda i: (i, 0)),
            pl.BlockSpec((tm, D_MODEL), lambda i: (i, 0)),
            _const_spec((dy, D_MODEL)),
            _const_spec((1, D_MODEL)),
        ],
        out_specs=pl.BlockSpec((tm, D_MODEL), lambda i: (i, 0)),
        compiler_params=_params("arbitrary"),
        name="mixer_out",
    )(y, x, w, nw)


def _split3(g):
    hi = g.astype(bf16)
    r1 = g - hi.astype(f32)
    mid = r1.astype(bf16)
    lo = (r1 - mid.astype(f32)).astype(bf16)
    return jnp.concatenate([hi, mid, lo], axis=1)


def _roll_in_groups(x, d):
    r = x.shape[0]
    return pltpu.roll(x.reshape(r // SUBLANES, SUBLANES, LANES), d, 1).reshape(r, LANES)


def _near_diagonal(q, k, b, v, ones):
    r = q.shape[0]
    sub = lax.broadcasted_iota(jnp.int32, (r, LANES), 0) % SUBLANES
    prods = [q * k]
    for d in range(1, SUBLANES):
        decay = jnp.exp(jnp.minimum(b - _roll_in_groups(b, d), 0.0))
        prods.append(q * _roll_in_groups(k, d) * decay)
    rs = jnp.dot(jnp.concatenate(prods, axis=0).astype(bf16), ones, preferred_element_type=f32)
    o = rs[:r] * v
    for d in range(1, SUBLANES):
        o = o + jnp.where(sub >= d, rs[d * r:(d + 1) * r], 0.0) * _roll_in_groups(v, d)
    return o


def _far_blocks(q, k, b, v):
    c = q.shape[0]
    row = lax.broadcasted_iota(jnp.int32, (c, LANES), 0)
    ri = lax.broadcasted_iota(jnp.int32, (c, c), 0)
    ci = lax.broadcasted_iota(jnp.int32, (c, c), 1)
    scores = jnp.zeros((c, c), f32)
    half = SUBLANES
    while 2 * half <= c:
        blk = 2 * half
        ref = jnp.concatenate(
            [jnp.broadcast_to(b[j * blk + half:j * blk + half + 1, :], (blk, LANES)) for j in range(c // blk)], axis=0)
        upper = (row % blk) >= half
        e = jnp.exp(jnp.minimum(jnp.where(upper, b - ref, ref - b), 0.0))
        qe = jnp.where(upper, q * e, 0.0).astype(bf16)
        ke = jnp.where(upper, 0.0, k * e).astype(bf16)
        s = lax.dot_general(qe, ke, _NT, preferred_element_type=f32)
        if blk < c:
            s = jnp.where((ri // blk) == (ci // blk), s, 0.0)
        scores = scores + s
        half = blk
    return jnp.dot(scores.astype(bf16), v.astype(bf16), preferred_element_type=f32)


def _even_attn_body(*refs, C, NB, with_init, layer):
    if with_init:
        (p_ref, w2_ref, bg_ref, lbt_ref, gnw_ref, hnw_ref, sg_in, sh_in, y_ref, sg_out, sh_out, st_ref) = refs
    else:
        (p_ref, w2_ref, bg_ref, lbt_ref, gnw_ref, hnw_ref, y_ref, sg_out, sh_out, st_ref) = refs
    R = NB * C
    c = pl.program_id(1)

    @pl.when(c == 0)
    def _():
        zeros_half = jnp.zeros((GLA_DK, EVEN_DV), f32)
        for n in range(NB):
            for h in range(N_HEADS):
                if with_init:
                    st_ref[n, h] = jnp.concatenate([sg_in[n, h], zeros_half], axis=0).T
                    st_ref[n, N_HEADS + h] = sh_in[n, h].T
                else:
                    st_ref[n, h] = jnp.zeros((EVEN_DV, EVEN_DK), f32)
                    st_ref[n, N_HEADS + h] = jnp.zeros((EVEN_DV, EVEN_DK), f32)

    ri = lax.broadcasted_iota(jnp.int32, (2 * R, R), 0)
    ci = lax.broadcasted_iota(jnp.int32, (2 * R, R), 1)
    rr = jnp.where(ri >= R, ri - R, ri)
    same = (rr // C) == (ci // C)
    sum_mat = jnp.where(same, jnp.where(ri >= R, 1.0, jnp.where(ci <= rr, 1.0, 0.0)), 0.0).astype(bf16)
    ones = jnp.ones((LANES, LANES), bf16)

    lr = p_ref[:, 32 * LANES:33 * LANES].astype(bf16)
    xg = jnp.dot(lr, w2_ref[...], preferred_element_type=f32) + bg_ref[...]
    log_alpha = (jnp.minimum(xg, 0.0) - jnp.log1p(jnp.exp(-jnp.abs(xg)))) * (1.0 / GLA_TAU)

    lbt = lbt_ref[...]
    lbe = jnp.exp(lbt - jnp.max(lbt, axis=0, keepdims=True))
    lb_all = jnp.sum(lbe[:layer + 1], axis=0, keepdims=True) / jnp.sum(lbe, axis=0, keepdims=True)

    for h in range(2 * N_HEADS):
        hs = slice((h % N_HEADS) * LANES, (h % N_HEADS + 1) * LANES)

        def col(base, hs=hs):
            return p_ref[:, base * LANES + hs.start:base * LANES + hs.stop]

        if h < N_HEADS:
            q = col(0) * (GLA_DK ** -0.5)
            k = col(4)
            v = col(8)
            gate = col(12)
            g = log_alpha[:, hs]
            nw = gnw_ref[...]
        else:
            qb = col(16)
            q = qb * _sigmoid(qb)
            lb = lb_all[:, hs]
            f = lb + (1.0 - lb) * _sigmoid(col(20))
            k = 1.0 - f
            g = jnp.log(f)
            v = col(24)
            gate = col(28)
            nw = hnw_ref[...]

        sums = jnp.dot(sum_mat, _split3(g), preferred_element_type=f32)
        sums = sums[:, :LANES] + sums[:, LANES:2 * LANES] + sums[:, 2 * LANES:]
        b, b_last = sums[:R], sums[R:]

        o = _near_diagonal(q, k, b, v, ones)
        q_dec = (q * jnp.exp(b)).astype(bf16)
        k_dec = (k * jnp.exp(b_last - b)).astype(bf16)
        state_decay = jnp.exp(b_last)
        vb = v.astype(bf16)
        parts = []
        for n in range(NB):
            rows = slice(n * C, (n + 1) * C)
            st = st_ref[n, h]
            o_n = lax.dot_general(q_dec[rows], st.astype(bf16), _NT, preferred_element_type=f32)
            if C > SUBLANES:
                o_n = o_n + _far_blocks(q[rows], k[rows], b[rows], v[rows])
            parts.append(o_n)
            upd = lax.dot_general(vb[rows], k_dec[rows], _TN, preferred_element_type=f32)
            st_ref[n, h] = st * state_decay[(n + 1) * C - 1:(n + 1) * C, :] + upd
        o = o + (parts[0] if NB == 1 else jnp.concatenate(parts, axis=0))

        y = _rms(o, nw) * (gate * _sigmoid(gate))
        y_ref[:, h * LANES:(h + 1) * LANES] = y.astype(bf16)

    @pl.when(c == pl.num_programs(1) - 1)
    def _():
        for n in range(NB):
            for h in range(N_HEADS):
                sg_out[n, h] = st_ref[n, h].T[:GLA_DK, :]
                sh_out[n, h] = st_ref[n, N_HEADS + h].T


def _even_attn(p, w2, bg, lbt, gnw, hnw, sg, sh, *, n_seq, seq_len, C, NB, layer):
    with_init = sg is not None
    R = NB * C
    nc = seq_len // C
    grid = (n_seq // NB, nc)
    row_map = lambda i, c: (i * nc + c, 0)
    in_specs = [
        pl.BlockSpec((R, EVEN_P), row_map),
        _const_spec((LANES, N_HEADS * LANES)),
        _const_spec((1, N_HEADS * LANES)),
        _const_spec(lbt.shape),
        _const_spec((1, EVEN_DV)),
        _const_spec((1, EVEN_DV)),
    ]
    args = [p, w2, bg, lbt, gnw, hnw]
    sg_spec = pl.BlockSpec((NB, N_HEADS, GLA_DK, EVEN_DV), lambda i, c: (i, 0, 0, 0))
    sh_spec = pl.BlockSpec((NB, N_HEADS, EVEN_DK, EVEN_DV), lambda i, c: (i, 0, 0, 0))
    if with_init:
        in_specs += [sg_spec, sh_spec]
        args += [sg, sh]
    return pl.pallas_call(
        functools.partial(_even_attn_body, C=C, NB=NB, with_init=with_init, layer=layer),
        out_shape=(
            jax.ShapeDtypeStruct((n_seq * seq_len, 2 * N_HEADS * EVEN_DV), bf16),
            jax.ShapeDtypeStruct((n_seq, N_HEADS, GLA_DK, EVEN_DV), f32),
            jax.ShapeDtypeStruct((n_seq, N_HEADS, EVEN_DK, EVEN_DV), f32),
        ),
        grid=grid,
        in_specs=in_specs,
        out_specs=(pl.BlockSpec((R, 2 * N_HEADS * EVEN_DV), row_map), sg_spec, sh_spec),
        scratch_shapes=[pltpu.VMEM((NB, 2 * N_HEADS, EVEN_DV, EVEN_DK), f32)],
        compiler_params=_params("arbitrary", "arbitrary"),
        name="even_attn",
    )(*args)


def _odd_attn_body(*refs, C, NB, with_init, pos0):
    if with_init:
        p_ref, nw_ref, s_in, y_ref, s_out = refs
    else:
        p_ref, nw_ref, y_ref, s_out = refs
    R = NB * C
    c = pl.program_id(1)

    @pl.when(c == 0)
    def _():
        if with_init:
            s_out[...] = s_in[...]
        else:
            s_out[...] = jnp.zeros(s_out.shape, f32)

    half = RET_DK // 2
    t_in = lax.broadcasted_iota(jnp.int32, (R, half), 0) % C
    lane = lax.broadcasted_iota(jnp.int32, (R, half), 1).astype(f32)
    pos = (pos0 + c * C + t_in).astype(f32)
    ang = pos * jnp.exp(lane * (-math.log(ROPE_BASE) / half))
    cos, sin = jnp.cos(ang), jnp.sin(ang)

    def rotary(x):
        x1, x2 = x[:, :half], x[:, half:]
        return jnp.concatenate([x1 * cos - x2 * sin, x1 * sin + x2 * cos], axis=1)

    ri = lax.broadcasted_iota(jnp.int32, (R, R), 0)
    ci = lax.broadcasted_iota(jnp.int32, (R, R), 1)
    visible = ((ri // C) == (ci // C)) & (ci <= ri)
    dist = jnp.where(visible, ri - ci, 0).astype(f32)
    t_k = (lax.broadcasted_iota(jnp.int32, (R, RET_DK), 0) % C).astype(f32)

    for h in range(N_HEADS):
        log_gamma = math.log(1.0 - 2.0 ** (-5.0 - h))
        q = rotary(p_ref[:, h * RET_DK:(h + 1) * RET_DK])
        k = rotary(p_ref[:, (N_HEADS + h) * RET_DK:(N_HEADS + h + 1) * RET_DK]) * (RET_DK ** -0.5)
        v = p_ref[:, 2 * N_HEADS * RET_DK + h * RET_DV:2 * N_HEADS * RET_DK + (h + 1) * RET_DV]
        gate = p_ref[:, 2 * N_HEADS * RET_DK + (N_HEADS + h) * RET_DV:2 * N_HEADS * RET_DK + (N_HEADS + h + 1) * RET_DV]
        vb = v.astype(bf16)

        decay = jnp.where(visible, jnp.exp(dist * log_gamma), 0.0)
        s = lax.dot_general(q.astype(bf16), k.astype(bf16), _NT, preferred_element_type=f32) * decay
        o = jnp.dot(s.astype(bf16), vb, preferred_element_type=f32)

        q_dec = (q * jnp.exp((t_k + 1.0) * log_gamma)).astype(bf16)
        k_dec = (k * jnp.exp((C - 1.0 - t_k) * log_gamma)).astype(bf16)
        parts = []
        for n in range(NB):
            rows = slice(n * C, (n + 1) * C)
            st = s_out[n, h]
            parts.append(jnp.dot(q_dec[rows], st.astype(bf16), preferred_element_type=f32))
            upd = lax.dot_general(k_dec[rows], vb[rows], _TN, preferred_element_type=f32)
            s_out[n, h] = st * math.exp(C * log_gamma) + upd
        o = o + (parts[0] if NB == 1 else jnp.concatenate(parts, axis=0))

        y = _rms(o, nw_ref[...]) * (gate * _sigmoid(gate))
        y_ref[:, h * RET_DV:(h + 1) * RET_DV] = y.astype(bf16)


def _odd_attn(p, nw, s, *, n_seq, seq_len, C, NB, pos0):
    with_init = s is not None
    R = NB * C
    nc = seq_len // C
    row_map = lambda i, c: (i * nc + c, 0)
    s_spec = pl.BlockSpec((NB, N_HEADS, RET_DK, RET_DV), lambda i, c: (i, 0, 0, 0))
    in_specs = [pl.BlockSpec((R, ODD_P), row_map), _const_spec((1, RET_DV))]
    args = [p, nw]
    if with_init:
        in_specs.append(s_spec)
        args.append(s)
    return pl.pallas_call(
        functools.partial(_odd_attn_body, C=C, NB=NB, with_init=with_init, pos0=pos0),
        out_shape=(
            jax.ShapeDtypeStruct((n_seq * seq_len, N_HEADS * RET_DV), bf16),
            jax.ShapeDtypeStruct((n_seq, N_HEADS, RET_DK, RET_DV), f32),
        ),
        grid=(n_seq // NB, nc),
        in_specs=in_specs,
        out_specs=(pl.BlockSpec((R, N_HEADS * RET_DV), row_map), s_spec),
        compiler_params=_params("arbitrary", "arbitrary"),
        name="odd_attn",
    )(*args)


def _pad_heads(w, dk):
    lead = w.shape[:-1]
    w = w.reshape(*lead, N_HEADS, dk)
    w = jnp.pad(w, [(0, 0)] * len(lead) + [(0, 0), (0, LANES - dk)])
    return w.reshape(*lead, N_HEADS * LANES)


def _pack_even_w_in(w):
    hk, hv = N_HEADS * GLA_DK, N_HEADS * EVEN_DV
    edges = [0, hk, 2 * hk, 2 * hk + hv, 2 * hk + 2 * hv, 2 * hk + 2 * hv + GLA_LOWRANK]
    q_a, k_a, v_a, r_a, lr = (w[:, edges[i]:edges[i + 1]] for i in range(5))
    rest = w[:, edges[5]:]
    lr = jnp.pad(lr, ((0, 0), (0, LANES - GLA_LOWRANK)))
    return jnp.concatenate([_pad_heads(q_a, GLA_DK), _pad_heads(k_a, GLA_DK), v_a, r_a, rest, lr], axis=1)


def _trunk(x, s_gla, s_hgrn, s_ret, w, *, n_seq, seq_len, C, NB_even, NB_odd, pos0):
    nw = w["norm_w"]
    depth = nw.shape[0]
    new_gla, new_hgrn, new_ret = [], [], []
    for l in range(depth):
        row = lambda i, l=l: nw[l, i][None, :]
        x = _ffn(x, row(0), row(1), w["ffn_g"][l, 0], w["ffn_u"][l, 0], w["ffn_d"][l, 0])
        if l % 2 == 0:
            e = l // 2
            p = _proj(x, row(2), w["even_in"][e], tm=512, tn=11 * LANES)
            y, sg, sh = _even_attn(
                p, w["gla_w2"][e], w["gla_bg"][e], w["lb_table"], w["gla_nw"][e], w["hgrn_nw"][e],
                None if s_gla is None else s_gla[e], None if s_hgrn is None else s_hgrn[e],
                n_seq=n_seq, seq_len=seq_len, C=C, NB=NB_even, layer=l)
            new_gla.append(sg)
            new_hgrn.append(sh)
            x = _out_block(y, x, w["even_out"][e], row(3))
        else:
            o = l // 2
            p = _proj(x, row(2), w["odd_in"][o], tm=256, tn=12 * LANES)
            y, sr = _odd_attn(p, w["ret_nw"][o], None if s_ret is None else s_ret[o],
                              n_seq=n_seq, seq_len=seq_len, C=C, NB=NB_odd, pos0=pos0)
            new_ret.append(sr)
            x = _out_block(y, x, w["odd_out"][o], row(3))
        x = _ffn(x, row(4), row(5), w["ffn_g"][l, 1], w["ffn_u"][l, 1], w["ffn_d"][l, 1])
    return x, jnp.stack(new_gla), jnp.stack(new_hgrn), jnp.stack(new_ret)


def kernel(x_prompt, x_sample, state_gla, state_hgrn, state_ret, norm_w, ffn_w_gate, ffn_w_up, ffn_w_down, even_w_in, gla_w_gate2, gla_b_gate, gla_norm_w, hgrn_lb_table, hgrn_norm_w, even_w_out, odd_w_in, ret_norm_w, odd_w_out):
    bp, tp, _ = x_prompt.shape
    bs, ts, _ = x_sample.shape
    n_even = even_w_in.shape[0]
    w = {
        "norm_w": norm_w.astype(f32),
        "ffn_g": ffn_w_gate.astype(bf16),
        "ffn_u": ffn_w_up.astype(bf16),
        "ffn_d": ffn_w_down.astype(bf16),
        "even_in": jnp.stack([_pack_even_w_in(even_w_in[e]) for e in range(n_even)]).astype(bf16),
        "gla_w2": jnp.pad(_pad_heads(gla_w_gate2, GLA_DK), ((0, 0), (0, LANES - GLA_LOWRANK), (0, 0))).astype(bf16),
        "gla_bg": _pad_heads(gla_b_gate, GLA_DK)[:, None, :].astype(f32),
        "lb_table": hgrn_lb_table.astype(f32),
        "gla_nw": gla_norm_w[:, None, :].astype(f32),
        "hgrn_nw": hgrn_norm_w[:, None, :].astype(f32),
        "even_out": even_w_out.astype(bf16),
        "odd_in": odd_w_in.astype(bf16),
        "ret_nw": ret_norm_w[:, None, :].astype(f32),
        "odd_out": odd_w_out.astype(bf16),
    }
    y_p, gla_p, hgrn_p, ret_p = _trunk(
        x_prompt.reshape(bp * tp, D_MODEL), None, None, None, w,
        n_seq=bp, seq_len=tp, C=64, NB_even=1, NB_odd=1, pos0=0)
    y_s, gla_s, hgrn_s, ret_s = _trunk(
        x_sample.reshape(bs * ts, D_MODEL), state_gla, state_hgrn, state_ret, w,
        n_seq=bs, seq_len=ts, C=ts, NB_even=8, NB_odd=4, pos0=PAST_LEN)
    return (y_p.reshape(bp, tp, D_MODEL), y_s.reshape(bs, ts, D_MODEL), gla_p, hgrn_p, ret_p, gla_s, hgrn_s, ret_s)
```

```python
import functools
import math

import jax
import jax.numpy as jnp
import numpy as np
from jax import lax
from jax.experimental import pallas as pl
from jax.experimental.pallas import tpu as pltpu

f32 = jnp.float32
bf16 = jnp.bfloat16

D_MODEL = 1024
D_FF = 2816
EPS = 1e-6
PAST_LEN = 16384
ROPE_BASE = 10000.0
GLA_TAU = 16.0
N_HEADS = 4
GLA_DK = 64
EVEN_DK = 128
EVEN_DV = 128
GLA_LOWRANK = 16
RET_DK = 256
RET_DV = 512
LANES = 128
SUBLANES = 8
EVEN_P = 33 * LANES
ODD_P = 2 * N_HEADS * RET_DK + 2 * N_HEADS * RET_DV
VMEM_LIMIT = 56 * 1024 * 1024

_NT = (((1,), (1,)), ((), ()))
_TN = (((0,), (0,)), ((), ()))


def _rms(x, w):
    return x * lax.rsqrt(jnp.mean(x * x, axis=-1, keepdims=True) + EPS) * w


def _sigmoid(x):
    return 1.0 / (1.0 + jnp.exp(-x))


def _const_spec(shape):
    return pl.BlockSpec(shape, lambda *_: (0,) * len(shape), pipeline_mode=pl.Buffered(1))


def _params(*sem):
    return pltpu.CompilerParams(dimension_semantics=sem, vmem_limit_bytes=VMEM_LIMIT)


def _ffn_body(x_ref, nwa_ref, nwb_ref, wg_ref, wu_ref, wd_ref, o_ref, acc_ref, *, tf):
    x = x_ref[...]
    hn = _rms(x, nwa_ref[...]).astype(bf16)
    for j in range(D_FF // tf):
        cols = slice(j * tf, (j + 1) * tf)
        g = jnp.dot(hn, wg_ref[:, cols], preferred_element_type=f32)
        u = jnp.dot(hn, wu_ref[:, cols], preferred_element_type=f32)
        a = (g * _sigmoid(g) * u).astype(bf16)
        part = jnp.dot(a, wd_ref[cols, :], preferred_element_type=f32)
        if j == 0:
            acc_ref[...] = part
        else:
            acc_ref[...] += part
    o_ref[...] = x + 0.5 * _rms(acc_ref[...], nwb_ref[...])


def _ffn(x, nwa, nwb, wg, wu, wd, tm=512, tf=256):
    n = x.shape[0]
    return pl.pallas_call(
        functools.partial(_ffn_body, tf=tf),
        out_shape=jax.ShapeDtypeStruct((n, D_MODEL), f32),
        grid=(n // tm,),
        in_specs=[
            pl.BlockSpec((tm, D_MODEL), lambda i: (i, 0)),
            _const_spec((1, D_MODEL)),
            _const_spec((1, D_MODEL)),
            _const_spec((D_MODEL, D_FF)),
            _const_spec((D_MODEL, D_FF)),
            _const_spec((D_FF, D_MODEL)),
        ],
        out_specs=pl.BlockSpec((tm, D_MODEL), lambda i: (i, 0)),
        scratch_shapes=[pltpu.VMEM((tm, D_MODEL), f32)],
        compiler_params=_params("arbitrary"),
        name="ffn",
    )(x, nwa, nwb, wg, wu, wd)


def _proj_body(x_ref, nw_ref, w_ref, o_ref, *, tn):
    hn = _rms(x_ref[...], nw_ref[...]).astype(bf16)
    for j in range(w_ref.shape[1] // tn):
        cols = slice(j * tn, (j + 1) * tn)
        o_ref[:, cols] = jnp.dot(hn, w_ref[:, cols], preferred_element_type=f32)


def _proj(x, nw, w, tm, tn):
    n, p = x.shape[0], w.shape[1]
    return pl.pallas_call(
        functools.partial(_proj_body, tn=tn),
        out_shape=jax.ShapeDtypeStruct((n, p), f32),
        grid=(n // tm,),
        in_specs=[
            pl.BlockSpec((tm, D_MODEL), lambda i: (i, 0)),
            _const_spec((1, D_MODEL)),
            _const_spec((D_MODEL, p)),
        ],
        out_specs=pl.BlockSpec((tm, p), lambda i: (i, 0)),
        compiler_params=_params("arbitrary"),
        name="proj",
    )(x, nw, w)


def _out_body(y_ref, x_ref, w_ref, nw_ref, o_ref):
    m = jnp.dot(y_ref[...], w_ref[...], preferred_element_type=f32)
    o_ref[...] = x_ref[...] + _rms(m, nw_ref[...])


def _out_block(y, x, w, nw, tm=512):
    n, dy = y.shape
    return pl.pallas_call(
        _out_body,
        out_shape=jax.ShapeDtypeStruct((n, D_MODEL), f32),
        grid=(n // tm,),
        in_specs=[
            pl.BlockSpec((tm, dy), lambda i: (i, 0)),
            pl.BlockSpec((tm, D_MODEL), lambda i: (i, 0)),
            _const_spec((dy, D_MODEL)),
            _const_spec((1, D_MODEL)),
        ],
        out_specs=pl.BlockSpec((tm, D_MODEL), lambda i: (i, 0)),
        compiler_params=_params("arbitrary"),
        name="mixer_out",
    )(y, x, w, nw)


def _split3(g):
    hi = g.astype(bf16)
    r1 = g - hi.astype(f32)
    mid = r1.astype(bf16)
    lo = (r1 - mid.astype(f32)).astype(bf16)
    return jnp.concatenate([hi, mid, lo], axis=1)


def _roll_in_groups(x, d):
    r = x.shape[0]
    return pltpu.roll(x.reshape(r // SUBLANES, SUBLANES, LANES), d, 1).reshape(r, LANES)


def _near_diagonal(q, k, b, v, ones):
    r = q.shape[0]
    sub = lax.broadcasted_iota(jnp.int32, (r, LANES), 0) % SUBLANES
    prods = [q * k]
    for d in range(1, SUBLANES):
        decay = jnp.exp(jnp.minimum(b - _roll_in_groups(b, d), 0.0))
        prods.append(q * _roll_in_groups(k, d) * decay)
    rs = jnp.dot(jnp.concatenate(prods, axis=0).astype(bf16), ones, preferred_element_type=f32)
    o = rs[:r] * v
    for d in range(1, SUBLANES):
        o = o + jnp.where(sub >= d, rs[d * r:(d + 1) * r], 0.0) * _roll_in_groups(v, d)
    return o


def _far_blocks(q, k, b, v):
    c = q.shape[0]
    row = lax.broadcasted_iota(jnp.int32, (c, LANES), 0)
    ri = lax.broadcasted_iota(jnp.int32, (c, c), 0)
    ci = lax.broadcasted_iota(jnp.int32, (c, c), 1)
    scores = jnp.zeros((c, c), f32)
    half = SUBLANES
    while 2 * half <= c:
        blk = 2 * half
        ref = jnp.concatenate(
            [jnp.broadcast_to(b[j * blk + half:j * blk + half + 1, :], (blk, LANES)) for j in range(c // blk)], axis=0)
        upper = (row % blk) >= half
        e = jnp.exp(jnp.minimum(jnp.where(upper, b - ref, ref - b), 0.0))
        qe = jnp.where(upper, q * e, 0.0).astype(bf16)
        ke = jnp.where(upper, 0.0, k * e).astype(bf16)
        s = lax.dot_general(qe, ke, _NT, preferred_element_type=f32)
        if blk < c:
            s = jnp.where((ri // blk) == (ci // blk), s, 0.0)
        scores = scores + s
        half = blk
    return jnp.dot(scores.astype(bf16), v.astype(bf16), preferred_element_type=f32)


def _even_attn_body(*refs, C, NB, with_init, layer):
    if with_init:
        (p_ref, w2_ref, bg_ref, lbt_ref, gnw_ref, hnw_ref, sg_in, sh_in, y_ref, sg_out, sh_out, st_ref) = refs
    else:
        (p_ref, w2_ref, bg_ref, lbt_ref, gnw_ref, hnw_ref, y_ref, sg_out, sh_out, st_ref) = refs
    R = NB * C
    c = pl.program_id(1)

    @pl.when(c == 0)
    def _():
        zeros_half = jnp.zeros((GLA_DK, EVEN_DV), f32)
        for n in range(NB):
            for h in range(N_HEADS):
                if with_init:
                    st_ref[n, h] = jnp.concatenate([sg_in[n, h], zeros_half], axis=0).T
                    st_ref[n, N_HEADS + h] = sh_in[n, h].T
                else:
                    st_ref[n, h] = jnp.zeros((EVEN_DV, EVEN_DK), f32)
                    st_ref[n, N_HEADS + h] = jnp.zeros((EVEN_DV, EVEN_DK), f32)

    ri = lax.broadcasted_iota(jnp.int32, (2 * R, R), 0)
    ci = lax.broadcasted_iota(jnp.int32, (2 * R, R), 1)
    rr = jnp.where(ri >= R, ri - R, ri)
    same = (rr // C) == (ci // C)
    sum_mat = jnp.where(same, jnp.where(ri >= R, 1.0, jnp.where(ci <= rr, 1.0, 0.0)), 0.0).astype(bf16)
    ones = jnp.ones((LANES, LANES), bf16)

    lr = p_ref[:, 32 * LANES:33 * LANES].astype(bf16)
    xg = jnp.dot(lr, w2_ref[...], preferred_element_type=f32) + bg_ref[...]
    log_alpha = (jnp.minimum(xg, 0.0) - jnp.log1p(jnp.exp(-jnp.abs(xg)))) * (1.0 / GLA_TAU)

    lbt = lbt_ref[...]
    lbe = jnp.exp(lbt - jnp.max(lbt, axis=0, keepdims=True))
    lb_all = jnp.sum(lbe[:layer + 1], axis=0, keepdims=True) / jnp.sum(lbe, axis=0, keepdims=True)

    for h in range(2 * N_HEADS):
        hs = slice((h % N_HEADS) * LANES, (h % N_HEADS + 1) * LANES)

        def col(base, hs=hs):
            return p_ref[:, base * LANES + hs.start:base * LANES + hs.stop]

        if h < N_HEADS:
            q = col(0) * (GLA_DK ** -0.5)
            k = col(4)
            v = col(8)
            gate = col(12)
            g = log_alpha[:, hs]
            nw = gnw_ref[...]
        else:
            qb = col(16)
            q = qb * _sigmoid(qb)
            lb = lb_all[:, hs]
            f = lb + (1.0 - lb) * _sigmoid(col(20))
            k = 1.0 - f
            g = jnp.log(f)
            v = col(24)
            gate = col(28)
            nw = hnw_ref[...]

        sums = jnp.dot(sum_mat, _split3(g), preferred_element_type=f32)
        sums = sums[:, :LANES] + sums[:, LANES:2 * LANES] + sums[:, 2 * LANES:]
        b, b_last = sums[:R], sums[R:]

        o = _near_diagonal(q, k, b, v, ones)
        q_dec = (q * jnp.exp(b)).astype(bf16)
        k_dec = (k * jnp.exp(b_last - b)).astype(bf16)
        state_decay = jnp.exp(b_last)
        vb = v.astype(bf16)
        parts = []
        for n in range(NB):
            rows = slice(n * C, (n + 1) * C)
            st = st_ref[n, h]
            o_n = lax.dot_general(q_dec[rows], st.astype(bf16), _NT, preferred_element_type=f32)
            if C > SUBLANES:
                o_n = o_n + _far_blocks(q[rows], k[rows], b[rows], v[rows])
            parts.append(o_n)
            upd = lax.dot_general(vb[rows], k_dec[rows], _TN, preferred_element_type=f32)
            st_ref[n, h] = st * state_decay[(n + 1) * C - 1:(n + 1) * C, :] + upd
        o = o + (parts[0] if NB == 1 else jnp.concatenate(parts, axis=0))

        y = _rms(o, nw) * (gate * _sigmoid(gate))
        y_ref[:, h * LANES:(h + 1) * LANES] = y.astype(bf16)

    @pl.when(c == pl.num_programs(1) - 1)
    def _():
        for n in range(NB):
            for h in range(N_HEADS):
                sg_out[n, h] = st_ref[n, h].T[:GLA_DK, :]
                sh_out[n, h] = st_ref[n, N_HEADS + h].T


def _even_attn(p, w2, bg, lbt, gnw, hnw, sg, sh, *, n_seq, seq_len, C, NB, layer):
    with_init = sg is not None
    R = NB * C
    nc = seq_len // C
    grid = (n_seq // NB, nc)
    row_map = lambda i, c: (i * nc + c, 0)
    in_specs = [
        pl.BlockSpec((R, EVEN_P), row_map),
        _const_spec((LANES, N_HEADS * LANES)),
        _const_spec((1, N_HEADS * LANES)),
        _const_spec(lbt.shape),
        _const_spec((1, EVEN_DV)),
        _const_spec((1, EVEN_DV)),
    ]
    args = [p, w2, bg, lbt, gnw, hnw]
    sg_spec = pl.BlockSpec((NB, N_HEADS, GLA_DK, EVEN_DV), lambda i, c: (i, 0, 0, 0))
    sh_spec = pl.BlockSpec((NB, N_HEADS, EVEN_DK, EVEN_DV), lambda i, c: (i, 0, 0, 0))
    if with_init:
        in_specs += [sg_spec, sh_spec]
        args += [sg, sh]
    return pl.pallas_call(
        functools.partial(_even_attn_body, C=C, NB=NB, with_init=with_init, layer=layer),
        out_shape=(
            jax.ShapeDtypeStruct((n_seq * seq_len, 2 * N_HEADS * EVEN_DV), bf16),
            jax.ShapeDtypeStruct((n_seq, N_HEADS, GLA_DK, EVEN_DV), f32),
            jax.ShapeDtypeStruct((n_seq, N_HEADS, EVEN_DK, EVEN_DV), f32),
        ),
        grid=grid,
        in_specs=in_specs,
        out_specs=(pl.BlockSpec((R, 2 * N_HEADS * EVEN_DV), row_map), sg_spec, sh_spec),
        scratch_shapes=[pltpu.VMEM((NB, 2 * N_HEADS, EVEN_DV, EVEN_DK), f32)],
        compiler_params=_params("arbitrary", "arbitrary"),
        name="even_attn",
    )(*args)


CHUNK = 64
N_LEVELS = 6
LOG2E = 1.4426950408889634
EVEN_PW = 36 * LANES


def _chunk_tables():
    t = np.arange(CHUNK)[:, None]
    j = np.arange(CHUNK)[None, :]
    mats = []
    for l in range(N_LEVELS):
        hs = 1 << l
        m = (t // (2 * hs)) * (2 * hs) + hs
        mats.append(np.where(t >= m, (j > m) & (j <= t), (j > t) & (j <= m)))
    mats.append(j <= t)
    mats.append(np.ones((CHUNK, CHUNK), bool))
    a = np.concatenate(mats, axis=0).astype(np.float32)
    x = t ^ j
    lvl = np.where(j > t, N_LEVELS + 1, np.where(j == t, 0, np.floor(np.log2(np.maximum(x, 1))).astype(np.int64) + 1))
    return np.concatenate([a, a], axis=1), lvl.astype(np.int32)


def _decay_unit(q, k, g2, v, st, sum_mat, level):
    hi = g2.astype(bf16)
    lo = (g2 - hi.astype(f32)).astype(bf16)
    z = jnp.dot(sum_mat, jnp.concatenate([hi, lo], axis=0), preferred_element_type=f32)
    vb = v.astype(bf16)
    s0 = lax.dot_general(q.astype(bf16), k.astype(bf16), _NT, preferred_element_type=f32)
    scores = jnp.where(level == 0, s0, 0.0)
    for l in range(N_LEVELS):
        e = jnp.exp2(z[l * CHUNK:(l + 1) * CHUNK])
        s = lax.dot_general((q * e).astype(bf16), (k * e).astype(bf16), _NT, preferred_element_type=f32)
        scores = jnp.where(level == l + 1, s, scores)
    b = z[N_LEVELS * CHUNK:(N_LEVELS + 1) * CHUNK]
    b_last = z[(N_LEVELS + 1) * CHUNK:]
    o = jnp.dot(scores.astype(bf16), vb, preferred_element_type=f32)
    o = o + lax.dot_general((q * jnp.exp2(b)).astype(bf16), st.astype(bf16), _NT, preferred_element_type=f32)
    upd = lax.dot_general(vb, (k * jnp.exp2(b_last - b)).astype(bf16), _TN, preferred_element_type=f32)
    return o, st * jnp.exp2(b_last[:1]) + upd


def _even_layer_body(xa_ref, xc_ref, nw2_ref, nw3_ref, win_ref, w2_ref, bg_ref, lbt_ref, gnw_ref, hnw_ref, wout_ref,
                     smat_ref, lvl_ref, o_ref, sg_out, sh_out, p_buf, hn_scr, y_scr, st_ref, *, NB, layer):
    R = NB * CHUNK
    s = pl.program_id(1)
    head_w = 4 * LANES
    xg0 = 2 * N_HEADS * head_w

    @pl.when(s == 0)
    def _():
        p_buf[...] = jnp.zeros((R, EVEN_PW), f32)

    @pl.when(s <= 1)
    def _():
        st_ref[...] = jnp.zeros(st_ref.shape, f32)

    hn_scr[...] = _rms(xa_ref[...].reshape(R, D_MODEL), nw2_ref[...]).astype(bf16)

    lbt = lbt_ref[...]
    lbe = jnp.exp(lbt - jnp.max(lbt, axis=0, keepdims=True))
    lb_all = jnp.sum(lbe[:layer + 1], axis=0, keepdims=True) / jnp.sum(lbe, axis=0, keepdims=True)
    sum_mat = smat_ref[...]
    level = lvl_ref[...]

    for h in range(2 * N_HEADS):
        hh = h % N_HEADS
        for n in range(NB):
            rows = slice(n * CHUNK, (n + 1) * CHUNK)

            def col(j, h=h, rows=rows):
                return p_buf[rows, h * head_w + j * LANES:h * head_w + (j + 1) * LANES]

            if h < N_HEADS:
                q = col(0) * (GLA_DK ** -0.5)
                k = col(1)
                xg = p_buf[rows, xg0 + hh * LANES:xg0 + (hh + 1) * LANES]
                g2 = (jnp.minimum(xg, 0.0) - jnp.log1p(jnp.exp(-jnp.abs(xg)))) * (LOG2E / GLA_TAU)
                nw = gnw_ref[...]
            else:
                qb = col(0)
                q = qb * _sigmoid(qb)
                lb = lb_all[:, hh * LANES:(hh + 1) * LANES]
                f = lb + (1.0 - lb) * _sigmoid(col(1))
                k = 1.0 - f
                g2 = jnp.log2(f)
                nw = hnw_ref[...]
            v = col(2)
            gate = col(3)
            o, st_new = _decay_unit(q, k, g2, v, st_ref[n, h], sum_mat, level)
            st_ref[n, h] = st_new
            y = _rms(o, nw) * (gate * _sigmoid(gate))
            y_scr[rows, h * LANES:(h + 1) * LANES] = y.astype(bf16)

        hs = slice(h * head_w, (h + 1) * head_w)
        p_buf[:, hs] = jnp.dot(hn_scr[...], win_ref[:, hs], preferred_element_type=f32)
        if h == N_HEADS - 1:
            lr = jnp.dot(hn_scr[...], win_ref[:, xg0:], preferred_element_type=f32).astype(bf16)
            p_buf[:, xg0:] = jnp.dot(lr, w2_ref[...], preferred_element_type=f32) + bg_ref[...]

    m = jnp.dot(y_scr[...], wout_ref[...], preferred_element_type=f32)
    o_ref[...] = (xc_ref[...].reshape(R, D_MODEL) + _rms(m, nw3_ref[...])).reshape(NB, CHUNK, D_MODEL)

    @pl.when(s == pl.num_programs(1) - 1)
    def _():
        for n in range(NB):
            for h in range(N_HEADS):
                sg_out[n, h] = st_ref[n, h].T[:GLA_DK, :]
                sh_out[n, h] = st_ref[n, N_HEADS + h].T


def _even_layer_prompt(x3, nw2, nw3, w_in, w2, bg, lbt, gnw, hnw, w_out, *, layer, NB=4):
    n_seq, seq_len, _ = x3.shape
    nc = seq_len // CHUNK
    R = NB * CHUNK
    smat, lvl = _chunk_tables()
    x_blk = (NB, CHUNK, D_MODEL)
    sg_spec = pl.BlockSpec((NB, N_HEADS, GLA_DK, EVEN_DV), lambda i, s: (i, 0, 0, 0))
    sh_spec = pl.BlockSpec((NB, N_HEADS, EVEN_DK, EVEN_DV), lambda i, s: (i, 0, 0, 0))
    return pl.pallas_call(
        functools.partial(_even_layer_body, NB=NB, layer=layer),
        out_shape=(
            jax.ShapeDtypeStruct(x3.shape, f32),
            jax.ShapeDtypeStruct((n_seq, N_HEADS, GLA_DK, EVEN_DV), f32),
            jax.ShapeDtypeStruct((n_seq, N_HEADS, EVEN_DK, EVEN_DV), f32),
        ),
        grid=(n_seq // NB, nc + 1),
        in_specs=[
            pl.BlockSpec(x_blk, lambda i, s: (i, jnp.minimum(s, nc - 1), 0)),
            pl.BlockSpec(x_blk, lambda i, s: (i, jnp.maximum(s - 1, 0), 0)),
            _const_spec((1, D_MODEL)),
            _const_spec((1, D_MODEL)),
            _const_spec(w_in.shape),
            _const_spec(w2.shape),
            _const_spec(bg.shape),
            _const_spec(lbt.shape),
            _const_spec((1, EVEN_DV)),
            _const_spec((1, EVEN_DV)),
            _const_spec(w_out.shape),
            _const_spec(smat.shape),
            _const_spec(lvl.shape),
        ],
        out_specs=(pl.BlockSpec(x_blk, lambda i, s: (i, jnp.maximum(s - 1, 0), 0)), sg_spec, sh_spec),
        scratch_shapes=[
            pltpu.VMEM((R, EVEN_PW), f32),
            pltpu.VMEM((R, D_MODEL), bf16),
            pltpu.VMEM((R, 2 * N_HEADS * EVEN_DV), bf16),
            pltpu.VMEM((NB, 2 * N_HEADS, EVEN_DV, EVEN_DK), f32),
        ],
        compiler_params=_params("arbitrary", "arbitrary"),
        name="even_layer",
    )(x3, x3, nw2, nw3, w_in, w2, bg, lbt, gnw, hnw, w_out, jnp.asarray(smat, bf16), jnp.asarray(lvl))


def _odd_attn_body(*refs, C, NB, with_init, pos0):
    if with_init:
        p_ref, nw_ref, s_in, y_ref, s_out = refs
    else:
        p_ref, nw_ref, y_ref, s_out = refs
    R = NB * C
    c = pl.program_id(1)

    @pl.when(c == 0)
    def _():
        if with_init:
            s_out[...] = s_in[...]
        else:
            s_out[...] = jnp.zeros(s_out.shape, f32)

    half = RET_DK // 2
    t_in = lax.broadcasted_iota(jnp.int32, (R, half), 0) % C
    lane = lax.broadcasted_iota(jnp.int32, (R, half), 1).astype(f32)
    pos = (pos0 + c * C + t_in).astype(f32)
    ang = pos * jnp.power(ROPE_BASE, -lane / half)
    cos, sin = jnp.cos(ang), jnp.sin(ang)

    def rotary(x):
        x1, x2 = x[:, :half], x[:, half:]
        return jnp.concatenate([x1 * cos - x2 * sin, x1 * sin + x2 * cos], axis=1)

    ri = lax.broadcasted_iota(jnp.int32, (R, R), 0)
    ci = lax.broadcasted_iota(jnp.int32, (R, R), 1)
    visible = ((ri // C) == (ci // C)) & (ci <= ri)
    dist = jnp.where(visible, ri - ci, 0).astype(f32)
    t_k = (lax.broadcasted_iota(jnp.int32, (R, RET_DK), 0) % C).astype(f32)

    for h in range(N_HEADS):
        log_gamma = math.log(1.0 - 2.0 ** (-5.0 - h))
        q = rotary(p_ref[:, h * RET_DK:(h + 1) * RET_DK])
        k = rotary(p_ref[:, (N_HEADS + h) * RET_DK:(N_HEADS + h + 1) * RET_DK]) * (RET_DK ** -0.5)
        v = p_ref[:, 2 * N_HEADS * RET_DK + h * RET_DV:2 * N_HEADS * RET_DK + (h + 1) * RET_DV]
        gate = p_ref[:, 2 * N_HEADS * RET_DK + (N_HEADS + h) * RET_DV:2 * N_HEADS * RET_DK + (N_HEADS + h + 1) * RET_DV]
        vb = v.astype(bf16)

        decay = jnp.where(visible, jnp.exp(dist * log_gamma), 0.0)
        s = lax.dot_general(q.astype(bf16), k.astype(bf16), _NT, preferred_element_type=f32) * decay
        o = jnp.dot(s.astype(bf16), vb, preferred_element_type=f32)

        q_dec = (q * jnp.exp((t_k + 1.0) * log_gamma)).astype(bf16)
        k_dec = (k * jnp.exp((C - 1.0 - t_k) * log_gamma)).astype(bf16)
        parts = []
        for n in range(NB):
            rows = slice(n * C, (n + 1) * C)
            st = s_out[n, h]
            parts.append(jnp.dot(q_dec[rows], st.astype(bf16), preferred_element_type=f32))
            upd = lax.dot_general(k_dec[rows], vb[rows], _TN, preferred_element_type=f32)
            s_out[n, h] = st * math.exp(C * log_gamma) + upd
        o = o + (parts[0] if NB == 1 else jnp.concatenate(parts, axis=0))

        y = _rms(o, nw_ref[...]) * (gate * _sigmoid(gate))
        y_ref[:, h * RET_DV:(h + 1) * RET_DV] = y.astype(bf16)


def _odd_attn(p, nw, s, *, n_seq, seq_len, C, NB, pos0):
    with_init = s is not None
    R = NB * C
    nc = seq_len // C
    row_map = lambda i, c: (i * nc + c, 0)
    s_spec = pl.BlockSpec((NB, N_HEADS, RET_DK, RET_DV), lambda i, c: (i, 0, 0, 0))
    in_specs = [pl.BlockSpec((R, ODD_P), row_map), _const_spec((1, RET_DV))]
    args = [p, nw]
    if with_init:
        in_specs.append(s_spec)
        args.append(s)
    return pl.pallas_call(
        functools.partial(_odd_attn_body, C=C, NB=NB, with_init=with_init, pos0=pos0),
        out_shape=(
            jax.ShapeDtypeStruct((n_seq * seq_len, N_HEADS * RET_DV), bf16),
            jax.ShapeDtypeStruct((n_seq, N_HEADS, RET_DK, RET_DV), f32),
        ),
        grid=(n_seq // NB, nc),
        in_specs=in_specs,
        out_specs=(pl.BlockSpec((R, N_HEADS * RET_DV), row_map), s_spec),
        compiler_params=_params("arbitrary", "arbitrary"),
        name="odd_attn",
    )(*args)


def _pad_heads(w, dk):
    lead = w.shape[:-1]
    w = w.reshape(*lead, N_HEADS, dk)
    w = jnp.pad(w, [(0, 0)] * len(lead) + [(0, 0), (0, LANES - dk)])
    return w.reshape(*lead, N_HEADS * LANES)


def _pack_even_w_in(w):
    hk, hv = N_HEADS * GLA_DK, N_HEADS * EVEN_DV
    edges = [0, hk, 2 * hk, 2 * hk + hv, 2 * hk + 2 * hv, 2 * hk + 2 * hv + GLA_LOWRANK]
    q_a, k_a, v_a, r_a, lr = (w[:, edges[i]:edges[i + 1]] for i in range(5))
    rest = w[:, edges[5]:]
    lr = jnp.pad(lr, ((0, 0), (0, LANES - GLA_LOWRANK)))
    return jnp.concatenate([_pad_heads(q_a, GLA_DK), _pad_heads(k_a, GLA_DK), v_a, r_a, rest, lr], axis=1)


def _pack_even_w_in_by_head(w):
    hk, hv = N_HEADS * GLA_DK, N_HEADS * EVEN_DV
    edges = np.cumsum([0, hk, hk, hv, hv, GLA_LOWRANK, hv, hv, hv, hv])
    q_a, k_a, v_a, r_a, lr, q_b, f_b, i_b, g_b = (w[:, edges[i]:edges[i + 1]] for i in range(9))
    tiles = lambda a: a.reshape(a.shape[0], N_HEADS, LANES)
    gla = jnp.stack([tiles(_pad_heads(q_a, GLA_DK)), tiles(_pad_heads(k_a, GLA_DK)), tiles(v_a), tiles(r_a)], axis=2)
    hgrn = jnp.stack([tiles(q_b), tiles(f_b), tiles(i_b), tiles(g_b)], axis=2)
    lr = jnp.pad(lr, ((0, 0), (0, LANES - GLA_LOWRANK)))
    return jnp.concatenate([gla.reshape(w.shape[0], -1), hgrn.reshape(w.shape[0], -1), lr], axis=1)


def _trunk(x, s_gla, s_hgrn, s_ret, w, *, n_seq, seq_len, C, NB_even, NB_odd, pos0):
    nw = w["norm_w"]
    depth = nw.shape[0]
    new_gla, new_hgrn, new_ret = [], [], []
    for l in range(depth):
        row = lambda i, l=l: nw[l, i][None, :]
        x = _ffn(x, row(0), row(1), w["ffn_g"][l, 0], w["ffn_u"][l, 0], w["ffn_d"][l, 0])
        if l % 2 == 0 and s_gla is None and seq_len % CHUNK == 0:
            e = l // 2
            x3, sg, sh = _even_layer_prompt(
                x.reshape(n_seq, seq_len, D_MODEL), row(2), row(3), w["even_in_by_head"][e], w["gla_w2"][e], w["gla_bg"][e],
                w["lb_table"], w["gla_nw"][e], w["hgrn_nw"][e], w["even_out"][e], layer=l)
            x = x3.reshape(n_seq * seq_len, D_MODEL)
            new_gla.append(sg)
            new_hgrn.append(sh)
        elif l % 2 == 0:
            e = l // 2
            p = _proj(x, row(2), w["even_in"][e], tm=512, tn=11 * LANES)
            y, sg, sh = _even_attn(
                p, w["gla_w2"][e], w["gla_bg"][e], w["lb_table"], w["gla_nw"][e], w["hgrn_nw"][e],
                None if s_gla is None else s_gla[e], None if s_hgrn is None else s_hgrn[e],
                n_seq=n_seq, seq_len=seq_len, C=C, NB=NB_even, layer=l)
            new_gla.append(sg)
            new_hgrn.append(sh)
            x = _out_block(y, x, w["even_out"][e], row(3))
        else:
            o = l // 2
            p = _proj(x, row(2), w["odd_in"][o], tm=256, tn=12 * LANES)
            y, sr = _odd_attn(p, w["ret_nw"][o], None if s_ret is None else s_ret[o],
                              n_seq=n_seq, seq_len=seq_len, C=C, NB=NB_odd, pos0=pos0)
            new_ret.append(sr)
            x = _out_block(y, x, w["odd_out"][o], row(3))
        x = _ffn(x, row(4), row(5), w["ffn_g"][l, 1], w["ffn_u"][l, 1], w["ffn_d"][l, 1])
    return x, jnp.stack(new_gla), jnp.stack(new_hgrn), jnp.stack(new_ret)


def kernel(x_prompt, x_sample, state_gla, state_hgrn, state_ret, norm_w, ffn_w_gate, ffn_w_up, ffn_w_down, even_w_in, gla_w_gate2, gla_b_gate, gla_norm_w, hgrn_lb_table, hgrn_norm_w, even_w_out, odd_w_in, ret_norm_w, odd_w_out):
    bp, tp, _ = x_prompt.shape
    bs, ts, _ = x_sample.shape
    n_even = even_w_in.shape[0]
    w = {
        "norm_w": norm_w.astype(f32),
        "ffn_g": ffn_w_gate.astype(bf16),
        "ffn_u": ffn_w_up.astype(bf16),
        "ffn_d": ffn_w_down.astype(bf16),
        "even_in": jnp.stack([_pack_even_w_in(even_w_in[e]) for e in range(n_even)]).astype(bf16),
        "even_in_by_head": jnp.stack([_pack_even_w_in_by_head(even_w_in[e]) for e in range(n_even)]).astype(bf16),
        "gla_w2":jnp.pad(_pad_heads(gla_w_gate2, GLA_DK), ((0, 0), (0, LANES - GLA_LOWRANK), (0, 0))).astype(bf16),
        "gla_bg": _pad_heads(gla_b_gate, GLA_DK)[:, None, :].astype(f32),
        "lb_table": hgrn_lb_table.astype(f32),
        "gla_nw": gla_norm_w[:, None, :].astype(f32),
        "hgrn_nw": hgrn_norm_w[:, None, :].astype(f32),
        "even_out": even_w_out.astype(bf16),
        "odd_in": odd_w_in.astype(bf16),
        "ret_nw": ret_norm_w[:, None, :].astype(f32),
        "odd_out": odd_w_out.astype(bf16),
    }
    y_p, gla_p, hgrn_p, ret_p = _trunk(
        x_prompt.reshape(bp * tp, D_MODEL), None, None, None, w,
        n_seq=bp, seq_len=tp, C=64, NB_even=1, NB_odd=1, pos0=0)
    y_s, gla_s, hgrn_s, ret_s = _trunk(
        x_sample.reshape(bs * ts, D_MODEL), state_gla, state_hgrn, state_ret, w,
        n_seq=bs, seq_len=ts, C=ts, NB_even=8, NB_odd=4, pos0=PAST_LEN)
    return (y_p.reshape(bp, tp, D_MODEL), y_s.reshape(bs, ts, D_MODEL), gla_p, hgrn_p, ret_p, gla_s, hgrn_s, ret_s)
```

```python
import functools
import math

import jax
import jax.numpy as jnp
import numpy as np
from jax import lax
from jax.experimental import pallas as pl
from jax.experimental.pallas import tpu as pltpu

f32 = jnp.float32
bf16 = jnp.bfloat16

D_MODEL = 1024
D_FF = 2816
EPS = 1e-6
PAST_LEN = 16384
ROPE_BASE = 10000.0
GLA_TAU = 16.0
N_HEADS = 4
GLA_DK = 64
EVEN_DK = 128
EVEN_DV = 128
GLA_LOWRANK = 16
RET_DK = 256
RET_DV = 512
LANES = 128
SUBLANES = 8
EVEN_P = 33 * LANES
ODD_P = 2 * N_HEADS * RET_DK + 2 * N_HEADS * RET_DV
VMEM_LIMIT = 56 * 1024 * 1024

_NT = (((1,), (1,)), ((), ()))
_TN = (((0,), (0,)), ((), ()))


def _rms(x, w):
    return x * lax.rsqrt(jnp.mean(x * x, axis=-1, keepdims=True) + EPS) * w


def _sigmoid(x):
    return 1.0 / (1.0 + jnp.exp(-x))


def _const_spec(shape):
    return pl.BlockSpec(shape, lambda *_: (0,) * len(shape), pipeline_mode=pl.Buffered(1))


def _params(*sem):
    return pltpu.CompilerParams(dimension_semantics=sem, vmem_limit_bytes=VMEM_LIMIT)


def _ffn_body(x_ref, nwa_ref, nwb_ref, wg_ref, wu_ref, wd_ref, o_ref, acc_ref, *, tf):
    x = x_ref[...]
    hn = _rms(x, nwa_ref[...]).astype(bf16)
    for j in range(D_FF // tf):
        cols = slice(j * tf, (j + 1) * tf)
        g = jnp.dot(hn, wg_ref[:, cols], preferred_element_type=f32)
        u = jnp.dot(hn, wu_ref[:, cols], preferred_element_type=f32)
        a = (g * _sigmoid(g) * u).astype(bf16)
        part = jnp.dot(a, wd_ref[cols, :], preferred_element_type=f32)
        if j == 0:
            acc_ref[...] = part
        else:
            acc_ref[...] += part
    o_ref[...] = x + 0.5 * _rms(acc_ref[...], nwb_ref[...])


def _ffn(x, nwa, nwb, wg, wu, wd, tm=512, tf=256):
    n = x.shape[0]
    return pl.pallas_call(
        functools.partial(_ffn_body, tf=tf),
        out_shape=jax.ShapeDtypeStruct((n, D_MODEL), f32),
        grid=(n // tm,),
        in_specs=[
            pl.BlockSpec((tm, D_MODEL), lambda i: (i, 0)),
            _const_spec((1, D_MODEL)),
            _const_spec((1, D_MODEL)),
            _const_spec((D_MODEL, D_FF)),
            _const_spec((D_MODEL, D_FF)),
            _const_spec((D_FF, D_MODEL)),
        ],
        out_specs=pl.BlockSpec((tm, D_MODEL), lambda i: (i, 0)),
        scratch_shapes=[pltpu.VMEM((tm, D_MODEL), f32)],
        compiler_params=_params("arbitrary"),
        name="ffn",
    )(x, nwa, nwb, wg, wu, wd)


def _proj_body(x_ref, nw_ref, w_ref, o_ref, *, tn):
    hn = _rms(x_ref[...], nw_ref[...]).astype(bf16)
    for j in range(w_ref.shape[1] // tn):
        cols = slice(j * tn, (j + 1) * tn)
        o_ref[:, cols] = jnp.dot(hn, w_ref[:, cols], preferred_element_type=f32)


def _proj(x, nw, w, tm, tn):
    n, p = x.shape[0], w.shape[1]
    return pl.pallas_call(
        functools.partial(_proj_body, tn=tn),
        out_shape=jax.ShapeDtypeStruct((n, p), f32),
        grid=(n // tm,),
        in_specs=[
            pl.BlockSpec((tm, D_MODEL), lambda i: (i, 0)),
            _const_spec((1, D_MODEL)),
            _const_spec((D_MODEL, p)),
        ],
        out_specs=pl.BlockSpec((tm, p), lambda i: (i, 0)),
        compiler_params=_params("arbitrary"),
        name="proj",
    )(x, nw, w)


def _out_body(y_ref, x_ref, w_ref, nw_ref, o_ref):
    m = jnp.dot(y_ref[...], w_ref[...], preferred_element_type=f32)
    o_ref[...] = x_ref[...] + _rms(m, nw_ref[...])


def _out_block(y, x, w, nw, tm=512):
    n, dy = y.shape
    return pl.pallas_call(
        _out_body,
        out_shape=jax.ShapeDtypeStruct((n, D_MODEL), f32),
        grid=(n // tm,),
        in_specs=[
            pl.BlockSpec((tm, dy), lambda i: (i, 0)),
            pl.BlockSpec((tm, D_MODEL), lambda i: (i, 0)),
            _const_spec((dy, D_MODEL)),
            _const_spec((1, D_MODEL)),
        ],
        out_specs=pl.BlockSpec((tm, D_MODEL), lambda i: (i, 0)),
        compiler_params=_params("arbitrary"),
        name="mixer_out",
    )(y, x, w, nw)


def _split3(g):
    hi = g.astype(bf16)
    r1 = g - hi.astype(f32)
    mid = r1.astype(bf16)
    lo = (r1 - mid.astype(f32)).astype(bf16)
    return jnp.concatenate([hi, mid, lo], axis=1)


def _roll_in_groups(x, d):
    r = x.shape[0]
    return pltpu.roll(x.reshape(r // SUBLANES, SUBLANES, LANES), d, 1).reshape(r, LANES)


def _near_diagonal(q, k, b, v, ones):
    r = q.shape[0]
    sub = lax.broadcasted_iota(jnp.int32, (r, LANES), 0) % SUBLANES
    prods = [q * k]
    for d in range(1, SUBLANES):
        decay = jnp.exp(jnp.minimum(b - _roll_in_groups(b, d), 0.0))
        prods.append(q * _roll_in_groups(k, d) * decay)
    rs = jnp.dot(jnp.concatenate(prods, axis=0).astype(bf16), ones, preferred_element_type=f32)
    o = rs[:r] * v
    for d in range(1, SUBLANES):
        o = o + jnp.where(sub >= d, rs[d * r:(d + 1) * r], 0.0) * _roll_in_groups(v, d)
    return o


def _far_blocks(q, k, b, v):
    c = q.shape[0]
    row = lax.broadcasted_iota(jnp.int32, (c, LANES), 0)
    ri = lax.broadcasted_iota(jnp.int32, (c, c), 0)
    ci = lax.broadcasted_iota(jnp.int32, (c, c), 1)
    scores = jnp.zeros((c, c), f32)
    half = SUBLANES
    while 2 * half <= c:
        blk = 2 * half
        ref = jnp.concatenate(
            [jnp.broadcast_to(b[j * blk + half:j * blk + half + 1, :], (blk, LANES)) for j in range(c // blk)], axis=0)
        upper = (row % blk) >= half
        e = jnp.exp(jnp.minimum(jnp.where(upper, b - ref, ref - b), 0.0))
        qe = jnp.where(upper, q * e, 0.0).astype(bf16)
        ke = jnp.where(upper, 0.0, k * e).astype(bf16)
        s = lax.dot_general(qe, ke, _NT, preferred_element_type=f32)
        if blk < c:
            s = jnp.where((ri // blk) == (ci // blk), s, 0.0)
        scores = scores + s
        half = blk
    return jnp.dot(scores.astype(bf16), v.astype(bf16), preferred_element_type=f32)


def _even_attn_body(*refs, C, NB, with_init, layer):
    if with_init:
        (p_ref, w2_ref, bg_ref, lbt_ref, gnw_ref, hnw_ref, sg_in, sh_in, y_ref, sg_out, sh_out, st_ref) = refs
    else:
        (p_ref, w2_ref, bg_ref, lbt_ref, gnw_ref, hnw_ref, y_ref, sg_out, sh_out, st_ref) = refs
    R = NB * C
    c = pl.program_id(1)

    @pl.when(c == 0)
    def _():
        zeros_half = jnp.zeros((GLA_DK, EVEN_DV), f32)
        for n in range(NB):
            for h in range(N_HEADS):
                if with_init:
                    st_ref[n, h] = jnp.concatenate([sg_in[n, h], zeros_half], axis=0).T
                    st_ref[n, N_HEADS + h] = sh_in[n, h].T
                else:
                    st_ref[n, h] = jnp.zeros((EVEN_DV, EVEN_DK), f32)
                    st_ref[n, N_HEADS + h] = jnp.zeros((EVEN_DV, EVEN_DK), f32)

    ri = lax.broadcasted_iota(jnp.int32, (2 * R, R), 0)
    ci = lax.broadcasted_iota(jnp.int32, (2 * R, R), 1)
    rr = jnp.where(ri >= R, ri - R, ri)
    same = (rr // C) == (ci // C)
    sum_mat = jnp.where(same, jnp.where(ri >= R, 1.0, jnp.where(ci <= rr, 1.0, 0.0)), 0.0).astype(bf16)
    ones = jnp.ones((LANES, LANES), bf16)

    lr = p_ref[:, 32 * LANES:33 * LANES].astype(bf16)
    xg = jnp.dot(lr, w2_ref[...], preferred_element_type=f32) + bg_ref[...]
    log_alpha = (jnp.minimum(xg, 0.0) - jnp.log1p(jnp.exp(-jnp.abs(xg)))) * (1.0 / GLA_TAU)

    lbt = lbt_ref[...]
    lbe = jnp.exp(lbt - jnp.max(lbt, axis=0, keepdims=True))
    lb_all = jnp.sum(lbe[:layer + 1], axis=0, keepdims=True) / jnp.sum(lbe, axis=0, keepdims=True)

    for h in range(2 * N_HEADS):
        hs = slice((h % N_HEADS) * LANES, (h % N_HEADS + 1) * LANES)

        def col(base, hs=hs):
            return p_ref[:, base * LANES + hs.start:base * LANES + hs.stop]

        if h < N_HEADS:
            q = col(0) * (GLA_DK ** -0.5)
            k = col(4)
            v = col(8)
            gate = col(12)
            g = log_alpha[:, hs]
            nw = gnw_ref[...]
        else:
            qb = col(16)
            q = qb * _sigmoid(qb)
            lb = lb_all[:, hs]
            f = lb + (1.0 - lb) * _sigmoid(col(20))
            k = 1.0 - f
            g = jnp.log(f)
            v = col(24)
            gate = col(28)
            nw = hnw_ref[...]

        sums = jnp.dot(sum_mat, _split3(g), preferred_element_type=f32)
        sums = sums[:, :LANES] + sums[:, LANES:2 * LANES] + sums[:, 2 * LANES:]
        b, b_last = sums[:R], sums[R:]

        o = _near_diagonal(q, k, b, v, ones)
        q_dec = (q * jnp.exp(b)).astype(bf16)
        k_dec = (k * jnp.exp(b_last - b)).astype(bf16)
        state_decay = jnp.exp(b_last)
        vb = v.astype(bf16)
        parts = []
        for n in range(NB):
            rows = slice(n * C, (n + 1) * C)
            st = st_ref[n, h]
            o_n = lax.dot_general(q_dec[rows], st.astype(bf16), _NT, preferred_element_type=f32)
            if C > SUBLANES:
                o_n = o_n + _far_blocks(q[rows], k[rows], b[rows], v[rows])
            parts.append(o_n)
            upd = lax.dot_general(vb[rows], k_dec[rows], _TN, preferred_element_type=f32)
            st_ref[n, h] = st * state_decay[(n + 1) * C - 1:(n + 1) * C, :] + upd
        o = o + (parts[0] if NB == 1 else jnp.concatenate(parts, axis=0))

        y = _rms(o, nw) * (gate * _sigmoid(gate))
        y_ref[:, h * LANES:(h + 1) * LANES] = y.astype(bf16)

    @pl.when(c == pl.num_programs(1) - 1)
    def _():
        for n in range(NB):
            for h in range(N_HEADS):
                sg_out[n, h] = st_ref[n, h].T[:GLA_DK, :]
                sh_out[n, h] = st_ref[n, N_HEADS + h].T


def _even_attn(p, w2, bg, lbt, gnw, hnw, sg, sh, *, n_seq, seq_len, C, NB, layer):
    with_init = sg is not None
    R = NB * C
    nc = seq_len // C
    grid = (n_seq // NB, nc)
    row_map = lambda i, c: (i * nc + c, 0)
    in_specs = [
        pl.BlockSpec((R, EVEN_P), row_map),
        _const_spec((LANES, N_HEADS * LANES)),
        _const_spec((1, N_HEADS * LANES)),
        _const_spec(lbt.shape),
        _const_spec((1, EVEN_DV)),
        _const_spec((1, EVEN_DV)),
    ]
    args = [p, w2, bg, lbt, gnw, hnw]
    sg_spec = pl.BlockSpec((NB, N_HEADS, GLA_DK, EVEN_DV), lambda i, c: (i, 0, 0, 0))
    sh_spec = pl.BlockSpec((NB, N_HEADS, EVEN_DK, EVEN_DV), lambda i, c: (i, 0, 0, 0))
    if with_init:
        in_specs += [sg_spec, sh_spec]
        args += [sg, sh]
    return pl.pallas_call(
        functools.partial(_even_attn_body, C=C, NB=NB, with_init=with_init, layer=layer),
        out_shape=(
            jax.ShapeDtypeStruct((n_seq * seq_len, 2 * N_HEADS * EVEN_DV), bf16),
            jax.ShapeDtypeStruct((n_seq, N_HEADS, GLA_DK, EVEN_DV), f32),
            jax.ShapeDtypeStruct((n_seq, N_HEADS, EVEN_DK, EVEN_DV), f32),
        ),
        grid=grid,
        in_specs=in_specs,
        out_specs=(pl.BlockSpec((R, 2 * N_HEADS * EVEN_DV), row_map), sg_spec, sh_spec),
        scratch_shapes=[pltpu.VMEM((NB, 2 * N_HEADS, EVEN_DV, EVEN_DK), f32)],
        compiler_params=_params("arbitrary", "arbitrary"),
        name="even_attn",
    )(*args)


CHUNK = 64
N_LEVELS = 6
N_SUM_LEVELS = 3
LOG2E = 1.4426950408889634
EVEN_PW = 36 * LANES


def _chunk_tables():
    t = np.arange(CHUNK)[:, None]
    j = np.arange(CHUNK)[None, :]
    mats = []
    for l in range(N_SUM_LEVELS):
        hs = 1 << l
        m = (t // (2 * hs)) * (2 * hs) + hs
        mats.append(np.where(t >= m, (j > m) & (j <= t), (j > t) & (j <= m)))
    mats.append(j <= t)
    a = np.concatenate(mats, axis=0).astype(np.float32)
    x = t ^ j
    lvl = np.where(j > t, N_LEVELS + 1, np.where(j == t, 0, np.floor(np.log2(np.maximum(x, 1))).astype(np.int64) + 1))
    return np.concatenate([a, a], axis=1), lvl.astype(np.int32)


def _log_decay_sums(g2_list, sum_mat):
    hi = [g.astype(bf16) for g in g2_list]
    lo = [(g - h.astype(f32)).astype(bf16) for g, h in zip(g2_list, hi)]
    rhs = jnp.concatenate([jnp.concatenate(hi, axis=1), jnp.concatenate(lo, axis=1)], axis=0)
    z = jnp.dot(sum_mat, rhs, preferred_element_type=f32)
    return [z[:, i * LANES:(i + 1) * LANES] for i in range(len(g2_list))]


def _level_log_decay(z, b, l):
    if l < N_SUM_LEVELS:
        return z[l * CHUNK:(l + 1) * CHUNK]
    hs = 1 << l
    ref = jnp.concatenate([jnp.broadcast_to(b[m:m + 1], (2 * hs, LANES)) for m in range(hs, CHUNK, 2 * hs)], axis=0)
    d = b - ref
    return jnp.minimum(d, -d)


def _decay_units(qs, ks, vs, sts, zs, level, fillers):
    n = len(qs)
    fillers = list(fillers)
    vbs = [v.astype(bf16) for v in vs]
    bs = [z[N_SUM_LEVELS * CHUNK:] for z in zs]
    scores = [0.0] * n
    for l in range(N_LEVELS):
        es = [jnp.exp2(_level_log_decay(zs[i], bs[i], l)) for i in range(n)]
        ss = [lax.dot_general((qs[i] * es[i]).astype(bf16), (ks[i] * es[i]).astype(bf16), _NT,
                              preferred_element_type=f32) for i in range(n)]
        scores = [jnp.where(level == l + 1, ss[i], scores[i]) for i in range(n)]
        if fillers:
            fillers.pop(0)()
    b_lasts = [b[CHUNK - 1:] for b in bs]
    o_in = [jnp.dot(scores[i].astype(bf16), vbs[i], preferred_element_type=f32) for i in range(n)]
    o_st = [lax.dot_general((qs[i] * jnp.exp2(bs[i])).astype(bf16), sts[i].astype(bf16), _NT,
                            preferred_element_type=f32) for i in range(n)]
    upd = [lax.dot_general(vbs[i], (ks[i] * jnp.exp2(b_lasts[i] - bs[i])).astype(bf16), _TN,
                           preferred_element_type=f32) for i in range(n)]
    while fillers:
        fillers.pop(0)()
    outs = [jnp.sum(qs[i] * ks[i], axis=-1, keepdims=True) * vs[i] + o_in[i] + o_st[i] for i in range(n)]
    return outs, [sts[i] * jnp.exp2(b_lasts[i]) + upd[i] for i in range(n)]


def _even_layer_body(xa_ref, xc_ref, nw2_ref, nw3_ref, win_ref, w2_ref, bg_ref, lbt_ref, gnw_ref, hnw_ref, wout_ref,
                     smat_ref, lvl_ref, o_ref, sg_out, sh_out, p_buf, hn_scr, y_scr, st_ref, *, NB, layer):
    R = NB * CHUNK
    s = pl.program_id(1)
    head_w = 4 * LANES
    xg0 = 2 * N_HEADS * head_w

    @pl.when(s == 0)
    def _():
        p_buf[...] = jnp.zeros((R, EVEN_PW), f32)

    @pl.when(s <= 1)
    def _():
        st_ref[...] = jnp.zeros(st_ref.shape, f32)

    hn_scr[...] = _rms(xa_ref[...].reshape(R, D_MODEL), nw2_ref[...]).astype(bf16)

    lbt = lbt_ref[...]
    lbe = jnp.exp(lbt - jnp.max(lbt, axis=0, keepdims=True))
    lb_all = jnp.sum(lbe[:layer + 1], axis=0, keepdims=True) / jnp.sum(lbe, axis=0, keepdims=True)
    sum_mat = smat_ref[...]
    level = lvl_ref[...]

    def proj_piece(c0, c1):
        def run():
            p_buf[:, c0:c1] = jnp.dot(hn_scr[...], win_ref[:, c0:c1], preferred_element_type=f32)
        return run

    def gate_piece():
        lr = jnp.dot(hn_scr[...], win_ref[:, xg0:], preferred_element_type=f32).astype(bf16)
        p_buf[:, xg0:] = jnp.dot(lr, w2_ref[...], preferred_element_type=f32) + bg_ref[...]

    for h0 in range(0, 2 * N_HEADS, 2):
        nw = gnw_ref[...] if h0 < N_HEADS else hnw_ref[...]
        qs, ks, g2s, vs, gates, ids = [], [], [], [], [], []
        for h in (h0, h0 + 1):
            hh = h % N_HEADS
            for n in range(NB):
                rows = slice(n * CHUNK, (n + 1) * CHUNK)
                q = p_buf[rows, h * head_w:h * head_w + LANES]
                x1 = p_buf[rows, h * head_w + LANES:h * head_w + 2 * LANES]
                if h < N_HEADS:
                    xg = p_buf[rows, xg0 + hh * LANES:xg0 + (hh + 1) * LANES]
                    qs.append(q * (GLA_DK ** -0.5))
                    ks.append(x1)
                    g2s.append((jnp.minimum(xg, 0.0) - jnp.log1p(jnp.exp(-jnp.abs(xg)))) * (LOG2E / GLA_TAU))
                else:
                    lb = lb_all[:, hh * LANES:(hh + 1) * LANES]
                    f = lb + (1.0 - lb) * _sigmoid(x1)
                    qs.append(q * _sigmoid(q))
                    ks.append(1.0 - f)
                    g2s.append(jnp.log2(f))
                vs.append(p_buf[rows, h * head_w + 2 * LANES:h * head_w + 3 * LANES])
                gates.append(p_buf[rows, h * head_w + 3 * LANES:(h + 1) * head_w])
                ids.append((n, h))
        zs = []
        for i in range(0, len(g2s), 2):
            zs += _log_decay_sums(g2s[i:i + 2], sum_mat)
        fillers = [proj_piece(h0 * head_w + i * 2 * LANES, h0 * head_w + (i + 1) * 2 * LANES) for i in range(4)]
        if h0 == N_HEADS - 2:
            fillers.append(gate_piece)
        outs, sts = _decay_units(qs, ks, vs, [st_ref[n, h] for n, h in ids], zs, level, fillers)
        for i, (n, h) in enumerate(ids):
            rows = slice(n * CHUNK, (n + 1) * CHUNK)
            st_ref[n, h] = sts[i]
            y = _rms(outs[i], nw) * (gates[i] * _sigmoid(gates[i]))
            y_scr[rows, h * LANES:(h + 1) * LANES] = y.astype(bf16)

    m = jnp.dot(y_scr[...], wout_ref[...], preferred_element_type=f32)
    o_ref[...] = (xc_ref[...].reshape(R, D_MODEL) + _rms(m, nw3_ref[...])).reshape(NB, CHUNK, D_MODEL)

    @pl.when(s == pl.num_programs(1) - 1)
    def _():
        for n in range(NB):
            for h in range(N_HEADS):
                sg_out[n, h] = st_ref[n, h].T[:GLA_DK, :]
                sh_out[n, h] = st_ref[n, N_HEADS + h].T


def _even_layer_prompt(x3, nw2, nw3, w_in, w2, bg, lbt, gnw, hnw, w_out, *, layer, NB=4):
    n_seq, seq_len, _ = x3.shape
    nc = seq_len // CHUNK
    R = NB * CHUNK
    smat, lvl = _chunk_tables()
    x_blk = (NB, CHUNK, D_MODEL)
    sg_spec = pl.BlockSpec((NB, N_HEADS, GLA_DK, EVEN_DV), lambda i, s: (i, 0, 0, 0))
    sh_spec = pl.BlockSpec((NB, N_HEADS, EVEN_DK, EVEN_DV), lambda i, s: (i, 0, 0, 0))
    return pl.pallas_call(
        functools.partial(_even_layer_body, NB=NB, layer=layer),
        out_shape=(
            jax.ShapeDtypeStruct(x3.shape, f32),
            jax.ShapeDtypeStruct((n_seq, N_HEADS, GLA_DK, EVEN_DV), f32),
            jax.ShapeDtypeStruct((n_seq, N_HEADS, EVEN_DK, EVEN_DV), f32),
        ),
        grid=(n_seq // NB, nc + 1),
        in_specs=[
            pl.BlockSpec(x_blk, lambda i, s: (i, jnp.minimum(s, nc - 1), 0)),
            pl.BlockSpec(x_blk, lambda i, s: (i, jnp.maximum(s - 1, 0), 0)),
            _const_spec((1, D_MODEL)),
            _const_spec((1, D_MODEL)),
            _const_spec(w_in.shape),
            _const_spec(w2.shape),
            _const_spec(bg.shape),
            _const_spec(lbt.shape),
            _const_spec((1, EVEN_DV)),
            _const_spec((1, EVEN_DV)),
            _const_spec(w_out.shape),
            _const_spec(smat.shape),
            _const_spec(lvl.shape),
        ],
        out_specs=(pl.BlockSpec(x_blk, lambda i, s: (i, jnp.maximum(s - 1, 0), 0)), sg_spec, sh_spec),
        scratch_shapes=[
            pltpu.VMEM((R, EVEN_PW), f32),
            pltpu.VMEM((R, D_MODEL), bf16),
            pltpu.VMEM((R, 2 * N_HEADS * EVEN_DV), bf16),
            pltpu.VMEM((NB, 2 * N_HEADS, EVEN_DV, EVEN_DK), f32),
        ],
        compiler_params=_params("arbitrary", "arbitrary"),
        name="even_layer",
    )(x3, x3, nw2, nw3, w_in, w2, bg, lbt, gnw, hnw, w_out, jnp.asarray(smat, bf16), jnp.asarray(lvl))


def _odd_attn_body(*refs, C, NB, with_init, pos0):
    if with_init:
        p_ref, nw_ref, s_in, y_ref, s_out = refs
    else:
        p_ref, nw_ref, y_ref, s_out = refs
    R = NB * C
    c = pl.program_id(1)

    @pl.when(c == 0)
    def _():
        if with_init:
            s_out[...] = s_in[...]
        else:
            s_out[...] = jnp.zeros(s_out.shape, f32)

    half = RET_DK // 2
    t_in = lax.broadcasted_iota(jnp.int32, (R, half), 0) % C
    lane = lax.broadcasted_iota(jnp.int32, (R, half), 1).astype(f32)
    pos = (pos0 + c * C + t_in).astype(f32)
    ang = pos * jnp.power(ROPE_BASE, -lane / half)
    cos, sin = jnp.cos(ang), jnp.sin(ang)

    def rotary(x):
        x1, x2 = x[:, :half], x[:, half:]
        return jnp.concatenate([x1 * cos - x2 * sin, x1 * sin + x2 * cos], axis=1)

    ri = lax.broadcasted_iota(jnp.int32, (R, R), 0)
    ci = lax.broadcasted_iota(jnp.int32, (R, R), 1)
    visible = ((ri // C) == (ci // C)) & (ci <= ri)
    dist = jnp.where(visible, ri - ci, 0).astype(f32)
    t_k = (lax.broadcasted_iota(jnp.int32, (R, RET_DK), 0) % C).astype(f32)

    for h in range(N_HEADS):
        log_gamma = math.log(1.0 - 2.0 ** (-5.0 - h))
        q = rotary(p_ref[:, h * RET_DK:(h + 1) * RET_DK])
        k = rotary(p_ref[:, (N_HEADS + h) * RET_DK:(N_HEADS + h + 1) * RET_DK]) * (RET_DK ** -0.5)
        v = p_ref[:, 2 * N_HEADS * RET_DK + h * RET_DV:2 * N_HEADS * RET_DK + (h + 1) * RET_DV]
        gate = p_ref[:, 2 * N_HEADS * RET_DK + (N_HEADS + h) * RET_DV:2 * N_HEADS * RET_DK + (N_HEADS + h + 1) * RET_DV]
        vb = v.astype(bf16)

        decay = jnp.where(visible, jnp.exp(dist * log_gamma), 0.0)
        s = lax.dot_general(q.astype(bf16), k.astype(bf16), _NT, preferred_element_type=f32) * decay
        o = jnp.dot(s.astype(bf16), vb, preferred_element_type=f32)

        q_dec = (q * jnp.exp((t_k + 1.0) * log_gamma)).astype(bf16)
        k_dec = (k * jnp.exp((C - 1.0 - t_k) * log_gamma)).astype(bf16)
        parts = []
        for n in range(NB):
            rows = slice(n * C, (n + 1) * C)
            st = s_out[n, h]
            parts.append(jnp.dot(q_dec[rows], st.astype(bf16), preferred_element_type=f32))
            upd = lax.dot_general(k_dec[rows], vb[rows], _TN, preferred_element_type=f32)
            s_out[n, h] = st * math.exp(C * log_gamma) + upd
        o = o + (parts[0] if NB == 1 else jnp.concatenate(parts, axis=0))

        y = _rms(o, nw_ref[...]) * (gate * _sigmoid(gate))
        y_ref[:, h * RET_DV:(h + 1) * RET_DV] = y.astype(bf16)


def _odd_attn(p, nw, s, *, n_seq, seq_len, C, NB, pos0):
    with_init = s is not None
    R = NB * C
    nc = seq_len // C
    row_map = lambda i, c: (i * nc + c, 0)
    s_spec = pl.BlockSpec((NB, N_HEADS, RET_DK, RET_DV), lambda i, c: (i, 0, 0, 0))
    in_specs = [pl.BlockSpec((R, ODD_P), row_map), _const_spec((1, RET_DV))]
    args = [p, nw]
    if with_init:
        in_specs.append(s_spec)
        args.append(s)
    return pl.pallas_call(
        functools.partial(_odd_attn_body, C=C, NB=NB, with_init=with_init, pos0=pos0),
        out_shape=(
            jax.ShapeDtypeStruct((n_seq * seq_len, N_HEADS * RET_DV), bf16),
            jax.ShapeDtypeStruct((n_seq, N_HEADS, RET_DK, RET_DV), f32),
        ),
        grid=(n_seq // NB, nc),
        in_specs=in_specs,
        out_specs=(pl.BlockSpec((R, N_HEADS * RET_DV), row_map), s_spec),
        compiler_params=_params("arbitrary", "arbitrary"),
        name="odd_attn",
    )(*args)


RET_CHUNK = 256
RET_HEAD_W = 2 * RET_DK + 2 * RET_DV


def _rope_body(cos_ref, sin_ref, *, pos0):
    n, half = cos_ref.shape
    pos = (pos0 + lax.broadcasted_iota(jnp.int32, (n, half), 0)).astype(f32)
    lane = lax.broadcasted_iota(jnp.int32, (n, half), 1).astype(f32)
    ang = pos * jnp.power(ROPE_BASE, -lane / half)
    cos_ref[...] = jnp.cos(ang)
    sin_ref[...] = jnp.sin(ang)


def _rope_tables(n, pos0):
    shape = jax.ShapeDtypeStruct((n, RET_DK // 2), f32)
    return pl.pallas_call(functools.partial(_rope_body, pos0=pos0), out_shape=(shape, shape), name="rope_tables")()


def _odd_layer_body(xa_ref, xc_ref, cos_ref, sin_ref, nw2_ref, nw3_ref, win_ref, rnw_ref, wout_ref,
                    o_ref, s_out, p_buf, hn_scr, y_scr, dec_scr, *, nc):
    C = RET_CHUNK
    j = pl.program_id(0)
    cb = jnp.maximum(j - 1, 0) % nc
    half = RET_DK // 2

    @pl.when(j == 0)
    def _():
        p_buf[...] = jnp.zeros(p_buf.shape, f32)
        ri = lax.broadcasted_iota(jnp.int32, (C, C), 0)
        ci = lax.broadcasted_iota(jnp.int32, (C, C), 1)
        dist = jnp.maximum(ri - ci, 0).astype(f32)
        for h in range(N_HEADS):
            dec_scr[h] = jnp.where(ci <= ri, jnp.exp2(dist * math.log2(1.0 - 2.0 ** (-5.0 - h))), 0.0)

    @pl.when(cb == 0)
    def _():
        s_out[...] = jnp.zeros(s_out.shape, f32)

    hn_scr[...] = _rms(xa_ref[0], nw2_ref[...]).astype(bf16)
    cos, sin = cos_ref[...], sin_ref[...]

    def rotary(x):
        x1, x2 = x[:, :half], x[:, half:]
        return jnp.concatenate([x1 * cos - x2 * sin, x1 * sin + x2 * cos], axis=1)

    t_k = lax.broadcasted_iota(jnp.int32, (C, RET_DK), 0).astype(f32)
    for h in range(N_HEADS):
        log2_gamma = math.log2(1.0 - 2.0 ** (-5.0 - h))
        base = h * RET_HEAD_W
        q = rotary(p_buf[:, base:base + RET_DK])
        k = rotary(p_buf[:, base + RET_DK:base + 2 * RET_DK]) * (RET_DK ** -0.5)
        vb = p_buf[:, base + 2 * RET_DK:base + 2 * RET_DK + RET_DV].astype(bf16)
        gate = p_buf[:, base + 2 * RET_DK + RET_DV:base + RET_HEAD_W]
        s = lax.dot_general(q.astype(bf16), k.astype(bf16), _NT, preferred_element_type=f32) * dec_scr[h]
        o = jnp.dot(s.astype(bf16), vb, preferred_element_type=f32)
        st = s_out[0, h]
        q_dec = (q * jnp.exp2((t_k + 1.0) * log2_gamma)).astype(bf16)
        o = o + jnp.dot(q_dec, st.astype(bf16), preferred_element_type=f32)
        k_dec = (k * jnp.exp2((C - 1.0 - t_k) * log2_gamma)).astype(bf16)
        upd = lax.dot_general(k_dec, vb, _TN, preferred_element_type=f32)
        s_out[0, h] = st * (2.0 ** (C * log2_gamma)) + upd
        y = _rms(o, rnw_ref[...]) * (gate * _sigmoid(gate))
        y_scr[:, h * RET_DV:(h + 1) * RET_DV] = y.astype(bf16)
        p_buf[:, base:base + RET_HEAD_W] = jnp.dot(hn_scr[...], win_ref[:, base:base + RET_HEAD_W],
                                                   preferred_element_type=f32)

    m = jnp.dot(y_scr[...], wout_ref[...], preferred_element_type=f32)
    o_ref[0] = xc_ref[0] + _rms(m, nw3_ref[...])


def _odd_layer_prompt(x3, nw2, nw3, w_in, rnw, w_out):
    n_seq, seq_len, _ = x3.shape
    C = RET_CHUNK
    nc = seq_len // C
    n_tiles = n_seq * nc
    cos, sin = _rope_tables(seq_len, 0)
    x_blk = (1, C, D_MODEL)
    tile_a = lambda j: (jnp.minimum(j, n_tiles - 1) // nc, jnp.minimum(j, n_tiles - 1) % nc, 0)
    tile_b = lambda j: (jnp.maximum(j - 1, 0) // nc, jnp.maximum(j - 1, 0) % nc, 0)
    rope_spec = pl.BlockSpec((C, RET_DK // 2), lambda j: (jnp.maximum(j - 1, 0) % nc, 0))
    return pl.pallas_call(
        functools.partial(_odd_layer_body, nc=nc),
        out_shape=(
            jax.ShapeDtypeStruct(x3.shape, f32),
            jax.ShapeDtypeStruct((n_seq, N_HEADS, RET_DK, RET_DV), f32),
        ),
        grid=(n_tiles + 1,),
        in_specs=[
            pl.BlockSpec(x_blk, tile_a),
            pl.BlockSpec(x_blk, tile_b),
            rope_spec,
            rope_spec,
            _const_spec((1, D_MODEL)),
            _const_spec((1, D_MODEL)),
            _const_spec(w_in.shape),
            _const_spec((1, RET_DV)),
            _const_spec(w_out.shape),
        ],
        out_specs=(
            pl.BlockSpec(x_blk, tile_b),
            pl.BlockSpec((1, N_HEADS, RET_DK, RET_DV), lambda j: (jnp.maximum(j - 1, 0) // nc, 0, 0, 0)),
        ),
        scratch_shapes=[
            pltpu.VMEM((C, ODD_P), f32),
            pltpu.VMEM((C, D_MODEL), bf16),
            pltpu.VMEM((C, N_HEADS * RET_DV), bf16),
            pltpu.VMEM((N_HEADS, C, C), f32),
        ],
        compiler_params=_params("arbitrary"),
        name="odd_layer",
    )(x3, x3, cos, sin, nw2, nw3, w_in, rnw, w_out)


def _pack_odd_w_in_by_head(w):
    d = w.shape[0]
    q, k = w[:, :N_HEADS * RET_DK], w[:, N_HEADS * RET_DK:2 * N_HEADS * RET_DK]
    v = w[:, 2 * N_HEADS * RET_DK:2 * N_HEADS * RET_DK + N_HEADS * RET_DV]
    g = w[:, 2 * N_HEADS * RET_DK + N_HEADS * RET_DV:]
    parts = [q.reshape(d, N_HEADS, RET_DK), k.reshape(d, N_HEADS, RET_DK),
             v.reshape(d, N_HEADS, RET_DV), g.reshape(d, N_HEADS, RET_DV)]
    return jnp.concatenate(parts, axis=2).reshape(d, N_HEADS * RET_HEAD_W)


def _pad_heads(w, dk):
    lead = w.shape[:-1]
    w = w.reshape(*lead, N_HEADS, dk)
    w = jnp.pad(w, [(0, 0)] * len(lead) + [(0, 0), (0, LANES - dk)])
    return w.reshape(*lead, N_HEADS * LANES)


def _pack_even_w_in(w):
    hk, hv = N_HEADS * GLA_DK, N_HEADS * EVEN_DV
    edges = [0, hk, 2 * hk, 2 * hk + hv, 2 * hk + 2 * hv, 2 * hk + 2 * hv + GLA_LOWRANK]
    q_a, k_a, v_a, r_a, lr = (w[:, edges[i]:edges[i + 1]] for i in range(5))
    rest = w[:, edges[5]:]
    lr = jnp.pad(lr, ((0, 0), (0, LANES - GLA_LOWRANK)))
    return jnp.concatenate([_pad_heads(q_a, GLA_DK), _pad_heads(k_a, GLA_DK), v_a, r_a, rest, lr], axis=1)


def _pack_even_w_in_by_head(w):
    hk, hv = N_HEADS * GLA_DK, N_HEADS * EVEN_DV
    edges = np.cumsum([0, hk, hk, hv, hv, GLA_LOWRANK, hv, hv, hv, hv])
    q_a, k_a, v_a, r_a, lr, q_b, f_b, i_b, g_b = (w[:, edges[i]:edges[i + 1]] for i in range(9))
    tiles = lambda a: a.reshape(a.shape[0], N_HEADS, LANES)
    gla = jnp.stack([tiles(_pad_heads(q_a, GLA_DK)), tiles(_pad_heads(k_a, GLA_DK)), tiles(v_a), tiles(r_a)], axis=2)
    hgrn = jnp.stack([tiles(q_b), tiles(f_b), tiles(i_b), tiles(g_b)], axis=2)
    lr = jnp.pad(lr, ((0, 0), (0, LANES - GLA_LOWRANK)))
    return jnp.concatenate([gla.reshape(w.shape[0], -1), hgrn.reshape(w.shape[0], -1), lr], axis=1)


def _trunk(x, s_gla, s_hgrn, s_ret, w, *, n_seq, seq_len, C, NB_even, NB_odd, pos0):
    nw = w["norm_w"]
    depth = nw.shape[0]
    new_gla, new_hgrn, new_ret = [], [], []
    for l in range(depth):
        row = lambda i, l=l: nw[l, i][None, :]
        x = _ffn(x, row(0), row(1), w["ffn_g"][l, 0], w["ffn_u"][l, 0], w["ffn_d"][l, 0])
        if l % 2 == 0 and s_gla is None and seq_len % CHUNK == 0:
            e = l // 2
            x3, sg, sh = _even_layer_prompt(
                x.reshape(n_seq, seq_len, D_MODEL), row(2), row(3), w["even_in_by_head"][e], w["gla_w2"][e], w["gla_bg"][e],
                w["lb_table"], w["gla_nw"][e], w["hgrn_nw"][e], w["even_out"][e], layer=l)
            x = x3.reshape(n_seq * seq_len, D_MODEL)
            new_gla.append(sg)
            new_hgrn.append(sh)
        elif l % 2 == 0:
            e = l // 2
            p = _proj(x, row(2), w["even_in"][e], tm=512, tn=11 * LANES)
            y, sg, sh = _even_attn(
                p, w["gla_w2"][e], w["gla_bg"][e], w["lb_table"], w["gla_nw"][e], w["hgrn_nw"][e],
                None if s_gla is None else s_gla[e], None if s_hgrn is None else s_hgrn[e],
                n_seq=n_seq, seq_len=seq_len, C=C, NB=NB_even, layer=l)
            new_gla.append(sg)
            new_hgrn.append(sh)
            x = _out_block(y, x, w["even_out"][e], row(3))
        elif s_ret is None and pos0 == 0 and seq_len % RET_CHUNK == 0:
            o = l // 2
            x3, sr = _odd_layer_prompt(x.reshape(n_seq, seq_len, D_MODEL), row(2), row(3), w["odd_in_by_head"][o],
                                       w["ret_nw"][o], w["odd_out"][o])
            x = x3.reshape(n_seq * seq_len, D_MODEL)
            new_ret.append(sr)
        else:
            o = l // 2
            p = _proj(x, row(2), w["odd_in"][o], tm=256, tn=12 * LANES)
            y, sr = _odd_attn(p, w["ret_nw"][o], None if s_ret is None else s_ret[o],
                              n_seq=n_seq, seq_len=seq_len, C=C, NB=NB_odd, pos0=pos0)
            new_ret.append(sr)
            x = _out_block(y, x, w["odd_out"][o], row(3))
        x = _ffn(x, row(4), row(5), w["ffn_g"][l, 1], w["ffn_u"][l, 1], w["ffn_d"][l, 1])
    return x, jnp.stack(new_gla), jnp.stack(new_hgrn), jnp.stack(new_ret)


def kernel(x_prompt, x_sample, state_gla, state_hgrn, state_ret, norm_w, ffn_w_gate, ffn_w_up, ffn_w_down, even_w_in, gla_w_gate2, gla_b_gate, gla_norm_w, hgrn_lb_table, hgrn_norm_w, even_w_out, odd_w_in, ret_norm_w, odd_w_out):
    bp, tp, _ = x_prompt.shape
    bs, ts, _ = x_sample.shape
    n_even = even_w_in.shape[0]
    w = {
        "norm_w": norm_w.astype(f32),
        "ffn_g": ffn_w_gate.astype(bf16),
        "ffn_u": ffn_w_up.astype(bf16),
        "ffn_d": ffn_w_down.astype(bf16),
        "even_in": jnp.stack([_pack_even_w_in(even_w_in[e]) for e in range(n_even)]).astype(bf16),
        "even_in_by_head": jnp.stack([_pack_even_w_in_by_head(even_w_in[e]) for e in range(n_even)]).astype(bf16),
        "gla_w2":jnp.pad(_pad_heads(gla_w_gate2, GLA_DK), ((0, 0), (0, LANES - GLA_LOWRANK), (0, 0))).astype(bf16),
        "gla_bg": _pad_heads(gla_b_gate, GLA_DK)[:, None, :].astype(f32),
        "lb_table": hgrn_lb_table.astype(f32),
        "gla_nw": gla_norm_w[:, None, :].astype(f32),
        "hgrn_nw": hgrn_norm_w[:, None, :].astype(f32),
        "even_out": even_w_out.astype(bf16),
        "odd_in": odd_w_in.astype(bf16),
        "odd_in_by_head": jnp.stack([_pack_odd_w_in_by_head(odd_w_in[o]) for o in range(odd_w_in.shape[0])]).astype(bf16),
        "ret_nw": ret_norm_w[:, None, :].astype(f32),
        "odd_out": odd_w_out.astype(bf16),
    }
    y_p, gla_p, hgrn_p, ret_p = _trunk(
        x_prompt.reshape(bp * tp, D_MODEL), None, None, None, w,
        n_seq=bp, seq_len=tp, C=64, NB_even=1, NB_odd=1, pos0=0)
    y_s, gla_s, hgrn_s, ret_s = _trunk(
        x_sample.reshape(bs * ts, D_MODEL), state_gla, state_hgrn, state_ret, w,
        n_seq=bs, seq_len=ts, C=ts, NB_even=8, NB_odd=4, pos0=PAST_LEN)
    return (y_p.reshape(bp, tp, D_MODEL), y_s.reshape(bs, ts, D_MODEL), gla_p, hgrn_p, ret_p, gla_s, hgrn_s, ret_s)
```

```python
import functools
import math

import jax
import jax.numpy as jnp
import numpy as np
from jax import lax
from jax.experimental import pallas as pl
from jax.experimental.pallas import tpu as pltpu

f32 = jnp.float32
bf16 = jnp.bfloat16

D_MODEL = 1024
D_FF = 2816
EPS = 1e-6
PAST_LEN = 16384
ROPE_BASE = 10000.0
GLA_TAU = 16.0
N_HEADS = 4
GLA_DK = 64
EVEN_DK = 128
EVEN_DV = 128
GLA_LOWRANK = 16
RET_DK = 256
RET_DV = 512
LANES = 128
SUBLANES = 8
T_QA, T_KA, T_VA, T_RA, T_QB, T_FB, T_IB, T_GB, T_LR = 0, 2, 4, 8, 12, 16, 20, 24, 28
EVEN_P = 29 * LANES
ODD_P = 2 * N_HEADS * RET_DK + 2 * N_HEADS * RET_DV
VMEM_LIMIT = 56 * 1024 * 1024

_NT = (((1,), (1,)), ((), ()))
_TN = (((0,), (0,)), ((), ()))


def _rms(x, w):
    return x * lax.rsqrt(jnp.mean(x * x, axis=-1, keepdims=True) + EPS) * w


def _sigmoid(x):
    return 1.0 / (1.0 + jnp.exp(-x))


def _const_spec(shape):
    return pl.BlockSpec(shape, lambda *_: (0,) * len(shape), pipeline_mode=pl.Buffered(1))


def _params(*sem):
    return pltpu.CompilerParams(dimension_semantics=sem, vmem_limit_bytes=VMEM_LIMIT)


def _ffn_tile(x_ref, nwa_ref, nwb_ref, wg_ref, wu_ref, wd_ref, o_ref, acc_ref, tf):
    x = x_ref[...]
    hn = _rms(x, nwa_ref[...]).astype(bf16)
    for j in range(D_FF // tf):
        cols = slice(j * tf, (j + 1) * tf)
        g = jnp.dot(hn, wg_ref[:, cols], preferred_element_type=f32)
        u = jnp.dot(hn, wu_ref[:, cols], preferred_element_type=f32)
        a = (g * _sigmoid(g) * u).astype(bf16)
        part = jnp.dot(a, wd_ref[cols, :], preferred_element_type=f32)
        if j == 0:
            acc_ref[...] = part
        else:
            acc_ref[...] += part
    o_ref[...] = x + 0.5 * _rms(acc_ref[...], nwb_ref[...])


def _ffn_body(xa_ref, xb_ref, nwa_ref, nwb_ref, wg_ref, wu_ref, wd_ref, oa_ref, ob_ref, acc_ref, *, tf, na):
    i = pl.program_id(0)

    @pl.when(i < na)
    def _():
        _ffn_tile(xa_ref, nwa_ref, nwb_ref, wg_ref, wu_ref, wd_ref, oa_ref, acc_ref, tf)

    @pl.when(i >= na)
    def _():
        _ffn_tile(xb_ref, nwa_ref, nwb_ref, wg_ref, wu_ref, wd_ref, ob_ref, acc_ref, tf)


def _ffn(xa, xb, nwa, nwb, wg, wu, wd, tm=512, tf=256):
    na, nb = xa.shape[0] // tm, xb.shape[0] // tm
    spec_a = pl.BlockSpec((tm, D_MODEL), lambda i: (jnp.minimum(i, na - 1), 0))
    spec_b = pl.BlockSpec((tm, D_MODEL), lambda i: (jnp.maximum(i - na, 0), 0))
    return pl.pallas_call(
        functools.partial(_ffn_body, tf=tf, na=na),
        out_shape=(jax.ShapeDtypeStruct(xa.shape, f32), jax.ShapeDtypeStruct(xb.shape, f32)),
        grid=(na + nb,),
        in_specs=[
            spec_a,
            spec_b,
            _const_spec((1, D_MODEL)),
            _const_spec((1, D_MODEL)),
            _const_spec((D_MODEL, D_FF)),
            _const_spec((D_MODEL, D_FF)),
            _const_spec((D_FF, D_MODEL)),
        ],
        out_specs=(spec_a, spec_b),
        scratch_shapes=[pltpu.VMEM((tm, D_MODEL), f32)],
        compiler_params=_params("arbitrary"),
        name="ffn",
    )(xa, xb, nwa, nwb, wg, wu, wd)


def _proj_body(x_ref, nw_ref, w_ref, o_ref, *, tn):
    hn = _rms(x_ref[...], nw_ref[...]).astype(bf16)
    for j in range(w_ref.shape[1] // tn):
        cols = slice(j * tn, (j + 1) * tn)
        o_ref[:, cols] = jnp.dot(hn, w_ref[:, cols], preferred_element_type=f32)


def _proj(x, nw, w, tm, tn):
    n, p = x.shape[0], w.shape[1]
    return pl.pallas_call(
        functools.partial(_proj_body, tn=tn),
        out_shape=jax.ShapeDtypeStruct((n, p), f32),
        grid=(n // tm,),
        in_specs=[
            pl.BlockSpec((tm, D_MODEL), lambda i: (i, 0)),
            _const_spec((1, D_MODEL)),
            _const_spec((D_MODEL, p)),
        ],
        out_specs=pl.BlockSpec((tm, p), lambda i: (i, 0)),
        compiler_params=_params("arbitrary"),
        name="proj",
    )(x, nw, w)


def _out_body(y_ref, x_ref, w_ref, nw_ref, o_ref):
    m = jnp.dot(y_ref[...], w_ref[...], preferred_element_type=f32)
    o_ref[...] = x_ref[...] + _rms(m, nw_ref[...])


def _out_block(y, x, w, nw, tm=512):
    n, dy = y.shape
    return pl.pallas_call(
        _out_body,
        out_shape=jax.ShapeDtypeStruct((n, D_MODEL), f32),
        grid=(n // tm,),
        in_specs=[
            pl.BlockSpec((tm, dy), lambda i: (i, 0)),
            pl.BlockSpec((tm, D_MODEL), lambda i: (i, 0)),
            _const_spec((dy, D_MODEL)),
            _const_spec((1, D_MODEL)),
        ],
        out_specs=pl.BlockSpec((tm, D_MODEL), lambda i: (i, 0)),
        compiler_params=_params("arbitrary"),
        name="mixer_out",
    )(y, x, w, nw)


def _split3(g):
    hi = g.astype(bf16)
    r1 = g - hi.astype(f32)
    mid = r1.astype(bf16)
    lo = (r1 - mid.astype(f32)).astype(bf16)
    return jnp.concatenate([hi, mid, lo], axis=1)


def _roll_in_groups(x, d):
    r = x.shape[0]
    return pltpu.roll(x.reshape(r // SUBLANES, SUBLANES, LANES), d, 1).reshape(r, LANES)


def _near_diagonal(q, k, b, v, ones):
    r = q.shape[0]
    sub = lax.broadcasted_iota(jnp.int32, (r, LANES), 0) % SUBLANES
    prods = [q * k]
    for d in range(1, SUBLANES):
        decay = jnp.exp(jnp.minimum(b - _roll_in_groups(b, d), 0.0))
        prods.append(q * _roll_in_groups(k, d) * decay)
    rs = jnp.dot(jnp.concatenate(prods, axis=0).astype(bf16), ones, preferred_element_type=f32)
    o = rs[:r] * v
    for d in range(1, SUBLANES):
        o = o + jnp.where(sub >= d, rs[d * r:(d + 1) * r], 0.0) * _roll_in_groups(v, d)
    return o


def _far_blocks(q, k, b, v):
    c = q.shape[0]
    row = lax.broadcasted_iota(jnp.int32, (c, LANES), 0)
    ri = lax.broadcasted_iota(jnp.int32, (c, c), 0)
    ci = lax.broadcasted_iota(jnp.int32, (c, c), 1)
    scores = jnp.zeros((c, c), f32)
    half = SUBLANES
    while 2 * half <= c:
        blk = 2 * half
        ref = jnp.concatenate(
            [jnp.broadcast_to(b[j * blk + half:j * blk + half + 1, :], (blk, LANES)) for j in range(c // blk)], axis=0)
        upper = (row % blk) >= half
        e = jnp.exp(jnp.minimum(jnp.where(upper, b - ref, ref - b), 0.0))
        qe = jnp.where(upper, q * e, 0.0).astype(bf16)
        ke = jnp.where(upper, 0.0, k * e).astype(bf16)
        s = lax.dot_general(qe, ke, _NT, preferred_element_type=f32)
        if blk < c:
            s = jnp.where((ri // blk) == (ci // blk), s, 0.0)
        scores = scores + s
        half = blk
    return jnp.dot(scores.astype(bf16), v.astype(bf16), preferred_element_type=f32)


def _even_attn_body(*refs, C, NB, with_init, layer):
    if with_init:
        (p_ref, w2_ref, bg_ref, lbt_ref, gnw_ref, hnw_ref, sg_in, sh_in, y_ref, sg_out, sh_out, st_ref) = refs
    else:
        (p_ref, w2_ref, bg_ref, lbt_ref, gnw_ref, hnw_ref, y_ref, sg_out, sh_out, st_ref) = refs
    R = NB * C
    c = pl.program_id(1)

    @pl.when(c == 0)
    def _():
        zeros_half = jnp.zeros((GLA_DK, EVEN_DV), f32)
        for n in range(NB):
            for h in range(N_HEADS):
                if with_init:
                    halves = [sg_in[n, h], zeros_half] if h % 2 == 0 else [zeros_half, sg_in[n, h]]
                    st_ref[n, h] = jnp.concatenate(halves, axis=0).T
                    st_ref[n, N_HEADS + h] = sh_in[n, h].T
                else:
                    st_ref[n, h] = jnp.zeros((EVEN_DV, EVEN_DK), f32)
                    st_ref[n, N_HEADS + h] = jnp.zeros((EVEN_DV, EVEN_DK), f32)

    ri = lax.broadcasted_iota(jnp.int32, (2 * R, R), 0)
    ci = lax.broadcasted_iota(jnp.int32, (2 * R, R), 1)
    rr = jnp.where(ri >= R, ri - R, ri)
    same = (rr // C) == (ci // C)
    sum_mat = jnp.where(same, jnp.where(ri >= R, 1.0, jnp.where(ci <= rr, 1.0, 0.0)), 0.0).astype(bf16)
    ones = jnp.ones((LANES, LANES), bf16)

    lr = p_ref[:, T_LR * LANES:(T_LR + 1) * LANES].astype(bf16)
    xg = jnp.dot(lr, w2_ref[...], preferred_element_type=f32) + bg_ref[...]
    log_alpha = (jnp.minimum(xg, 0.0) - jnp.log1p(jnp.exp(-jnp.abs(xg)))) * (1.0 / GLA_TAU)
    low_half = lax.broadcasted_iota(jnp.int32, (R, LANES), 1) < GLA_DK

    lbt = lbt_ref[...]
    lbe = jnp.exp(lbt - jnp.max(lbt, axis=0, keepdims=True))
    lb_all = jnp.sum(lbe[:layer + 1], axis=0, keepdims=True) / jnp.sum(lbe, axis=0, keepdims=True)

    for h in range(2 * N_HEADS):
        hh = h % N_HEADS

        def col(tile):
            return p_ref[:, tile * LANES:(tile + 1) * LANES]

        if h < N_HEADS:
            mine = low_half if h % 2 == 0 else jnp.logical_not(low_half)
            q = jnp.where(mine, col(T_QA + h // 2), 0.0) * (GLA_DK ** -0.5)
            k = jnp.where(mine, col(T_KA + h // 2), 0.0)
            v = col(T_VA + h)
            gate = col(T_RA + h)
            g = log_alpha[:, (h // 2) * LANES:(h // 2 + 1) * LANES]
            nw = gnw_ref[...]
        else:
            qb = col(T_QB + hh)
            q = qb * _sigmoid(qb)
            lb = lb_all[:, hh * LANES:(hh + 1) * LANES]
            f = lb + (1.0 - lb) * _sigmoid(col(T_FB + hh))
            k = 1.0 - f
            g = jnp.log(f)
            v = col(T_IB + hh)
            gate = col(T_GB + hh)
            nw = hnw_ref[...]

        sums = jnp.dot(sum_mat, _split3(g), preferred_element_type=f32)
        sums = sums[:, :LANES] + sums[:, LANES:2 * LANES] + sums[:, 2 * LANES:]
        b, b_last = sums[:R], sums[R:]

        o = _near_diagonal(q, k, b, v, ones)
        q_dec = (q * jnp.exp(b)).astype(bf16)
        k_dec = (k * jnp.exp(b_last - b)).astype(bf16)
        state_decay = jnp.exp(b_last)
        vb = v.astype(bf16)
        parts = []
        for n in range(NB):
            rows = slice(n * C, (n + 1) * C)
            st = st_ref[n, h]
            o_n = lax.dot_general(q_dec[rows], st.astype(bf16), _NT, preferred_element_type=f32)
            if C > SUBLANES:
                o_n = o_n + _far_blocks(q[rows], k[rows], b[rows], v[rows])
            parts.append(o_n)
            upd = lax.dot_general(vb[rows], k_dec[rows], _TN, preferred_element_type=f32)
            st_ref[n, h] = st * state_decay[(n + 1) * C - 1:(n + 1) * C, :] + upd
        o = o + (parts[0] if NB == 1 else jnp.concatenate(parts, axis=0))

        y = _rms(o, nw) * (gate * _sigmoid(gate))
        y_ref[:, h * LANES:(h + 1) * LANES] = y.astype(bf16)

    @pl.when(c == pl.num_programs(1) - 1)
    def _():
        for n in range(NB):
            for h in range(N_HEADS):
                sg_out[n, h] = st_ref[n, h].T[(h % 2) * GLA_DK:(h % 2 + 1) * GLA_DK, :]
                sh_out[n, h] = st_ref[n, N_HEADS + h].T


def _even_attn(p, w2, bg, lbt, gnw, hnw, sg, sh, *, n_seq, seq_len, C, NB, layer):
    with_init = sg is not None
    R = NB * C
    nc = seq_len // C
    grid = (n_seq // NB, nc)
    row_map = lambda i, c: (i * nc + c, 0)
    in_specs = [
        pl.BlockSpec((R, EVEN_P), row_map),
        _const_spec(w2.shape),
        _const_spec(bg.shape),
        _const_spec(lbt.shape),
        _const_spec((1, EVEN_DV)),
        _const_spec((1, EVEN_DV)),
    ]
    args = [p, w2, bg, lbt, gnw, hnw]
    sg_spec = pl.BlockSpec((NB, N_HEADS, GLA_DK, EVEN_DV), lambda i, c: (i, 0, 0, 0))
    sh_spec = pl.BlockSpec((NB, N_HEADS, EVEN_DK, EVEN_DV), lambda i, c: (i, 0, 0, 0))
    if with_init:
        in_specs += [sg_spec, sh_spec]
        args += [sg, sh]
    return pl.pallas_call(
        functools.partial(_even_attn_body, C=C, NB=NB, with_init=with_init, layer=layer),
        out_shape=(
            jax.ShapeDtypeStruct((n_seq * seq_len, 2 * N_HEADS * EVEN_DV), bf16),
            jax.ShapeDtypeStruct((n_seq, N_HEADS, GLA_DK, EVEN_DV), f32),
            jax.ShapeDtypeStruct((n_seq, N_HEADS, EVEN_DK, EVEN_DV), f32),
        ),
        grid=grid,
        in_specs=in_specs,
        out_specs=(pl.BlockSpec((R, 2 * N_HEADS * EVEN_DV), row_map), sg_spec, sh_spec),
        scratch_shapes=[pltpu.VMEM((NB, 2 * N_HEADS, EVEN_DV, EVEN_DK), f32)],
        compiler_params=_params("arbitrary", "arbitrary"),
        name="even_attn",
    )(*args)


CHUNK = 64
N_LEVELS = 6
N_SUM_LEVELS = 3
LOG2E = 1.4426950408889634
EVEN_PW = (T_LR + 2) * LANES


def _chunk_tables():
    t = np.arange(CHUNK)[:, None]
    j = np.arange(CHUNK)[None, :]
    mats = []
    for l in range(N_SUM_LEVELS):
        hs = 1 << l
        m = (t // (2 * hs)) * (2 * hs) + hs
        mats.append(np.where(t >= m, (j > m) & (j <= t), (j > t) & (j <= m)))
    mats.append(j <= t)
    a = np.concatenate(mats, axis=0).astype(np.float32)
    x = t ^ j
    lvl = np.where(j > t, N_LEVELS + 1, np.where(j == t, 0, np.floor(np.log2(np.maximum(x, 1))).astype(np.int64) + 1))
    return np.concatenate([a, a], axis=1), lvl.astype(np.int32)


def _log_decay_sums(g2_list, sum_mat):
    hi = [g.astype(bf16) for g in g2_list]
    lo = [(g - h.astype(f32)).astype(bf16) for g, h in zip(g2_list, hi)]
    rhs = jnp.concatenate([jnp.concatenate(hi, axis=1), jnp.concatenate(lo, axis=1)], axis=0)
    z = jnp.dot(sum_mat, rhs, preferred_element_type=f32)
    return [z[:, i * LANES:(i + 1) * LANES] for i in range(len(g2_list))]


def _level_log_decay(z, b, l):
    if l < N_SUM_LEVELS:
        return z[l * CHUNK:(l + 1) * CHUNK]
    hs = 1 << l
    ref = jnp.concatenate([jnp.broadcast_to(b[m:m + 1], (2 * hs, LANES)) for m in range(hs, CHUNK, 2 * hs)], axis=0)
    d = b - ref
    return jnp.minimum(d, -d)


def _decay_units(qs, ks, vs, sts, zs, level, fillers):
    n = len(qs)
    fillers = list(fillers)
    vbs = [v.astype(bf16) for v in vs]
    bs = [z[N_SUM_LEVELS * CHUNK:] for z in zs]
    scores = [0.0] * n
    for l in range(N_LEVELS):
        es = [jnp.exp2(_level_log_decay(zs[i], bs[i], l)) for i in range(n)]
        ss = [lax.dot_general((qs[i] * es[i]).astype(bf16), (ks[i] * es[i]).astype(bf16), _NT,
                              preferred_element_type=f32) for i in range(n)]
        scores = [jnp.where(level == l + 1, ss[i], scores[i]) for i in range(n)]
        if fillers:
            fillers.pop(0)()
    b_lasts = [b[CHUNK - 1:] for b in bs]
    o_in = [jnp.dot(scores[i].astype(bf16), vbs[i], preferred_element_type=f32) for i in range(n)]
    o_st = [lax.dot_general((qs[i] * jnp.exp2(bs[i])).astype(bf16), sts[i].astype(bf16), _NT,
                            preferred_element_type=f32) for i in range(n)]
    upd = [lax.dot_general(vbs[i], (ks[i] * jnp.exp2(b_lasts[i] - bs[i])).astype(bf16), _TN,
                           preferred_element_type=f32) for i in range(n)]
    while fillers:
        fillers.pop(0)()
    outs = [jnp.sum(qs[i] * ks[i], axis=-1, keepdims=True) * vs[i] + o_in[i] + o_st[i] for i in range(n)]
    return outs, [sts[i] * jnp.exp2(b_lasts[i]) + upd[i] for i in range(n)]


def _even_layer_body(xa_ref, xc_ref, nw2_ref, nw3_ref, win_ref, w2_ref, bg_ref, lbt_ref, gnw_ref, hnw_ref, wout_ref,
                     smat_ref, lvl_ref, o_ref, sg_out, sh_out, p_buf, hn_scr, y_scr, st_ref, *, NB, nc, layer):
    R = NB * CHUNK
    j = pl.program_id(0)
    cb = jnp.maximum(j - 1, 0) % nc
    xg0 = T_LR * LANES

    @pl.when(j == 0)
    def _():
        p_buf[...] = jnp.zeros(p_buf.shape, f32)

    @pl.when(cb == 0)
    def _():
        st_ref[...] = jnp.zeros(st_ref.shape, f32)

    hn_scr[...] = _rms(xa_ref[...].reshape(R, D_MODEL), nw2_ref[...]).astype(bf16)

    lbt = lbt_ref[...]
    lbe = jnp.exp(lbt - jnp.max(lbt, axis=0, keepdims=True))
    lb_all = jnp.sum(lbe[:layer + 1], axis=0, keepdims=True) / jnp.sum(lbe, axis=0, keepdims=True)
    sum_mat = smat_ref[...]
    level = lvl_ref[...]

    low_half = lax.broadcasted_iota(jnp.int32, (CHUNK, LANES), 1) < GLA_DK

    def proj_piece(tile):
        def run():
            cols = slice(tile * LANES, (tile + 2) * LANES)
            p_buf[:, cols] = jnp.dot(hn_scr[...], win_ref[:, cols], preferred_element_type=f32)
        return run

    def gate_piece():
        lr = jnp.dot(hn_scr[...], win_ref[:, xg0:], preferred_element_type=f32).astype(bf16)
        p_buf[:, xg0:] = jnp.dot(lr, w2_ref[...], preferred_element_type=f32) + bg_ref[...]

    def tile(rows, t):
        return p_buf[rows, t * LANES:(t + 1) * LANES]

    for h0 in range(0, 2 * N_HEADS, 2):
        nw = gnw_ref[...] if h0 < N_HEADS else hnw_ref[...]
        qs, ks, g2s, vs, gates, ids = [], [], [], [], [], []
        for h in (h0, h0 + 1):
            hh = h % N_HEADS
            for n in range(NB):
                rows = slice(n * CHUNK, (n + 1) * CHUNK)
                if h < N_HEADS:
                    mine = low_half if h % 2 == 0 else jnp.logical_not(low_half)
                    xg = tile(rows, T_LR + h // 2)
                    qs.append(jnp.where(mine, tile(rows, T_QA + h // 2), 0.0) * (GLA_DK ** -0.5))
                    ks.append(jnp.where(mine, tile(rows, T_KA + h // 2), 0.0))
                    g2s.append((jnp.minimum(xg, 0.0) - jnp.log1p(jnp.exp(-jnp.abs(xg)))) * (LOG2E / GLA_TAU))
                    vs.append(tile(rows, T_VA + h))
                    gates.append(tile(rows, T_RA + h))
                else:
                    lb = lb_all[:, hh * LANES:(hh + 1) * LANES]
                    f = lb + (1.0 - lb) * _sigmoid(tile(rows, T_FB + hh))
                    q = tile(rows, T_QB + hh)
                    qs.append(q * _sigmoid(q))
                    ks.append(1.0 - f)
                    g2s.append(jnp.log2(f))
                    vs.append(tile(rows, T_IB + hh))
                    gates.append(tile(rows, T_GB + hh))
                ids.append((n, h))
        zs = []
        for i in range(0, len(g2s), 2):
            zs += _log_decay_sums(g2s[i:i + 2], sum_mat)
        if h0 == 0:
            fillers = [proj_piece(T_VA), proj_piece(T_RA)]
        elif h0 == 2:
            fillers = [proj_piece(T_VA + 2), proj_piece(T_RA + 2), proj_piece(T_QA), proj_piece(T_KA), gate_piece]
        else:
            fillers = [proj_piece(t + h0 - N_HEADS) for t in (T_QB, T_FB, T_IB, T_GB)]
        outs, sts = _decay_units(qs, ks, vs, [st_ref[n, h] for n, h in ids], zs, level, fillers)
        for i, (n, h) in enumerate(ids):
            rows = slice(n * CHUNK, (n + 1) * CHUNK)
            st_ref[n, h] = sts[i]
            y = _rms(outs[i], nw) * (gates[i] * _sigmoid(gates[i]))
            y_scr[rows, h * LANES:(h + 1) * LANES] = y.astype(bf16)

    m = jnp.dot(y_scr[...], wout_ref[...], preferred_element_type=f32)
    o_ref[...] = (xc_ref[...].reshape(R, D_MODEL) + _rms(m, nw3_ref[...])).reshape(NB, CHUNK, D_MODEL)

    @pl.when((cb == nc - 1) & (j >= 1))
    def _():
        for n in range(NB):
            for h in range(N_HEADS):
                sg_out[n, h] = st_ref[n, h].T[(h % 2) * GLA_DK:(h % 2 + 1) * GLA_DK, :]
                sh_out[n, h] = st_ref[n, N_HEADS + h].T


def _even_layer_prompt(x3, nw2, nw3, w_in, w2, bg, lbt, gnw, hnw, w_out, *, layer, NB=4):
    n_seq, seq_len, _ = x3.shape
    nc = seq_len // CHUNK
    n_tiles = (n_seq // NB) * nc
    R = NB * CHUNK
    smat, lvl = _chunk_tables()
    x_blk = (NB, CHUNK, D_MODEL)

    def tile(lag):
        def index(j):
            t = jnp.clip(j - lag, 0, n_tiles - 1)
            return (t // nc, t % nc, 0)
        return index

    group = lambda j: (jnp.maximum(j - 1, 0) // nc, 0, 0, 0)
    return pl.pallas_call(
        functools.partial(_even_layer_body, NB=NB, nc=nc, layer=layer),
        out_shape=(
            jax.ShapeDtypeStruct(x3.shape, f32),
            jax.ShapeDtypeStruct((n_seq, N_HEADS, GLA_DK, EVEN_DV), f32),
            jax.ShapeDtypeStruct((n_seq, N_HEADS, EVEN_DK, EVEN_DV), f32),
        ),
        grid=(n_tiles + 1,),
        in_specs=[
            pl.BlockSpec(x_blk, tile(0)),
            pl.BlockSpec(x_blk, tile(1)),
            _const_spec((1, D_MODEL)),
            _const_spec((1, D_MODEL)),
            _const_spec(w_in.shape),
            _const_spec(w2.shape),
            _const_spec(bg.shape),
            _const_spec(lbt.shape),
            _const_spec((1, EVEN_DV)),
            _const_spec((1, EVEN_DV)),
            _const_spec(w_out.shape),
            _const_spec(smat.shape),
            _const_spec(lvl.shape),
        ],
        out_specs=(
            pl.BlockSpec(x_blk, tile(1)),
            pl.BlockSpec((NB, N_HEADS, GLA_DK, EVEN_DV), group),
            pl.BlockSpec((NB, N_HEADS, EVEN_DK, EVEN_DV), group),
        ),
        scratch_shapes=[
            pltpu.VMEM((R, EVEN_PW), f32),
            pltpu.VMEM((R, D_MODEL), bf16),
            pltpu.VMEM((R, 2 * N_HEADS * EVEN_DV), bf16),
            pltpu.VMEM((NB, 2 * N_HEADS, EVEN_DV, EVEN_DK), f32),
        ],
        compiler_params=_params("arbitrary"),
        name="even_layer",
    )(x3, x3, nw2, nw3, w_in, w2, bg, lbt, gnw, hnw, w_out, jnp.asarray(smat, bf16), jnp.asarray(lvl))


def _odd_attn_body(*refs, C, NB, with_init, pos0):
    if with_init:
        p_ref, nw_ref, s_in, y_ref, s_out = refs
    else:
        p_ref, nw_ref, y_ref, s_out = refs
    R = NB * C
    c = pl.program_id(1)

    @pl.when(c == 0)
    def _():
        if with_init:
            s_out[...] = s_in[...]
        else:
            s_out[...] = jnp.zeros(s_out.shape, f32)

    half = RET_DK // 2
    t_in = lax.broadcasted_iota(jnp.int32, (R, half), 0) % C
    lane = lax.broadcasted_iota(jnp.int32, (R, half), 1).astype(f32)
    pos = (pos0 + c * C + t_in).astype(f32)
    ang = pos * jnp.power(ROPE_BASE, -lane / half)
    cos, sin = jnp.cos(ang), jnp.sin(ang)

    def rotary(x):
        x1, x2 = x[:, :half], x[:, half:]
        return jnp.concatenate([x1 * cos - x2 * sin, x1 * sin + x2 * cos], axis=1)

    ri = lax.broadcasted_iota(jnp.int32, (R, R), 0)
    ci = lax.broadcasted_iota(jnp.int32, (R, R), 1)
    visible = ((ri // C) == (ci // C)) & (ci <= ri)
    dist = jnp.where(visible, ri - ci, 0).astype(f32)
    t_k = (lax.broadcasted_iota(jnp.int32, (R, RET_DK), 0) % C).astype(f32)

    for h in range(N_HEADS):
        log_gamma = math.log(1.0 - 2.0 ** (-5.0 - h))
        q = rotary(p_ref[:, h * RET_DK:(h + 1) * RET_DK])
        k = rotary(p_ref[:, (N_HEADS + h) * RET_DK:(N_HEADS + h + 1) * RET_DK]) * (RET_DK ** -0.5)
        v = p_ref[:, 2 * N_HEADS * RET_DK + h * RET_DV:2 * N_HEADS * RET_DK + (h + 1) * RET_DV]
        gate = p_ref[:, 2 * N_HEADS * RET_DK + (N_HEADS + h) * RET_DV:2 * N_HEADS * RET_DK + (N_HEADS + h + 1) * RET_DV]
        vb = v.astype(bf16)

        decay = jnp.where(visible, jnp.exp(dist * log_gamma), 0.0)
        s = lax.dot_general(q.astype(bf16), k.astype(bf16), _NT, preferred_element_type=f32) * decay
        o = jnp.dot(s.astype(bf16), vb, preferred_element_type=f32)

        q_dec = (q * jnp.exp((t_k + 1.0) * log_gamma)).astype(bf16)
        k_dec = (k * jnp.exp((C - 1.0 - t_k) * log_gamma)).astype(bf16)
        parts = []
        for n in range(NB):
            rows = slice(n * C, (n + 1) * C)
            st = s_out[n, h]
            parts.append(jnp.dot(q_dec[rows], st.astype(bf16), preferred_element_type=f32))
            upd = lax.dot_general(k_dec[rows], vb[rows], _TN, preferred_element_type=f32)
            s_out[n, h] = st * math.exp(C * log_gamma) + upd
        o = o + (parts[0] if NB == 1 else jnp.concatenate(parts, axis=0))

        y = _rms(o, nw_ref[...]) * (gate * _sigmoid(gate))
        y_ref[:, h * RET_DV:(h + 1) * RET_DV] = y.astype(bf16)


def _odd_attn(p, nw, s, *, n_seq, seq_len, C, NB, pos0):
    with_init = s is not None
    R = NB * C
    nc = seq_len // C
    row_map = lambda i, c: (i * nc + c, 0)
    s_spec = pl.BlockSpec((NB, N_HEADS, RET_DK, RET_DV), lambda i, c: (i, 0, 0, 0))
    in_specs = [pl.BlockSpec((R, ODD_P), row_map), _const_spec((1, RET_DV))]
    args = [p, nw]
    if with_init:
        in_specs.append(s_spec)
        args.append(s)
    return pl.pallas_call(
        functools.partial(_odd_attn_body, C=C, NB=NB, with_init=with_init, pos0=pos0),
        out_shape=(
            jax.ShapeDtypeStruct((n_seq * seq_len, N_HEADS * RET_DV), bf16),
            jax.ShapeDtypeStruct((n_seq, N_HEADS, RET_DK, RET_DV), f32),
        ),
        grid=(n_seq // NB, nc),
        in_specs=in_specs,
        out_specs=(pl.BlockSpec((R, N_HEADS * RET_DV), row_map), s_spec),
        compiler_params=_params("arbitrary", "arbitrary"),
        name="odd_attn",
    )(*args)


RET_CHUNK = 256


def _rope_body(cos_ref, sin_ref, *, pos0):
    n, half = cos_ref.shape
    pos = (pos0 + lax.broadcasted_iota(jnp.int32, (n, half), 0)).astype(f32)
    lane = lax.broadcasted_iota(jnp.int32, (n, half), 1).astype(f32)
    ang = pos * jnp.power(ROPE_BASE, -lane / half)
    cos_ref[...] = jnp.cos(ang)
    sin_ref[...] = jnp.sin(ang)


def _rope_tables(n, pos0):
    shape = jax.ShapeDtypeStruct((n, RET_DK // 2), f32)
    return pl.pallas_call(functools.partial(_rope_body, pos0=pos0), out_shape=(shape, shape), name="rope_tables")()


def _odd_layer_body(xa_ref, xc_ref, cos_ref, sin_ref, nw2_ref, nw3_ref, win_ref, rnw_ref, wout_ref,
                    o_ref, s_out, p_buf, hn_scr, y_scr, dec_scr, *, nc):
    C = RET_CHUNK
    j = pl.program_id(0)
    cb = jnp.maximum(j - 1, 0) % nc
    half = RET_DK // 2

    @pl.when(j == 0)
    def _():
        p_buf[...] = jnp.zeros(p_buf.shape, f32)
        ri = lax.broadcasted_iota(jnp.int32, (C, C), 0)
        ci = lax.broadcasted_iota(jnp.int32, (C, C), 1)
        dist = jnp.maximum(ri - ci, 0).astype(f32)
        for h in range(N_HEADS):
            dec_scr[h] = jnp.where(ci <= ri, jnp.exp2(dist * math.log2(1.0 - 2.0 ** (-5.0 - h))), 0.0)

    @pl.when(cb == 0)
    def _():
        s_out[...] = jnp.zeros(s_out.shape, f32)

    hn_scr[...] = _rms(xa_ref[0], nw2_ref[...]).astype(bf16)
    cos, sin = cos_ref[...], sin_ref[...]

    def rotary(x):
        x1, x2 = x[:, :half], x[:, half:]
        return jnp.concatenate([x1 * cos - x2 * sin, x1 * sin + x2 * cos], axis=1)

    t_k = lax.broadcasted_iota(jnp.int32, (C, RET_DK), 0).astype(f32)
    for h in range(N_HEADS):
        log2_gamma = math.log2(1.0 - 2.0 ** (-5.0 - h))
        cols = [slice(off + h * w, off + (h + 1) * w) for off, w in (
            (0, RET_DK), (N_HEADS * RET_DK, RET_DK), (2 * N_HEADS * RET_DK, RET_DV),
            (2 * N_HEADS * RET_DK + N_HEADS * RET_DV, RET_DV))]
        q = rotary(p_buf[:, cols[0]])
        k = rotary(p_buf[:, cols[1]]) * (RET_DK ** -0.5)
        vb = p_buf[:, cols[2]].astype(bf16)
        gate = p_buf[:, cols[3]]
        s = lax.dot_general(q.astype(bf16), k.astype(bf16), _NT, preferred_element_type=f32) * dec_scr[h]
        o = jnp.dot(s.astype(bf16), vb, preferred_element_type=f32)
        st = s_out[0, h]
        q_dec = (q * jnp.exp2((t_k + 1.0) * log2_gamma)).astype(bf16)
        o = o + jnp.dot(q_dec, st.astype(bf16), preferred_element_type=f32)
        k_dec = (k * jnp.exp2((C - 1.0 - t_k) * log2_gamma)).astype(bf16)
        upd = lax.dot_general(k_dec, vb, _TN, preferred_element_type=f32)
        s_out[0, h] = st * (2.0 ** (C * log2_gamma)) + upd
        y = _rms(o, rnw_ref[...]) * (gate * _sigmoid(gate))
        y_scr[:, h * RET_DV:(h + 1) * RET_DV] = y.astype(bf16)
        for cs in cols:
            p_buf[:, cs] = jnp.dot(hn_scr[...], win_ref[:, cs], preferred_element_type=f32)

    m = jnp.dot(y_scr[...], wout_ref[...], preferred_element_type=f32)
    o_ref[0] = xc_ref[0] + _rms(m, nw3_ref[...])


def _odd_layer_prompt(x3, nw2, nw3, w_in, rnw, w_out):
    n_seq, seq_len, _ = x3.shape
    C = RET_CHUNK
    nc = seq_len // C
    n_tiles = n_seq * nc
    cos, sin = _rope_tables(seq_len, 0)
    x_blk = (1, C, D_MODEL)
    tile_a = lambda j: (jnp.minimum(j, n_tiles - 1) // nc, jnp.minimum(j, n_tiles - 1) % nc, 0)
    tile_b = lambda j: (jnp.maximum(j - 1, 0) // nc, jnp.maximum(j - 1, 0) % nc, 0)
    rope_spec = pl.BlockSpec((C, RET_DK // 2), lambda j: (jnp.maximum(j - 1, 0) % nc, 0))
    return pl.pallas_call(
        functools.partial(_odd_layer_body, nc=nc),
        out_shape=(
            jax.ShapeDtypeStruct(x3.shape, f32),
            jax.ShapeDtypeStruct((n_seq, N_HEADS, RET_DK, RET_DV), f32),
        ),
        grid=(n_tiles + 1,),
        in_specs=[
            pl.BlockSpec(x_blk, tile_a),
            pl.BlockSpec(x_blk, tile_b),
            rope_spec,
            rope_spec,
            _const_spec((1, D_MODEL)),
            _const_spec((1, D_MODEL)),
            _const_spec(w_in.shape),
            _const_spec((1, RET_DV)),
            _const_spec(w_out.shape),
        ],
        out_specs=(
            pl.BlockSpec(x_blk, tile_b),
            pl.BlockSpec((1, N_HEADS, RET_DK, RET_DV), lambda j: (jnp.maximum(j - 1, 0) // nc, 0, 0, 0)),
        ),
        scratch_shapes=[
            pltpu.VMEM((C, ODD_P), f32),
            pltpu.VMEM((C, D_MODEL), bf16),
            pltpu.VMEM((C, N_HEADS * RET_DV), bf16),
            pltpu.VMEM((N_HEADS, C, C), f32),
        ],
        compiler_params=_params("arbitrary"),
        name="odd_layer",
    )(x3, x3, cos, sin, nw2, nw3, w_in, rnw, w_out)


def _pack_even_w_in(w):
    lr0 = 2 * N_HEADS * GLA_DK + 2 * N_HEADS * EVEN_DV
    lr = jnp.pad(w[:, lr0:lr0 + GLA_LOWRANK], ((0, 0), (0, LANES - GLA_LOWRANK)))
    return jnp.concatenate([w[:, :lr0], w[:, lr0 + GLA_LOWRANK:], lr], axis=1)


def _mixer_prompt(x, l, w, row, n_seq, seq_len):
    x3 = x.reshape(n_seq, seq_len, D_MODEL)
    if l % 2 == 0:
        e = l // 2
        x3, sg, sh = _even_layer_prompt(x3, row(2), row(3), w["even_in"][e], w["gla_w2"][e], w["gla_bg"][e],
                                        w["lb_table"], w["gla_nw"][e], w["hgrn_nw"][e], w["even_out"][e], layer=l)
        new = (sg, sh)
    else:
        o = l // 2
        x3, sr = _odd_layer_prompt(x3, row(2), row(3), w["odd_in"][o], w["ret_nw"][o], w["odd_out"][o])
        new = (sr,)
    return x3.reshape(n_seq * seq_len, D_MODEL), new


def _mixer_sample(x, l, w, row, s_gla, s_hgrn, s_ret, n_seq, seq_len):
    if l % 2 == 0:
        e = l // 2
        p = _proj(x, row(2), w["even_in"][e], tm=512, tn=EVEN_P)
        y, sg, sh = _even_attn(p, w["gla_w2"][e], w["gla_bg"][e], w["lb_table"], w["gla_nw"][e], w["hgrn_nw"][e],
                               s_gla[e], s_hgrn[e], n_seq=n_seq, seq_len=seq_len, C=seq_len, NB=8, layer=l)
        return _out_block(y, x, w["even_out"][e], row(3)), (sg, sh)
    o = l // 2
    p = _proj(x, row(2), w["odd_in"][o], tm=256, tn=12 * LANES)
    y, sr = _odd_attn(p, w["ret_nw"][o], s_ret[o], n_seq=n_seq, seq_len=seq_len, C=seq_len, NB=4, pos0=PAST_LEN)
    return _out_block(y, x, w["odd_out"][o], row(3)), (sr,)


def kernel(x_prompt, x_sample, state_gla, state_hgrn, state_ret, norm_w, ffn_w_gate, ffn_w_up, ffn_w_down, even_w_in, gla_w_gate2, gla_b_gate, gla_norm_w, hgrn_lb_table, hgrn_norm_w, even_w_out, odd_w_in, ret_norm_w, odd_w_out):
    bp, tp, _ = x_prompt.shape
    bs, ts, _ = x_sample.shape
    n_even = even_w_in.shape[0]
    nw = norm_w.astype(f32)
    ffn_w = [w_.astype(bf16) for w_ in (ffn_w_gate, ffn_w_up, ffn_w_down)]
    w = {
        "even_in": jnp.stack([_pack_even_w_in(even_w_in[e]) for e in range(n_even)]).astype(bf16),
        "gla_w2": jnp.pad(gla_w_gate2, ((0, 0), (0, LANES - GLA_LOWRANK), (0, 0))).astype(bf16),
        "gla_bg": gla_b_gate[:, None, :].astype(f32),
        "lb_table": hgrn_lb_table.astype(f32),
        "gla_nw": gla_norm_w[:, None, :].astype(f32),
        "hgrn_nw": hgrn_norm_w[:, None, :].astype(f32),
        "even_out": even_w_out.astype(bf16),
        "odd_in": odd_w_in.astype(bf16),
        "ret_nw": ret_norm_w[:, None, :].astype(f32),
        "odd_out": odd_w_out.astype(bf16),
    }
    xp = x_prompt.reshape(bp * tp, D_MODEL)
    xs = x_sample.reshape(bs * ts, D_MODEL)
    new_p, new_s = ([], [], []), ([], [], [])
    for l in range(nw.shape[0]):
        row = lambda i, l=l: nw[l, i][None, :]
        xp, xs = _ffn(xp, xs, row(0), row(1), *(w_[l, 0] for w_ in ffn_w))
        xp, sp = _mixer_prompt(xp, l, w, row, bp, tp)
        xs, ss = _mixer_sample(xs, l, w, row, state_gla, state_hgrn, state_ret, bs, ts)
        for dst, src in ((new_p, sp), (new_s, ss)):
            for kind, st in zip((0, 1) if l % 2 == 0 else (2,), src):
                dst[kind].append(st)
        xp, xs = _ffn(xp, xs, row(4), row(5), *(w_[l, 1] for w_ in ffn_w))
    gla_p, hgrn_p, ret_p = (jnp.stack(v) for v in new_p)
    gla_s, hgrn_s, ret_s = (jnp.stack(v) for v in new_s)
    return (xp.reshape(bp, tp, D_MODEL), xs.reshape(bs, ts, D_MODEL), gla_p, hgrn_p, ret_p, gla_s, hgrn_s, ret_s)
```

```python
import functools
import math

import jax
import jax.numpy as jnp
import numpy as np
from jax import lax
from jax.experimental import pallas as pl
from jax.experimental.pallas import tpu as pltpu

f32 = jnp.float32
bf16 = jnp.bfloat16

D_MODEL = 1024
D_FF = 2816
EPS = 1e-6
PAST_LEN = 16384
ROPE_BASE = 10000.0
GLA_TAU = 16.0
N_HEADS = 4
GLA_DK = 64
EVEN_DK = 128
EVEN_DV = 128
GLA_LOWRANK = 16
RET_DK = 256
RET_DV = 512
LANES = 128
SUBLANES = 8
T_QA, T_KA, T_VA, T_RA, T_QB, T_FB, T_IB, T_GB, T_LR = 0, 2, 4, 8, 12, 16, 20, 24, 28
EVEN_P = 29 * LANES
ODD_P = 2 * N_HEADS * RET_DK + 2 * N_HEADS * RET_DV
VMEM_LIMIT = 58 * 1024 * 1024

_NT = (((1,), (1,)), ((), ()))
_TN = (((0,), (0,)), ((), ()))


def _rms(x, w):
    return x * lax.rsqrt(jnp.mean(x * x, axis=-1, keepdims=True) + EPS) * w


def _sigmoid(x):
    return 1.0 / (1.0 + jnp.exp(-x))


def _const_spec(shape):
    return pl.BlockSpec(shape, lambda *_: (0,) * len(shape), pipeline_mode=pl.Buffered(1))


def _params(*sem):
    return pltpu.CompilerParams(dimension_semantics=sem, vmem_limit_bytes=VMEM_LIMIT)


def _ffn_tile(x_ref, nwa_ref, nwb_ref, wg_ref, wu_ref, wd_ref, o_ref, acc_ref, tf):
    x = x_ref[...]
    hn = _rms(x, nwa_ref[...]).astype(bf16)
    for j in range(D_FF // tf):
        cols = slice(j * tf, (j + 1) * tf)
        g = jnp.dot(hn, wg_ref[:, cols], preferred_element_type=f32)
        u = jnp.dot(hn, wu_ref[:, cols], preferred_element_type=f32)
        a = (g * _sigmoid(g) * u).astype(bf16)
        part = jnp.dot(a, wd_ref[cols, :], preferred_element_type=f32)
        if j == 0:
            acc_ref[...] = part
        else:
            acc_ref[...] += part
    o_ref[...] = x + 0.5 * _rms(acc_ref[...], nwb_ref[...])


def _ffn_body(xa_ref, xb_ref, nwa_ref, nwb_ref, wg_ref, wu_ref, wd_ref, oa_ref, ob_ref, acc_ref, *, tf, na):
    i = pl.program_id(0)

    @pl.when(i < na)
    def _():
        _ffn_tile(xa_ref, nwa_ref, nwb_ref, wg_ref, wu_ref, wd_ref, oa_ref, acc_ref, tf)

    @pl.when(i >= na)
    def _():
        _ffn_tile(xb_ref, nwa_ref, nwb_ref, wg_ref, wu_ref, wd_ref, ob_ref, acc_ref, tf)


def _ffn(xa, xb, nwa, nwb, wg, wu, wd, l, pos, tm=512, tf=256):
    na, nb = xa.shape[0] // tm, xb.shape[0] // tm
    spec_a = pl.BlockSpec((tm, D_MODEL), lambda i: (jnp.minimum(i, na - 1), 0))
    spec_b = pl.BlockSpec((tm, D_MODEL), lambda i: (jnp.maximum(i - na, 0), 0))
    w_spec = lambda r, c: pl.BlockSpec((None, None, r, c), lambda i: (l, pos, 0, 0), pipeline_mode=pl.Buffered(1))
    return pl.pallas_call(
        functools.partial(_ffn_body, tf=tf, na=na),
        out_shape=(jax.ShapeDtypeStruct(xa.shape, f32), jax.ShapeDtypeStruct(xb.shape, f32)),
        grid=(na + nb,),
        in_specs=[
            spec_a,
            spec_b,
            _const_spec((1, D_MODEL)),
            _const_spec((1, D_MODEL)),
            w_spec(D_MODEL, D_FF),
            w_spec(D_MODEL, D_FF),
            w_spec(D_FF, D_MODEL),
        ],
        out_specs=(spec_a, spec_b),
        scratch_shapes=[pltpu.VMEM((tm, D_MODEL), f32)],
        compiler_params=_params("arbitrary"),
        name="ffn",
    )(xa, xb, nwa, nwb, wg, wu, wd)


def _proj_body(x_ref, nw_ref, w_ref, o_ref, *, tn):
    hn = _rms(x_ref[...], nw_ref[...]).astype(bf16)
    for j in range(w_ref.shape[1] // tn):
        cols = slice(j * tn, (j + 1) * tn)
        o_ref[:, cols] = jnp.dot(hn, w_ref[:, cols], preferred_element_type=f32)


def _proj(x, nw, w, tm, tn):
    n, p = x.shape[0], w.shape[1]
    return pl.pallas_call(
        functools.partial(_proj_body, tn=tn),
        out_shape=jax.ShapeDtypeStruct((n, p), f32),
        grid=(n // tm,),
        in_specs=[
            pl.BlockSpec((tm, D_MODEL), lambda i: (i, 0)),
            _const_spec((1, D_MODEL)),
            _const_spec((D_MODEL, p)),
        ],
        out_specs=pl.BlockSpec((tm, p), lambda i: (i, 0)),
        compiler_params=_params("arbitrary"),
        name="proj",
    )(x, nw, w)


def _out_body(y_ref, x_ref, w_ref, nw_ref, o_ref):
    m = jnp.dot(y_ref[...], w_ref[...], preferred_element_type=f32)
    o_ref[...] = x_ref[...] + _rms(m, nw_ref[...])


def _out_block(y, x, w, nw, tm=512):
    n, dy = y.shape
    return pl.pallas_call(
        _out_body,
        out_shape=jax.ShapeDtypeStruct((n, D_MODEL), f32),
        grid=(n // tm,),
        in_specs=[
            pl.BlockSpec((tm, dy), lambda i: (i, 0)),
            pl.BlockSpec((tm, D_MODEL), lambda i: (i, 0)),
            _const_spec((dy, D_MODEL)),
            _const_spec((1, D_MODEL)),
        ],
        out_specs=pl.BlockSpec((tm, D_MODEL), lambda i: (i, 0)),
        compiler_params=_params("arbitrary"),
        name="mixer_out",
    )(y, x, w, nw)


def _split3(g):
    hi = g.astype(bf16)
    r1 = g - hi.astype(f32)
    mid = r1.astype(bf16)
    lo = (r1 - mid.astype(f32)).astype(bf16)
    return jnp.concatenate([hi, mid, lo], axis=1)


def _roll_in_groups(x, d):
    r = x.shape[0]
    return pltpu.roll(x.reshape(r // SUBLANES, SUBLANES, LANES), d, 1).reshape(r, LANES)


def _near_diagonal(q, k, b, v, ones):
    r = q.shape[0]
    sub = lax.broadcasted_iota(jnp.int32, (r, LANES), 0) % SUBLANES
    prods = [q * k]
    for d in range(1, SUBLANES):
        decay = jnp.exp(jnp.minimum(b - _roll_in_groups(b, d), 0.0))
        prods.append(q * _roll_in_groups(k, d) * decay)
    rs = jnp.dot(jnp.concatenate(prods, axis=0).astype(bf16), ones, preferred_element_type=f32)
    o = rs[:r] * v
    for d in range(1, SUBLANES):
        o = o + jnp.where(sub >= d, rs[d * r:(d + 1) * r], 0.0) * _roll_in_groups(v, d)
    return o


def _even_attn_body(p_ref, w2_ref, bg_ref, lbt_ref, gnw_ref, hnw_ref, sg_in, sh_in, y_ref, sg_out, sh_out, st_ref,
                    *, C, NB, layer):
    R = NB * C
    c = pl.program_id(1)

    @pl.when(c == 0)
    def _():
        zeros_half = jnp.zeros((GLA_DK, EVEN_DV), f32)
        for n in range(NB):
            for h in range(N_HEADS):
                halves = [sg_in[n, h], zeros_half] if h % 2 == 0 else [zeros_half, sg_in[n, h]]
                st_ref[n, h] = jnp.concatenate(halves, axis=0).T
                st_ref[n, N_HEADS + h] = sh_in[n, h].T

    ri = lax.broadcasted_iota(jnp.int32, (2 * R, R), 0)
    ci = lax.broadcasted_iota(jnp.int32, (2 * R, R), 1)
    rr = jnp.where(ri >= R, ri - R, ri)
    same = (rr // C) == (ci // C)
    sum_mat = jnp.where(same, jnp.where(ri >= R, 1.0, jnp.where(ci <= rr, 1.0, 0.0)), 0.0).astype(bf16)
    ones = jnp.ones((LANES, LANES), bf16)

    lr = p_ref[:, T_LR * LANES:(T_LR + 1) * LANES].astype(bf16)
    xg = jnp.dot(lr, w2_ref[...], preferred_element_type=f32) + bg_ref[...]
    log_alpha = (jnp.minimum(xg, 0.0) - jnp.log1p(jnp.exp(-jnp.abs(xg)))) * (1.0 / GLA_TAU)
    low_half = lax.broadcasted_iota(jnp.int32, (R, LANES), 1) < GLA_DK

    lbt = lbt_ref[...]
    lbe = jnp.exp(lbt - jnp.max(lbt, axis=0, keepdims=True))
    lb_all = jnp.sum(lbe[:layer + 1], axis=0, keepdims=True) / jnp.sum(lbe, axis=0, keepdims=True)

    for h in range(2 * N_HEADS):
        hh = h % N_HEADS

        def col(tile):
            return p_ref[:, tile * LANES:(tile + 1) * LANES]

        if h < N_HEADS:
            mine = low_half if h % 2 == 0 else jnp.logical_not(low_half)
            q = jnp.where(mine, col(T_QA + h // 2), 0.0) * (GLA_DK ** -0.5)
            k = jnp.where(mine, col(T_KA + h // 2), 0.0)
            v = col(T_VA + h)
            gate = col(T_RA + h)
            g = log_alpha[:, (h // 2) * LANES:(h // 2 + 1) * LANES]
            nw = gnw_ref[...]
        else:
            qb = col(T_QB + hh)
            q = qb * _sigmoid(qb)
            lb = lb_all[:, hh * LANES:(hh + 1) * LANES]
            f = lb + (1.0 - lb) * _sigmoid(col(T_FB + hh))
            k = 1.0 - f
            g = jnp.log(f)
            v = col(T_IB + hh)
            gate = col(T_GB + hh)
            nw = hnw_ref[...]

        sums = jnp.dot(sum_mat, _split3(g), preferred_element_type=f32)
        sums = sums[:, :LANES] + sums[:, LANES:2 * LANES] + sums[:, 2 * LANES:]
        b, b_last = sums[:R], sums[R:]

        o = _near_diagonal(q, k, b, v, ones)
        q_dec = (q * jnp.exp(b)).astype(bf16)
        k_dec = (k * jnp.exp(b_last - b)).astype(bf16)
        state_decay = jnp.exp(b_last)
        vb = v.astype(bf16)
        parts = []
        for n in range(NB):
            rows = slice(n * C, (n + 1) * C)
            st = st_ref[n, h]
            parts.append(lax.dot_general(q_dec[rows], st.astype(bf16), _NT, preferred_element_type=f32))
            upd = lax.dot_general(vb[rows], k_dec[rows], _TN, preferred_element_type=f32)
            st_ref[n, h] = st * state_decay[(n + 1) * C - 1:(n + 1) * C, :] + upd
        o = o + (parts[0] if NB == 1 else jnp.concatenate(parts, axis=0))

        y = _rms(o, nw) * (gate * _sigmoid(gate))
        y_ref[:, h * LANES:(h + 1) * LANES] = y.astype(bf16)

    @pl.when(c == pl.num_programs(1) - 1)
    def _():
        for n in range(NB):
            for h in range(N_HEADS):
                sg_out[n, h] = st_ref[n, h].T[(h % 2) * GLA_DK:(h % 2 + 1) * GLA_DK, :]
                sh_out[n, h] = st_ref[n, N_HEADS + h].T


def _even_attn(p, w2, bg, lbt, gnw, hnw, sg, sh, *, n_seq, seq_len, C, NB, layer):
    assert C == SUBLANES, "pairs further apart than one 8-row group are not handled here"
    R = NB * C
    nc = seq_len // C
    grid = (n_seq // NB, nc)
    row_map = lambda i, c: (i * nc + c, 0)
    in_specs = [
        pl.BlockSpec((R, EVEN_P), row_map),
        _const_spec(w2.shape),
        _const_spec(bg.shape),
        _const_spec(lbt.shape),
        _const_spec((1, EVEN_DV)),
        _const_spec((1, EVEN_DV)),
    ]
    sg_spec = pl.BlockSpec((NB, N_HEADS, GLA_DK, EVEN_DV), lambda i, c: (i, 0, 0, 0))
    sh_spec = pl.BlockSpec((NB, N_HEADS, EVEN_DK, EVEN_DV), lambda i, c: (i, 0, 0, 0))
    in_specs += [sg_spec, sh_spec]
    return pl.pallas_call(
        functools.partial(_even_attn_body, C=C, NB=NB, layer=layer),
        out_shape=(
            jax.ShapeDtypeStruct((n_seq * seq_len, 2 * N_HEADS * EVEN_DV), bf16),
            jax.ShapeDtypeStruct((n_seq, N_HEADS, GLA_DK, EVEN_DV), f32),
            jax.ShapeDtypeStruct((n_seq, N_HEADS, EVEN_DK, EVEN_DV), f32),
        ),
        grid=grid,
        in_specs=in_specs,
        out_specs=(pl.BlockSpec((R, 2 * N_HEADS * EVEN_DV), row_map), sg_spec, sh_spec),
        scratch_shapes=[pltpu.VMEM((NB, 2 * N_HEADS, EVEN_DV, EVEN_DK), f32)],
        compiler_params=_params("arbitrary", "arbitrary"),
        name="even_attn",
    )(p, w2, bg, lbt, gnw, hnw, sg, sh)


CHUNK = 64
N_LEVELS = 6
N_SUM_LEVELS = 3
LOG2E = 1.4426950408889634
EVEN_PW = (T_LR + 2) * LANES


def _chunk_tables():
    t = np.arange(CHUNK)[:, None]
    j = np.arange(CHUNK)[None, :]
    mats = []
    for l in range(N_SUM_LEVELS):
        hs = 1 << l
        m = (t // (2 * hs)) * (2 * hs) + hs
        mats.append(np.where(t >= m, (j > m) & (j <= t), (j > t) & (j <= m)))
    mats.append(j <= t)
    a = np.concatenate(mats, axis=0).astype(np.float32)
    x = t ^ j
    lvl = np.where(j > t, N_LEVELS + 1, np.where(j == t, 0, np.floor(np.log2(np.maximum(x, 1))).astype(np.int64) + 1))
    return np.concatenate([a, a], axis=1), lvl.astype(np.int32)


def _log_decay_sums(g2_list, sum_mat):
    hi = [g.astype(bf16) for g in g2_list]
    lo = [(g - h.astype(f32)).astype(bf16) for g, h in zip(g2_list, hi)]
    rhs = jnp.concatenate([jnp.concatenate(hi, axis=1), jnp.concatenate(lo, axis=1)], axis=0)
    z = jnp.dot(sum_mat, rhs, preferred_element_type=f32)
    return [z[:, i * LANES:(i + 1) * LANES] for i in range(len(g2_list))]


def _level_log_decay(z, b, l):
    if l < N_SUM_LEVELS:
        return z[l * CHUNK:(l + 1) * CHUNK]
    hs = 1 << l
    ref = jnp.concatenate([jnp.broadcast_to(b[m:m + 1], (2 * hs, LANES)) for m in range(hs, CHUNK, 2 * hs)], axis=0)
    d = b - ref
    return jnp.minimum(d, -d)


def _decay_units(qs, ks, vs, sts, zs, level, fillers):
    n = len(qs)
    fillers = list(fillers)
    vbs = [v.astype(bf16) for v in vs]
    bs = [z[N_SUM_LEVELS * CHUNK:] for z in zs]
    scores = [0.0] * n
    for l in range(N_LEVELS):
        es = [jnp.exp2(_level_log_decay(zs[i], bs[i], l)) for i in range(n)]
        ss = [lax.dot_general((qs[i] * es[i]).astype(bf16), (ks[i] * es[i]).astype(bf16), _NT,
                              preferred_element_type=f32) for i in range(n)]
        scores = [jnp.where(level == l + 1, ss[i], scores[i]) for i in range(n)]
        if fillers:
            fillers.pop(0)()
    b_lasts = [b[CHUNK - 1:] for b in bs]
    o_in = [jnp.dot(scores[i].astype(bf16), vbs[i], preferred_element_type=f32) for i in range(n)]
    o_st = [lax.dot_general((qs[i] * jnp.exp2(bs[i])).astype(bf16), sts[i].astype(bf16), _NT,
                            preferred_element_type=f32) for i in range(n)]
    upd = [lax.dot_general(vbs[i], (ks[i] * jnp.exp2(b_lasts[i] - bs[i])).astype(bf16), _TN,
                           preferred_element_type=f32) for i in range(n)]
    while fillers:
        fillers.pop(0)()
    outs = [jnp.sum(qs[i] * ks[i], axis=-1, keepdims=True) * vs[i] + o_in[i] + o_st[i] for i in range(n)]
    return outs, [sts[i] * jnp.exp2(b_lasts[i]) + upd[i] for i in range(n)]


def _even_layer_body(xa_ref, xc_ref, nw2_ref, nw3_ref, win_ref, w2_ref, bg_ref, lbt_ref, gnw_ref, hnw_ref, wout_ref,
                     smat_ref, lvl_ref, o_ref, sg_out, sh_out, p_buf, hn_scr, y_scr, st_ref, *, NB, nc, layer):
    R = NB * CHUNK
    j = pl.program_id(0)
    cb = jnp.maximum(j - 1, 0) % nc
    xg0 = T_LR * LANES

    @pl.when(j == 0)
    def _():
        p_buf[...] = jnp.zeros(p_buf.shape, f32)

    @pl.when(cb == 0)
    def _():
        st_ref[...] = jnp.zeros(st_ref.shape, f32)

    hn_scr[...] = _rms(xa_ref[...].reshape(R, D_MODEL), nw2_ref[...]).astype(bf16)

    lbt = lbt_ref[...]
    lbe = jnp.exp(lbt - jnp.max(lbt, axis=0, keepdims=True))
    lb_all = jnp.sum(lbe[:layer + 1], axis=0, keepdims=True) / jnp.sum(lbe, axis=0, keepdims=True)
    sum_mat = smat_ref[...]
    level = lvl_ref[...]

    low_half = lax.broadcasted_iota(jnp.int32, (CHUNK, LANES), 1) < GLA_DK

    def proj_piece(tile):
        def run():
            cols = slice(tile * LANES, (tile + 2) * LANES)
            p_buf[:, cols] = jnp.dot(hn_scr[...], win_ref[:, cols], preferred_element_type=f32)
        return run

    def gate_piece():
        lr = jnp.dot(hn_scr[...], win_ref[:, xg0:], preferred_element_type=f32).astype(bf16)
        p_buf[:, xg0:] = jnp.dot(lr, w2_ref[...], preferred_element_type=f32) + bg_ref[...]

    def tile(rows, t):
        return p_buf[rows, t * LANES:(t + 1) * LANES]

    for h0 in range(0, 2 * N_HEADS, 2):
        nw = gnw_ref[...] if h0 < N_HEADS else hnw_ref[...]
        qs, ks, g2s, vs, gates, ids = [], [], [], [], [], []
        for h in (h0, h0 + 1):
            hh = h % N_HEADS
            for n in range(NB):
                rows = slice(n * CHUNK, (n + 1) * CHUNK)
                if h < N_HEADS:
                    mine = low_half if h % 2 == 0 else jnp.logical_not(low_half)
                    xg = tile(rows, T_LR + h // 2)
                    qs.append(jnp.where(mine, tile(rows, T_QA + h // 2), 0.0) * (GLA_DK ** -0.5))
                    ks.append(jnp.where(mine, tile(rows, T_KA + h // 2), 0.0))
                    g2s.append((jnp.minimum(xg, 0.0) - jnp.log1p(jnp.exp(-jnp.abs(xg)))) * (LOG2E / GLA_TAU))
                    vs.append(tile(rows, T_VA + h))
                    gates.append(tile(rows, T_RA + h))
                else:
                    lb = lb_all[:, hh * LANES:(hh + 1) * LANES]
                    f = lb + (1.0 - lb) * _sigmoid(tile(rows, T_FB + hh))
                    q = tile(rows, T_QB + hh)
                    qs.append(q * _sigmoid(q))
                    ks.append(1.0 - f)
                    g2s.append(jnp.log2(f))
                    vs.append(tile(rows, T_IB + hh))
                    gates.append(tile(rows, T_GB + hh))
                ids.append((n, h))
        zs = []
        for i in range(0, len(g2s), 2):
            zs += _log_decay_sums(g2s[i:i + 2], sum_mat)
        if h0 == 0:
            fillers = [proj_piece(T_VA), proj_piece(T_RA)]
        elif h0 == 2:
            fillers = [proj_piece(T_VA + 2), proj_piece(T_RA + 2), proj_piece(T_QA), proj_piece(T_KA), gate_piece]
        else:
            fillers = [proj_piece(t + h0 - N_HEADS) for t in (T_QB, T_FB, T_IB, T_GB)]
        outs, sts = _decay_units(qs, ks, vs, [st_ref[n, h] for n, h in ids], zs, level, fillers)
        for i, (n, h) in enumerate(ids):
            rows = slice(n * CHUNK, (n + 1) * CHUNK)
            st_ref[n, h] = sts[i]
            y = _rms(outs[i], nw) * (gates[i] * _sigmoid(gates[i]))
            y_scr[rows, h * LANES:(h + 1) * LANES] = y.astype(bf16)

    m = jnp.dot(y_scr[...], wout_ref[...], preferred_element_type=f32)
    o_ref[...] = (xc_ref[...].reshape(R, D_MODEL) + _rms(m, nw3_ref[...])).reshape(NB, CHUNK, D_MODEL)

    @pl.when((cb == nc - 1) & (j >= 1))
    def _():
        for n in range(NB):
            for h in range(N_HEADS):
                sg_out[n, h] = st_ref[n, h].T[(h % 2) * GLA_DK:(h % 2 + 1) * GLA_DK, :]
                sh_out[n, h] = st_ref[n, N_HEADS + h].T


def _even_layer_prompt(x3, nw2, nw3, w_in, w2, bg, lbt, gnw, hnw, w_out, *, layer, NB=4):
    n_seq, seq_len, _ = x3.shape
    nc = seq_len // CHUNK
    n_tiles = (n_seq // NB) * nc
    R = NB * CHUNK
    smat, lvl = _chunk_tables()
    x_blk = (NB, CHUNK, D_MODEL)

    def tile(lag):
        def index(j):
            t = jnp.clip(j - lag, 0, n_tiles - 1)
            return (t // nc, t % nc, 0)
        return index

    group = lambda j: (jnp.maximum(j - 1, 0) // nc, 0, 0, 0)
    return pl.pallas_call(
        functools.partial(_even_layer_body, NB=NB, nc=nc, layer=layer),
        out_shape=(
            jax.ShapeDtypeStruct(x3.shape, f32),
            jax.ShapeDtypeStruct((n_seq, N_HEADS, GLA_DK, EVEN_DV), f32),
            jax.ShapeDtypeStruct((n_seq, N_HEADS, EVEN_DK, EVEN_DV), f32),
        ),
        grid=(n_tiles + 1,),
        in_specs=[
            pl.BlockSpec(x_blk, tile(0)),
            pl.BlockSpec(x_blk, tile(1)),
            _const_spec((1, D_MODEL)),
            _const_spec((1, D_MODEL)),
            _const_spec(w_in.shape),
            _const_spec(w2.shape),
            _const_spec(bg.shape),
            _const_spec(lbt.shape),
            _const_spec((1, EVEN_DV)),
            _const_spec((1, EVEN_DV)),
            _const_spec(w_out.shape),
            _const_spec(smat.shape),
            _const_spec(lvl.shape),
        ],
        out_specs=(
            pl.BlockSpec(x_blk, tile(1)),
            pl.BlockSpec((NB, N_HEADS, GLA_DK, EVEN_DV), group),
            pl.BlockSpec((NB, N_HEADS, EVEN_DK, EVEN_DV), group),
        ),
        scratch_shapes=[
            pltpu.VMEM((R, EVEN_PW), f32),
            pltpu.VMEM((R, D_MODEL), bf16),
            pltpu.VMEM((R, 2 * N_HEADS * EVEN_DV), bf16),
            pltpu.VMEM((NB, 2 * N_HEADS, EVEN_DV, EVEN_DK), f32),
        ],
        compiler_params=_params("arbitrary"),
        name="even_layer",
    )(x3, x3, nw2, nw3, w_in, w2, bg, lbt, gnw, hnw, w_out, jnp.asarray(smat, bf16), jnp.asarray(lvl))


RET_CHUNK = 256


def _rope_body(cos_ref, sin_ref, *, pos0):
    n, half = cos_ref.shape
    pos = (pos0 + lax.broadcasted_iota(jnp.int32, (n, half), 0)).astype(f32)
    lane = lax.broadcasted_iota(jnp.int32, (n, half), 1).astype(f32)
    ang = pos * jnp.power(ROPE_BASE, -lane / half)
    cos_ref[...] = jnp.cos(ang)
    sin_ref[...] = jnp.sin(ang)


def _rope_tables(n, pos0):
    shape = jax.ShapeDtypeStruct((n, RET_DK // 2), f32)
    return pl.pallas_call(functools.partial(_rope_body, pos0=pos0), out_shape=(shape, shape), name="rope_tables")()


def _ret_cols(h):
    return [slice(off + h * w, off + (h + 1) * w) for off, w in (
        (0, RET_DK), (N_HEADS * RET_DK, RET_DK), (2 * N_HEADS * RET_DK, RET_DV),
        (2 * N_HEADS * RET_DK + N_HEADS * RET_DV, RET_DV))]


def _rotary(x, cos, sin):
    half = x.shape[1] // 2
    x1, x2 = x[:, :half], x[:, half:]
    return jnp.concatenate([x1 * cos - x2 * sin, x1 * sin + x2 * cos], axis=1)


def _odd_layer_body(xa_ref, xc_ref, cos_ref, sin_ref, nw2_ref, nw3_ref, win_ref, rnw_ref, wout_ref,
                    ps_ref, cos_s_ref, sin_s_ref, s_in, o_ref, s_out, ys_ref, s_new,
                    p_buf, hn_scr, y_scr, dec_scr, *, nc, NS, TS):
    C = RET_CHUNK
    j = pl.program_id(0)
    cb = jnp.maximum(j - 1, 0) % nc

    @pl.when(j == 0)
    def _():
        p_buf[...] = jnp.zeros(p_buf.shape, f32)
        ri = lax.broadcasted_iota(jnp.int32, (C, C), 0)
        ci = lax.broadcasted_iota(jnp.int32, (C, C), 1)
        dist = jnp.maximum(ri - ci, 0).astype(f32)
        for h in range(N_HEADS):
            dec_scr[h] = jnp.where(ci <= ri, jnp.exp2(dist * math.log2(1.0 - 2.0 ** (-5.0 - h))), 0.0)

    @pl.when(cb == 0)
    def _():
        s_out[...] = jnp.zeros(s_out.shape, f32)

    hn_scr[...] = _rms(xa_ref[0], nw2_ref[...]).astype(bf16)
    cos, sin = cos_ref[...], sin_ref[...]
    t_k = lax.broadcasted_iota(jnp.int32, (C, RET_DK), 0).astype(f32)

    RS = NS * TS
    cos_s = jnp.concatenate([cos_s_ref[...]] * NS, axis=0)
    sin_s = jnp.concatenate([sin_s_ref[...]] * NS, axis=0)
    ri = lax.broadcasted_iota(jnp.int32, (RS, RS), 0)
    ci = lax.broadcasted_iota(jnp.int32, (RS, RS), 1)
    visible = ((ri // TS) == (ci // TS)) & (ci <= ri)
    dist_s = jnp.where(visible, ri - ci, 0).astype(f32)
    t_s = (lax.broadcasted_iota(jnp.int32, (RS, RET_DK), 0) % TS).astype(f32)

    for h in range(N_HEADS):
        log2_gamma = math.log2(1.0 - 2.0 ** (-5.0 - h))
        cols = _ret_cols(h)
        q = _rotary(p_buf[:, cols[0]], cos, sin)
        k = _rotary(p_buf[:, cols[1]], cos, sin) * (RET_DK ** -0.5)
        vb = p_buf[:, cols[2]].astype(bf16)
        gate = p_buf[:, cols[3]]
        s = lax.dot_general(q.astype(bf16), k.astype(bf16), _NT, preferred_element_type=f32) * dec_scr[h]
        o = jnp.dot(s.astype(bf16), vb, preferred_element_type=f32)
        st = s_out[0, h]
        q_dec = (q * jnp.exp2((t_k + 1.0) * log2_gamma)).astype(bf16)
        o = o + jnp.dot(q_dec, st.astype(bf16), preferred_element_type=f32)
        k_dec = (k * jnp.exp2((C - 1.0 - t_k) * log2_gamma)).astype(bf16)
        upd = lax.dot_general(k_dec, vb, _TN, preferred_element_type=f32)
        s_out[0, h] = st * (2.0 ** (C * log2_gamma)) + upd
        y = _rms(o, rnw_ref[...]) * (gate * _sigmoid(gate))
        y_scr[:, h * RET_DV:(h + 1) * RET_DV] = y.astype(bf16)
        for cs in cols:
            p_buf[:, cs] = jnp.dot(hn_scr[...], win_ref[:, cs], preferred_element_type=f32)

        q = _rotary(ps_ref[:, cols[0]], cos_s, sin_s)
        k = _rotary(ps_ref[:, cols[1]], cos_s, sin_s) * (RET_DK ** -0.5)
        vb = ps_ref[:, cols[2]].astype(bf16)
        gate = ps_ref[:, cols[3]]
        decay = jnp.where(visible, jnp.exp2(dist_s * log2_gamma), 0.0)
        s = lax.dot_general(q.astype(bf16), k.astype(bf16), _NT, preferred_element_type=f32) * decay
        o = jnp.dot(s.astype(bf16), vb, preferred_element_type=f32)
        q_dec = (q * jnp.exp2((t_s + 1.0) * log2_gamma)).astype(bf16)
        k_dec = (k * jnp.exp2((TS - 1.0 - t_s) * log2_gamma)).astype(bf16)
        parts = []
        for n in range(NS):
            rows = slice(n * TS, (n + 1) * TS)
            st = s_in[n, h]
            parts.append(jnp.dot(q_dec[rows], st.astype(bf16), preferred_element_type=f32))
            upd = lax.dot_general(k_dec[rows], vb[rows], _TN, preferred_element_type=f32)
            s_new[n, h] = st * (2.0 ** (TS * log2_gamma)) + upd
        o = o + jnp.concatenate(parts, axis=0)
        y = _rms(o, rnw_ref[...]) * (gate * _sigmoid(gate))
        ys_ref[:, h * RET_DV:(h + 1) * RET_DV] = y.astype(bf16)

    m = jnp.dot(y_scr[...], wout_ref[...], preferred_element_type=f32)
    o_ref[0] = xc_ref[0] + _rms(m, nw3_ref[...])


def _odd_layer(x3, ps, s_ret, nw2, nw3, w_in, rnw, w_out):
    n_seq, seq_len, _ = x3.shape
    n_s = s_ret.shape[0]
    TS = ps.shape[0] // n_s
    C = RET_CHUNK
    nc = seq_len // C
    n_tiles = n_seq * nc
    NS = n_s // n_tiles
    assert NS * n_tiles == n_s and (NS * TS) % 16 == 0
    cos, sin = _rope_tables(seq_len, 0)
    cos_s, sin_s = _rope_tables(TS, PAST_LEN)
    x_blk = (1, C, D_MODEL)

    def tile(lag):
        def index(j):
            t = jnp.clip(j - lag, 0, n_tiles - 1)
            return (t // nc, t % nc, 0)
        return index

    short = lambda j: jnp.minimum(j, n_tiles - 1)
    rope_spec = pl.BlockSpec((C, RET_DK // 2), lambda j: (jnp.maximum(j - 1, 0) % nc, 0))
    state_s_spec = pl.BlockSpec((NS, N_HEADS, RET_DK, RET_DV), lambda j: (short(j), 0, 0, 0))
    return pl.pallas_call(
        functools.partial(_odd_layer_body, nc=nc, NS=NS, TS=TS),
        out_shape=(
            jax.ShapeDtypeStruct(x3.shape, f32),
            jax.ShapeDtypeStruct((n_seq, N_HEADS, RET_DK, RET_DV), f32),
            jax.ShapeDtypeStruct((n_s * TS, N_HEADS * RET_DV), bf16),
            jax.ShapeDtypeStruct(s_ret.shape, f32),
        ),
        grid=(n_tiles + 1,),
        in_specs=[
            pl.BlockSpec(x_blk, tile(0)),
            pl.BlockSpec(x_blk, tile(1)),
            rope_spec,
            rope_spec,
            _const_spec((1, D_MODEL)),
            _const_spec((1, D_MODEL)),
            _const_spec(w_in.shape),
            _const_spec((1, RET_DV)),
            _const_spec(w_out.shape),
            pl.BlockSpec((NS * TS, ODD_P), lambda j: (short(j), 0)),
            _const_spec((TS, RET_DK // 2)),
            _const_spec((TS, RET_DK // 2)),
            state_s_spec,
        ],
        out_specs=(
            pl.BlockSpec(x_blk, tile(1)),
            pl.BlockSpec((1, N_HEADS, RET_DK, RET_DV), lambda j: (jnp.maximum(j - 1, 0) // nc, 0, 0, 0)),
            pl.BlockSpec((NS * TS, N_HEADS * RET_DV), lambda j: (short(j), 0)),
            state_s_spec,
        ),
        scratch_shapes=[
            pltpu.VMEM((C, ODD_P), f32),
            pltpu.VMEM((C, D_MODEL), bf16),
            pltpu.VMEM((C, N_HEADS * RET_DV), bf16),
            pltpu.VMEM((N_HEADS, C, C), f32),
        ],
        compiler_params=_params("arbitrary"),
        name="odd_layer",
    )(x3, x3, cos, sin, nw2, nw3, w_in, rnw, w_out, ps, cos_s, sin_s, s_ret)


def _pack_even_w_in(w):
    lr0 = 2 * N_HEADS * GLA_DK + 2 * N_HEADS * EVEN_DV
    lr = jnp.pad(w[:, lr0:lr0 + GLA_LOWRANK], ((0, 0), (0, LANES - GLA_LOWRANK)))
    return jnp.concatenate([w[:, :lr0], w[:, lr0 + GLA_LOWRANK:], lr], axis=1)


def _mixer_even_prompt(x, l, w, row, n_seq, seq_len):
    e = l // 2
    x3, sg, sh = _even_layer_prompt(x.reshape(n_seq, seq_len, D_MODEL), row(2), row(3), w["even_in"][e], w["gla_w2"][e],
                                    w["gla_bg"][e], w["lb_table"], w["gla_nw"][e], w["hgrn_nw"][e], w["even_out"][e],
                                    layer=l)
    return x3.reshape(n_seq * seq_len, D_MODEL), sg, sh


def _mixer_even_sample(x, l, w, row, s_gla, s_hgrn, n_seq, seq_len):
    e = l // 2
    p = _proj(x, row(2), w["even_in"][e], tm=512, tn=EVEN_P)
    y, sg, sh = _even_attn(p, w["gla_w2"][e], w["gla_bg"][e], w["lb_table"], w["gla_nw"][e], w["hgrn_nw"][e],
                           s_gla[e], s_hgrn[e], n_seq=n_seq, seq_len=seq_len, C=seq_len, NB=8, layer=l)
    return _out_block(y, x, w["even_out"][e], row(3)), sg, sh


def _mixer_odd(xp, xs, l, w, row, s_ret, n_seq, seq_len):
    o = l // 2
    ps = _proj(xs, row(2), w["odd_in"][o], tm=256, tn=12 * LANES)
    x3, sr_p, ys, sr_s = _odd_layer(xp.reshape(n_seq, seq_len, D_MODEL), ps, s_ret[o], row(2), row(3), w["odd_in"][o],
                                    w["ret_nw"][o], w["odd_out"][o])
    return x3.reshape(n_seq * seq_len, D_MODEL), _out_block(ys, xs, w["odd_out"][o], row(3)), sr_p, sr_s


def kernel(x_prompt, x_sample, state_gla, state_hgrn, state_ret, norm_w, ffn_w_gate, ffn_w_up, ffn_w_down, even_w_in, gla_w_gate2, gla_b_gate, gla_norm_w, hgrn_lb_table, hgrn_norm_w, even_w_out, odd_w_in, ret_norm_w, odd_w_out):
    bp, tp, _ = x_prompt.shape
    bs, ts, _ = x_sample.shape
    n_even = even_w_in.shape[0]
    nw = norm_w.astype(f32)
    ffn_w = [w_.astype(bf16) for w_ in (ffn_w_gate, ffn_w_up, ffn_w_down)]
    w = {
        "even_in": jnp.stack([_pack_even_w_in(even_w_in[e]) for e in range(n_even)]).astype(bf16),
        "gla_w2": jnp.pad(gla_w_gate2, ((0, 0), (0, LANES - GLA_LOWRANK), (0, 0))).astype(bf16),
        "gla_bg": gla_b_gate[:, None, :].astype(f32),
        "lb_table": hgrn_lb_table.astype(f32),
        "gla_nw": gla_norm_w[:, None, :].astype(f32),
        "hgrn_nw": hgrn_norm_w[:, None, :].astype(f32),
        "even_out": even_w_out.astype(bf16),
        "odd_in": odd_w_in.astype(bf16),
        "ret_nw": ret_norm_w[:, None, :].astype(f32),
        "odd_out": odd_w_out.astype(bf16),
    }
    xp = x_prompt.reshape(bp * tp, D_MODEL)
    xs = x_sample.reshape(bs * ts, D_MODEL)
    new_p, new_s = ([], [], []), ([], [], [])
    for l in range(nw.shape[0]):
        row = lambda i, l=l: nw[l, i][None, :]
        xp, xs = _ffn(xp, xs, row(0), row(1), *ffn_w, l, 0)
        if l % 2 == 0:
            xp, sg, sh = _mixer_even_prompt(xp, l, w, row, bp, tp)
            new_p[0].append(sg)
            new_p[1].append(sh)
            xs, sg, sh = _mixer_even_sample(xs, l, w, row, state_gla, state_hgrn, bs, ts)
            new_s[0].append(sg)
            new_s[1].append(sh)
        else:
            xp, xs, sr_p, sr_s = _mixer_odd(xp, xs, l, w, row, state_ret, bp, tp)
            new_p[2].append(sr_p)
            new_s[2].append(sr_s)
        xp, xs = _ffn(xp, xs, row(4), row(5), *ffn_w, l, 1)
    gla_p, hgrn_p, ret_p = (jnp.stack(v) for v in new_p)
    gla_s, hgrn_s, ret_s = (jnp.stack(v) for v in new_s)
    return (xp.reshape(bp, tp, D_MODEL), xs.reshape(bs, ts, D_MODEL), gla_p, hgrn_p, ret_p, gla_s, hgrn_s, ret_s)
```

```python
import functools
import math

import jax
import jax.numpy as jnp
import numpy as np
from jax import lax
from jax.experimental import pallas as pl
from jax.experimental.pallas import tpu as pltpu

f32 = jnp.float32
bf16 = jnp.bfloat16

D_MODEL = 1024
D_FF = 2816
EPS = 1e-6
PAST_LEN = 16384
ROPE_BASE = 10000.0
GLA_TAU = 16.0
N_HEADS = 4
GLA_DK = 64
EVEN_DK = 128
EVEN_DV = 128
GLA_LOWRANK = 16
RET_DK = 256
RET_DV = 512
LANES = 128
SUBLANES = 8
T_QA, T_KA, T_VA, T_RA, T_QB, T_FB, T_IB, T_GB, T_LR = 0, 2, 4, 8, 12, 16, 20, 24, 28
EVEN_P = 29 * LANES
ODD_P = 2 * N_HEADS * RET_DK + 2 * N_HEADS * RET_DV
VMEM_LIMIT = 58 * 1024 * 1024

_NT = (((1,), (1,)), ((), ()))
_TN = (((0,), (0,)), ((), ()))


def _rms(x, w):
    return x * lax.rsqrt(jnp.mean(x * x, axis=-1, keepdims=True) + EPS) * w


def _sigmoid(x):
    return 1.0 / (1.0 + jnp.exp(-x))


def _const_spec(shape):
    return pl.BlockSpec(shape, lambda *_: (0,) * len(shape), pipeline_mode=pl.Buffered(1))


def _params(*sem):
    return pltpu.CompilerParams(dimension_semantics=sem, vmem_limit_bytes=VMEM_LIMIT)


def _ffn_tile(x_ref, nwa_ref, nwb_ref, wg_ref, wu_ref, wd_ref, o_ref, acc_ref, tf):
    tm = x_ref.shape[0]
    hm = tm // 2
    halves = (slice(0, hm), slice(hm, tm))
    hn = [_rms(x_ref[r, :], nwa_ref[...]).astype(bf16) for r in halves]
    nj = D_FF // tf

    def epilogue(r):
        o_ref[r, :] = x_ref[r, :] + 0.5 * _rms(acc_ref[r, :], nwb_ref[...])

    for i, r in enumerate(halves):
        for j in range(nj):
            cols = slice(j * tf, (j + 1) * tf)
            g = jnp.dot(hn[i], wg_ref[:, cols], preferred_element_type=f32)
            u = jnp.dot(hn[i], wu_ref[:, cols], preferred_element_type=f32)
            a = (g * _sigmoid(g) * u).astype(bf16)
            part = jnp.dot(a, wd_ref[cols, :], preferred_element_type=f32)
            if j == 0:
                acc_ref[r, :] = part
            else:
                acc_ref[r, :] += part
            if i == 1 and j == 1:
                epilogue(halves[0])
    epilogue(halves[1])


def _ffn_body(xa_ref, xb_ref, nwa_ref, nwb_ref, wg_ref, wu_ref, wd_ref, oa_ref, ob_ref, acc_ref, *, tf, na):
    i = pl.program_id(0)

    @pl.when(i < na)
    def _():
        _ffn_tile(xa_ref, nwa_ref, nwb_ref, wg_ref, wu_ref, wd_ref, oa_ref, acc_ref, tf)

    @pl.when(i >= na)
    def _():
        _ffn_tile(xb_ref, nwa_ref, nwb_ref, wg_ref, wu_ref, wd_ref, ob_ref, acc_ref, tf)


def _ffn(xa, xb, nwa, nwb, wg, wu, wd, l, pos, tm=1024, tf=256):
    na, nb = xa.shape[0] // tm, xb.shape[0] // tm
    spec_a = pl.BlockSpec((tm, D_MODEL), lambda i: (jnp.minimum(i, na - 1), 0))
    spec_b = pl.BlockSpec((tm, D_MODEL), lambda i: (jnp.maximum(i - na, 0), 0))
    w_spec = lambda r, c: pl.BlockSpec((None, None, r, c), lambda i: (l, pos, 0, 0), pipeline_mode=pl.Buffered(1))
    return pl.pallas_call(
        functools.partial(_ffn_body, tf=tf, na=na),
        out_shape=(jax.ShapeDtypeStruct(xa.shape, f32), jax.ShapeDtypeStruct(xb.shape, f32)),
        grid=(na + nb,),
        in_specs=[
            spec_a,
            spec_b,
            _const_spec((1, D_MODEL)),
            _const_spec((1, D_MODEL)),
            w_spec(D_MODEL, D_FF),
            w_spec(D_MODEL, D_FF),
            w_spec(D_FF, D_MODEL),
        ],
        out_specs=(spec_a, spec_b),
        scratch_shapes=[pltpu.VMEM((tm, D_MODEL), f32)],
        compiler_params=_params("arbitrary"),
        name="ffn",
    )(xa, xb, nwa, nwb, wg, wu, wd)


def _proj_body(x_ref, nw_ref, w_ref, o_ref, *, tn):
    hn = _rms(x_ref[...], nw_ref[...]).astype(bf16)
    for j in range(w_ref.shape[1] // tn):
        cols = slice(j * tn, (j + 1) * tn)
        o_ref[:, cols] = jnp.dot(hn, w_ref[:, cols], preferred_element_type=f32)


def _proj(x, nw, w, tm, tn):
    n, p = x.shape[0], w.shape[1]
    return pl.pallas_call(
        functools.partial(_proj_body, tn=tn),
        out_shape=jax.ShapeDtypeStruct((n, p), f32),
        grid=(n // tm,),
        in_specs=[
            pl.BlockSpec((tm, D_MODEL), lambda i: (i, 0)),
            _const_spec((1, D_MODEL)),
            _const_spec((D_MODEL, p)),
        ],
        out_specs=pl.BlockSpec((tm, p), lambda i: (i, 0)),
        compiler_params=_params("arbitrary"),
        name="proj",
    )(x, nw, w)


def _out_body(y_ref, x_ref, w_ref, nw_ref, o_ref):
    m = jnp.dot(y_ref[...], w_ref[...], preferred_element_type=f32)
    o_ref[...] = x_ref[...] + _rms(m, nw_ref[...])


def _out_block(y, x, w, nw, tm=512):
    n, dy = y.shape
    return pl.pallas_call(
        _out_body,
        out_shape=jax.ShapeDtypeStruct((n, D_MODEL), f32),
        grid=(n // tm,),
        in_specs=[
            pl.BlockSpec((tm, dy), lambda i: (i, 0)),
            pl.BlockSpec((tm, D_MODEL), lambda i: (i, 0)),
            _const_spec((dy, D_MODEL)),
            _const_spec((1, D_MODEL)),
        ],
        out_specs=pl.BlockSpec((tm, D_MODEL), lambda i: (i, 0)),
        compiler_params=_params("arbitrary"),
        name="mixer_out",
    )(y, x, w, nw)


def _split3(g):
    hi = g.astype(bf16)
    r1 = g - hi.astype(f32)
    mid = r1.astype(bf16)
    lo = (r1 - mid.astype(f32)).astype(bf16)
    return jnp.concatenate([hi, mid, lo], axis=1)


def _roll_in_groups(x, d):
    r = x.shape[0]
    return pltpu.roll(x.reshape(r // SUBLANES, SUBLANES, LANES), d, 1).reshape(r, LANES)


def _near_diagonal(q, k, b, v, ones):
    r = q.shape[0]
    sub = lax.broadcasted_iota(jnp.int32, (r, LANES), 0) % SUBLANES
    prods = [q * k]
    for d in range(1, SUBLANES):
        decay = jnp.exp(jnp.minimum(b - _roll_in_groups(b, d), 0.0))
        prods.append(q * _roll_in_groups(k, d) * decay)
    rs = jnp.dot(jnp.concatenate(prods, axis=0).astype(bf16), ones, preferred_element_type=f32)
    o = rs[:r] * v
    for d in range(1, SUBLANES):
        o = o + jnp.where(sub >= d, rs[d * r:(d + 1) * r], 0.0) * _roll_in_groups(v, d)
    return o


def _even_attn_body(p_ref, w2_ref, bg_ref, lbt_ref, gnw_ref, hnw_ref, sg_in, sh_in, y_ref, sg_out, sh_out, st_ref,
                    *, C, NB, layer):
    R = NB * C
    c = pl.program_id(1)

    @pl.when(c == 0)
    def _():
        zeros_half = jnp.zeros((GLA_DK, EVEN_DV), f32)
        for n in range(NB):
            for h in range(N_HEADS):
                halves = [sg_in[n, h], zeros_half] if h % 2 == 0 else [zeros_half, sg_in[n, h]]
                st_ref[n, h] = jnp.concatenate(halves, axis=0).T
                st_ref[n, N_HEADS + h] = sh_in[n, h].T

    ri = lax.broadcasted_iota(jnp.int32, (2 * R, R), 0)
    ci = lax.broadcasted_iota(jnp.int32, (2 * R, R), 1)
    rr = jnp.where(ri >= R, ri - R, ri)
    same = (rr // C) == (ci // C)
    sum_mat = jnp.where(same, jnp.where(ri >= R, 1.0, jnp.where(ci <= rr, 1.0, 0.0)), 0.0).astype(bf16)
    ones = jnp.ones((LANES, LANES), bf16)

    lr = p_ref[:, T_LR * LANES:(T_LR + 1) * LANES].astype(bf16)
    xg = jnp.dot(lr, w2_ref[...], preferred_element_type=f32) + bg_ref[...]
    log_alpha = (jnp.minimum(xg, 0.0) - jnp.log1p(jnp.exp(-jnp.abs(xg)))) * (1.0 / GLA_TAU)
    low_half = lax.broadcasted_iota(jnp.int32, (R, LANES), 1) < GLA_DK

    lbt = lbt_ref[...]
    lbe = jnp.exp(lbt - jnp.max(lbt, axis=0, keepdims=True))
    lb_all = jnp.sum(lbe[:layer + 1], axis=0, keepdims=True) / jnp.sum(lbe, axis=0, keepdims=True)

    for h in range(2 * N_HEADS):
        hh = h % N_HEADS

        def col(tile):
            return p_ref[:, tile * LANES:(tile + 1) * LANES]

        if h < N_HEADS:
            mine = low_half if h % 2 == 0 else jnp.logical_not(low_half)
            q = jnp.where(mine, col(T_QA + h // 2), 0.0) * (GLA_DK ** -0.5)
            k = jnp.where(mine, col(T_KA + h // 2), 0.0)
            v = col(T_VA + h)
            gate = col(T_RA + h)
            g = log_alpha[:, (h // 2) * LANES:(h // 2 + 1) * LANES]
            nw = gnw_ref[...]
        else:
            qb = col(T_QB + hh)
            q = qb * _sigmoid(qb)
            lb = lb_all[:, hh * LANES:(hh + 1) * LANES]
            f = lb + (1.0 - lb) * _sigmoid(col(T_FB + hh))
            k = 1.0 - f
            g = jnp.log(f)
            v = col(T_IB + hh)
            gate = col(T_GB + hh)
            nw = hnw_ref[...]

        sums = jnp.dot(sum_mat, _split3(g), preferred_element_type=f32)
        sums = sums[:, :LANES] + sums[:, LANES:2 * LANES] + sums[:, 2 * LANES:]
        b, b_last = sums[:R], sums[R:]

        o = _near_diagonal(q, k, b, v, ones)
        q_dec = (q * jnp.exp(b)).astype(bf16)
        k_dec = (k * jnp.exp(b_last - b)).astype(bf16)
        state_decay = jnp.exp(b_last)
        vb = v.astype(bf16)
        parts = []
        for n in range(NB):
            rows = slice(n * C, (n + 1) * C)
            st = st_ref[n, h]
            parts.append(lax.dot_general(q_dec[rows], st.astype(bf16), _NT, preferred_element_type=f32))
            upd = lax.dot_general(vb[rows], k_dec[rows], _TN, preferred_element_type=f32)
            st_ref[n, h] = st * state_decay[(n + 1) * C - 1:(n + 1) * C, :] + upd
        o = o + (parts[0] if NB == 1 else jnp.concatenate(parts, axis=0))

        y = _rms(o, nw) * (gate * _sigmoid(gate))
        y_ref[:, h * LANES:(h + 1) * LANES] = y.astype(bf16)

    @pl.when(c == pl.num_programs(1) - 1)
    def _():
        for n in range(NB):
            for h in range(N_HEADS):
                sg_out[n, h] = st_ref[n, h].T[(h % 2) * GLA_DK:(h % 2 + 1) * GLA_DK, :]
                sh_out[n, h] = st_ref[n, N_HEADS + h].T


def _even_attn(p, w2, bg, lbt, gnw, hnw, sg, sh, *, n_seq, seq_len, C, NB, layer):
    assert C == SUBLANES, "pairs further apart than one 8-row group are not handled here"
    R = NB * C
    nc = seq_len // C
    grid = (n_seq // NB, nc)
    row_map = lambda i, c: (i * nc + c, 0)
    in_specs = [
        pl.BlockSpec((R, EVEN_P), row_map),
        _const_spec(w2.shape),
        _const_spec(bg.shape),
        _const_spec(lbt.shape),
        _const_spec((1, EVEN_DV)),
        _const_spec((1, EVEN_DV)),
    ]
    sg_spec = pl.BlockSpec((NB, N_HEADS, GLA_DK, EVEN_DV), lambda i, c: (i, 0, 0, 0))
    sh_spec = pl.BlockSpec((NB, N_HEADS, EVEN_DK, EVEN_DV), lambda i, c: (i, 0, 0, 0))
    in_specs += [sg_spec, sh_spec]
    return pl.pallas_call(
        functools.partial(_even_attn_body, C=C, NB=NB, layer=layer),
        out_shape=(
            jax.ShapeDtypeStruct((n_seq * seq_len, 2 * N_HEADS * EVEN_DV), bf16),
            jax.ShapeDtypeStruct((n_seq, N_HEADS, GLA_DK, EVEN_DV), f32),
            jax.ShapeDtypeStruct((n_seq, N_HEADS, EVEN_DK, EVEN_DV), f32),
        ),
        grid=grid,
        in_specs=in_specs,
        out_specs=(pl.BlockSpec((R, 2 * N_HEADS * EVEN_DV), row_map), sg_spec, sh_spec),
        scratch_shapes=[pltpu.VMEM((NB, 2 * N_HEADS, EVEN_DV, EVEN_DK), f32)],
        compiler_params=_params("arbitrary", "arbitrary"),
        name="even_attn",
    )(p, w2, bg, lbt, gnw, hnw, sg, sh)


CHUNK = 64
N_LEVELS = 6
N_SUM_LEVELS = 3
LOG2E = 1.4426950408889634
EVEN_PW = (T_LR + 2) * LANES


def _chunk_tables():
    t = np.arange(CHUNK)[:, None]
    j = np.arange(CHUNK)[None, :]
    mats = []
    for l in range(N_SUM_LEVELS):
        hs = 1 << l
        m = (t // (2 * hs)) * (2 * hs) + hs
        mats.append(np.where(t >= m, (j > m) & (j <= t), (j > t) & (j <= m)))
    mats.append(j <= t)
    a = np.concatenate(mats, axis=0).astype(np.float32)
    x = t ^ j
    lvl = np.where(j > t, N_LEVELS + 1, np.where(j == t, 0, np.floor(np.log2(np.maximum(x, 1))).astype(np.int64) + 1))
    return np.concatenate([a, a], axis=1), lvl.astype(np.int32)


def _log_decay_sums(g2_list, sum_mat):
    hi = [g.astype(bf16) for g in g2_list]
    lo = [(g - h.astype(f32)).astype(bf16) for g, h in zip(g2_list, hi)]
    rhs = jnp.concatenate([jnp.concatenate(hi, axis=1), jnp.concatenate(lo, axis=1)], axis=0)
    z = jnp.dot(sum_mat, rhs, preferred_element_type=f32)
    return [z[:, i * LANES:(i + 1) * LANES] for i in range(len(g2_list))]


def _level_log_decay(z, b, l):
    if l < N_SUM_LEVELS:
        return z[l * CHUNK:(l + 1) * CHUNK]
    hs = 1 << l
    ref = jnp.concatenate([jnp.broadcast_to(b[m:m + 1], (2 * hs, LANES)) for m in range(hs, CHUNK, 2 * hs)], axis=0)
    d = b - ref
    return jnp.minimum(d, -d)


def _decay_units(qs, ks, vs, sts, zs, level, fillers):
    n = len(qs)
    fillers = list(fillers)
    vbs = [v.astype(bf16) for v in vs]
    bs = [z[N_SUM_LEVELS * CHUNK:] for z in zs]
    scores = [0.0] * n
    for l in range(N_LEVELS):
        es = [jnp.exp2(_level_log_decay(zs[i], bs[i], l)) for i in range(n)]
        ss = [lax.dot_general((qs[i] * es[i]).astype(bf16), (ks[i] * es[i]).astype(bf16), _NT,
                              preferred_element_type=f32) for i in range(n)]
        scores = [jnp.where(level == l + 1, ss[i], scores[i]) for i in range(n)]
        if fillers:
            fillers.pop(0)()
    b_lasts = [b[CHUNK - 1:] for b in bs]
    o_in = [jnp.dot(scores[i].astype(bf16), vbs[i], preferred_element_type=f32) for i in range(n)]
    o_st = [lax.dot_general((qs[i] * jnp.exp2(bs[i])).astype(bf16), sts[i].astype(bf16), _NT,
                            preferred_element_type=f32) for i in range(n)]
    upd = [lax.dot_general(vbs[i], (ks[i] * jnp.exp2(b_lasts[i] - bs[i])).astype(bf16), _TN,
                           preferred_element_type=f32) for i in range(n)]
    while fillers:
        fillers.pop(0)()
    outs = [jnp.sum(qs[i] * ks[i], axis=-1, keepdims=True) * vs[i] + o_in[i] + o_st[i] for i in range(n)]
    return outs, [sts[i] * jnp.exp2(b_lasts[i]) + upd[i] for i in range(n)]


def _even_layer_body(xa_ref, xc_ref, nw2_ref, nw3_ref, win_ref, w2_ref, bg_ref, lbt_ref, gnw_ref, hnw_ref, wout_ref,
                     smat_ref, lvl_ref, o_ref, sg_out, sh_out, p_buf, hn_scr, y_scr, st_ref, *, NB, nc, layer):
    R = NB * CHUNK
    j = pl.program_id(0)
    cb = jnp.maximum(j - 1, 0) % nc
    xg0 = T_LR * LANES

    @pl.when(j == 0)
    def _():
        p_buf[...] = jnp.zeros(p_buf.shape, f32)

    @pl.when(cb == 0)
    def _():
        st_ref[...] = jnp.zeros(st_ref.shape, f32)

    hn_scr[...] = _rms(xa_ref[...].reshape(R, D_MODEL), nw2_ref[...]).astype(bf16)

    lbt = lbt_ref[...]
    lbe = jnp.exp(lbt - jnp.max(lbt, axis=0, keepdims=True))
    lb_all = jnp.sum(lbe[:layer + 1], axis=0, keepdims=True) / jnp.sum(lbe, axis=0, keepdims=True)
    sum_mat = smat_ref[...]
    level = lvl_ref[...]

    low_half = lax.broadcasted_iota(jnp.int32, (CHUNK, LANES), 1) < GLA_DK

    def proj_piece(tile):
        def run():
            cols = slice(tile * LANES, (tile + 2) * LANES)
            p_buf[:, cols] = jnp.dot(hn_scr[...], win_ref[:, cols], preferred_element_type=f32)
        return run

    def gate_piece():
        lr = jnp.dot(hn_scr[...], win_ref[:, xg0:], preferred_element_type=f32).astype(bf16)
        p_buf[:, xg0:] = jnp.dot(lr, w2_ref[...], preferred_element_type=f32) + bg_ref[...]

    def tile(rows, t):
        return p_buf[rows, t * LANES:(t + 1) * LANES]

    for h0 in range(0, 2 * N_HEADS, 2):
        nw = gnw_ref[...] if h0 < N_HEADS else hnw_ref[...]
        qs, ks, g2s, vs, gates, ids = [], [], [], [], [], []
        for h in (h0, h0 + 1):
            hh = h % N_HEADS
            for n in range(NB):
                rows = slice(n * CHUNK, (n + 1) * CHUNK)
                if h < N_HEADS:
                    mine = low_half if h % 2 == 0 else jnp.logical_not(low_half)
                    xg = tile(rows, T_LR + h // 2)
                    qs.append(jnp.where(mine, tile(rows, T_QA + h // 2), 0.0) * (GLA_DK ** -0.5))
                    ks.append(jnp.where(mine, tile(rows, T_KA + h // 2), 0.0))
                    g2s.append((jnp.minimum(xg, 0.0) - jnp.log1p(jnp.exp(-jnp.abs(xg)))) * (LOG2E / GLA_TAU))
                    vs.append(tile(rows, T_VA + h))
                    gates.append(tile(rows, T_RA + h))
                else:
                    lb = lb_all[:, hh * LANES:(hh + 1) * LANES]
                    f = lb + (1.0 - lb) * _sigmoid(tile(rows, T_FB + hh))
                    q = tile(rows, T_QB + hh)
                    qs.append(q * _sigmoid(q))
                    ks.append(1.0 - f)
                    g2s.append(jnp.log2(f))
                    vs.append(tile(rows, T_IB + hh))
                    gates.append(tile(rows, T_GB + hh))
                ids.append((n, h))
        zs = []
        for i in range(0, len(g2s), 2):
            zs += _log_decay_sums(g2s[i:i + 2], sum_mat)
        if h0 == 0:
            fillers = [proj_piece(T_VA), proj_piece(T_RA)]
        elif h0 == 2:
            fillers = [proj_piece(T_VA + 2), proj_piece(T_RA + 2), proj_piece(T_QA), proj_piece(T_KA), gate_piece]
        else:
            fillers = [proj_piece(t + h0 - N_HEADS) for t in (T_QB, T_FB, T_IB, T_GB)]
        outs, sts = _decay_units(qs, ks, vs, [st_ref[n, h] for n, h in ids], zs, level, fillers)
        for i, (n, h) in enumerate(ids):
            rows = slice(n * CHUNK, (n + 1) * CHUNK)
            st_ref[n, h] = sts[i]
            y = _rms(outs[i], nw) * (gates[i] * _sigmoid(gates[i]))
            y_scr[rows, h * LANES:(h + 1) * LANES] = y.astype(bf16)

    m = jnp.dot(y_scr[...], wout_ref[...], preferred_element_type=f32)
    o_ref[...] = (xc_ref[...].reshape(R, D_MODEL) + _rms(m, nw3_ref[...])).reshape(NB, CHUNK, D_MODEL)

    @pl.when((cb == nc - 1) & (j >= 1))
    def _():
        for n in range(NB):
            for h in range(N_HEADS):
                sg_out[n, h] = st_ref[n, h].T[(h % 2) * GLA_DK:(h % 2 + 1) * GLA_DK, :]
                sh_out[n, h] = st_ref[n, N_HEADS + h].T


def _even_layer_prompt(x3, nw2, nw3, w_in, w2, bg, lbt, gnw, hnw, w_out, *, layer, NB=4):
    n_seq, seq_len, _ = x3.shape
    nc = seq_len // CHUNK
    n_tiles = (n_seq // NB) * nc
    R = NB * CHUNK
    smat, lvl = _chunk_tables()
    x_blk = (NB, CHUNK, D_MODEL)

    def tile(lag):
        def index(j):
            t = jnp.clip(j - lag, 0, n_tiles - 1)
            return (t // nc, t % nc, 0)
        return index

    group = lambda j: (jnp.maximum(j - 1, 0) // nc, 0, 0, 0)
    return pl.pallas_call(
        functools.partial(_even_layer_body, NB=NB, nc=nc, layer=layer),
        out_shape=(
            jax.ShapeDtypeStruct(x3.shape, f32),
            jax.ShapeDtypeStruct((n_seq, N_HEADS, GLA_DK, EVEN_DV), f32),
            jax.ShapeDtypeStruct((n_seq, N_HEADS, EVEN_DK, EVEN_DV), f32),
        ),
        grid=(n_tiles + 1,),
        in_specs=[
            pl.BlockSpec(x_blk, tile(0)),
            pl.BlockSpec(x_blk, tile(1)),
            _const_spec((1, D_MODEL)),
            _const_spec((1, D_MODEL)),
            _const_spec(w_in.shape),
            _const_spec(w2.shape),
            _const_spec(bg.shape),
            _const_spec(lbt.shape),
            _const_spec((1, EVEN_DV)),
            _const_spec((1, EVEN_DV)),
            _const_spec(w_out.shape),
            _const_spec(smat.shape),
            _const_spec(lvl.shape),
        ],
        out_specs=(
            pl.BlockSpec(x_blk, tile(1)),
            pl.BlockSpec((NB, N_HEADS, GLA_DK, EVEN_DV), group),
            pl.BlockSpec((NB, N_HEADS, EVEN_DK, EVEN_DV), group),
        ),
        scratch_shapes=[
            pltpu.VMEM((R, EVEN_PW), f32),
            pltpu.VMEM((R, D_MODEL), bf16),
            pltpu.VMEM((R, 2 * N_HEADS * EVEN_DV), bf16),
            pltpu.VMEM((NB, 2 * N_HEADS, EVEN_DV, EVEN_DK), f32),
        ],
        compiler_params=_params("arbitrary"),
        name="even_layer",
    )(x3, x3, nw2, nw3, w_in, w2, bg, lbt, gnw, hnw, w_out, jnp.asarray(smat, bf16), jnp.asarray(lvl))


RET_CHUNK = 256


def _rope_body(cos_ref, sin_ref, *, pos0):
    n, half = cos_ref.shape
    pos = (pos0 + lax.broadcasted_iota(jnp.int32, (n, half), 0)).astype(f32)
    lane = lax.broadcasted_iota(jnp.int32, (n, half), 1).astype(f32)
    ang = pos * jnp.power(ROPE_BASE, -lane / half)
    cos_ref[...] = jnp.cos(ang)
    sin_ref[...] = jnp.sin(ang)


def _rope_tables(n, pos0):
    shape = jax.ShapeDtypeStruct((n, RET_DK // 2), f32)
    return pl.pallas_call(functools.partial(_rope_body, pos0=pos0), out_shape=(shape, shape), name="rope_tables")()


def _ret_cols(h):
    return [slice(off + h * w, off + (h + 1) * w) for off, w in (
        (0, RET_DK), (N_HEADS * RET_DK, RET_DK), (2 * N_HEADS * RET_DK, RET_DV),
        (2 * N_HEADS * RET_DK + N_HEADS * RET_DV, RET_DV))]


def _rotary(x, cos, sin):
    half = x.shape[1] // 2
    x1, x2 = x[:, :half], x[:, half:]
    return jnp.concatenate([x1 * cos - x2 * sin, x1 * sin + x2 * cos], axis=1)


def _odd_layer_body(xa_ref, xc_ref, cos_ref, sin_ref, nw2_ref, nw3_ref, win_ref, rnw_ref, wout_ref,
                    ps_ref, cos_s_ref, sin_s_ref, s_in, o_ref, s_out, ys_ref, s_new,
                    p_buf, hn_scr, y_scr, dec_scr, *, nc, NS, TS):
    C = RET_CHUNK
    j = pl.program_id(0)
    cb = jnp.maximum(j - 1, 0) % nc

    @pl.when(j == 0)
    def _():
        p_buf[...] = jnp.zeros(p_buf.shape, f32)
        ri = lax.broadcasted_iota(jnp.int32, (C, C), 0)
        ci = lax.broadcasted_iota(jnp.int32, (C, C), 1)
        dist = jnp.maximum(ri - ci, 0).astype(f32)
        for h in range(N_HEADS):
            dec_scr[h] = jnp.where(ci <= ri, jnp.exp2(dist * math.log2(1.0 - 2.0 ** (-5.0 - h))), 0.0)

    @pl.when(cb == 0)
    def _():
        s_out[...] = jnp.zeros(s_out.shape, f32)

    hn_scr[...] = _rms(xa_ref[0], nw2_ref[...]).astype(bf16)
    cos, sin = cos_ref[...], sin_ref[...]
    t_k = lax.broadcasted_iota(jnp.int32, (C, RET_DK), 0).astype(f32)

    RS = NS * TS
    cos_s = jnp.concatenate([cos_s_ref[...]] * NS, axis=0)
    sin_s = jnp.concatenate([sin_s_ref[...]] * NS, axis=0)
    ri = lax.broadcasted_iota(jnp.int32, (RS, RS), 0)
    ci = lax.broadcasted_iota(jnp.int32, (RS, RS), 1)
    visible = ((ri // TS) == (ci // TS)) & (ci <= ri)
    dist_s = jnp.where(visible, ri - ci, 0).astype(f32)
    t_s = (lax.broadcasted_iota(jnp.int32, (RS, RET_DK), 0) % TS).astype(f32)

    lg = [math.log2(1.0 - 2.0 ** (-5.0 - h)) for h in range(N_HEADS)]
    colsl = [_ret_cols(h) for h in range(N_HEADS)]
    for h in range(N_HEADS):
        log2_gamma = lg[h]
        cols = colsl[h]
        q = _rotary(ps_ref[:, cols[0]], cos_s, sin_s)
        k = _rotary(ps_ref[:, cols[1]], cos_s, sin_s) * (RET_DK ** -0.5)
        vb = ps_ref[:, cols[2]].astype(bf16)
        gate = ps_ref[:, cols[3]]
        decay = jnp.where(visible, jnp.exp2(dist_s * log2_gamma), 0.0)
        s = lax.dot_general(q.astype(bf16), k.astype(bf16), _NT, preferred_element_type=f32) * decay
        o = jnp.dot(s.astype(bf16), vb, preferred_element_type=f32)
        q_dec = (q * jnp.exp2((t_s + 1.0) * log2_gamma)).astype(bf16)
        k_dec = (k * jnp.exp2((TS - 1.0 - t_s) * log2_gamma)).astype(bf16)
        parts = []
        for n in range(NS):
            rows = slice(n * TS, (n + 1) * TS)
            st = s_in[n, h]
            parts.append(jnp.dot(q_dec[rows], st.astype(bf16), preferred_element_type=f32))
            upd = lax.dot_general(k_dec[rows], vb[rows], _TN, preferred_element_type=f32)
            s_new[n, h] = st * (2.0 ** (TS * log2_gamma)) + upd
        o = o + jnp.concatenate(parts, axis=0)
        y = _rms(o, rnw_ref[...]) * (gate * _sigmoid(gate))
        ys_ref[:, h * RET_DV:(h + 1) * RET_DV] = y.astype(bf16)

    for h in range(N_HEADS):
        log2_gamma = lg[h]
        cols = colsl[h]
        q = _rotary(p_buf[:, cols[0]], cos, sin)
        k = _rotary(p_buf[:, cols[1]], cos, sin) * (RET_DK ** -0.5)
        vb = p_buf[:, cols[2]].astype(bf16)
        gate = p_buf[:, cols[3]]
        s = lax.dot_general(q.astype(bf16), k.astype(bf16), _NT, preferred_element_type=f32) * dec_scr[h]
        o = jnp.dot(s.astype(bf16), vb, preferred_element_type=f32)
        st = s_out[0, h]
        q_dec = (q * jnp.exp2((t_k + 1.0) * log2_gamma)).astype(bf16)
        o = o + jnp.dot(q_dec, st.astype(bf16), preferred_element_type=f32)
        k_dec = (k * jnp.exp2((C - 1.0 - t_k) * log2_gamma)).astype(bf16)
        upd = lax.dot_general(k_dec, vb, _TN, preferred_element_type=f32)
        s_out[0, h] = st * (2.0 ** (C * log2_gamma)) + upd
        y = _rms(o, rnw_ref[...]) * (gate * _sigmoid(gate))
        y_scr[:, h * RET_DV:(h + 1) * RET_DV] = y.astype(bf16)
        for cs in cols:
            p_buf[:, cs] = jnp.dot(hn_scr[...], win_ref[:, cs], preferred_element_type=f32)

    m = jnp.dot(y_scr[...], wout_ref[...], preferred_element_type=f32)
    o_ref[0] = xc_ref[0] + _rms(m, nw3_ref[...])


def _odd_layer(x3, ps, s_ret, nw2, nw3, w_in, rnw, w_out):
    n_seq, seq_len, _ = x3.shape
    n_s = s_ret.shape[0]
    TS = ps.shape[0] // n_s
    C = RET_CHUNK
    nc = seq_len // C
    n_tiles = n_seq * nc
    NS = n_s // n_tiles
    assert NS * n_tiles == n_s and (NS * TS) % 16 == 0
    cos, sin = _rope_tables(seq_len, 0)
    cos_s, sin_s = _rope_tables(TS, PAST_LEN)
    x_blk = (1, C, D_MODEL)

    def tile(lag):
        def index(j):
            t = jnp.clip(j - lag, 0, n_tiles - 1)
            return (t // nc, t % nc, 0)
        return index

    short = lambda j: jnp.minimum(j, n_tiles - 1)
    rope_spec = pl.BlockSpec((C, RET_DK // 2), lambda j: (jnp.maximum(j - 1, 0) % nc, 0))
    state_s_spec = pl.BlockSpec((NS, N_HEADS, RET_DK, RET_DV), lambda j: (short(j), 0, 0, 0))
    return pl.pallas_call(
        functools.partial(_odd_layer_body, nc=nc, NS=NS, TS=TS),
        out_shape=(
            jax.ShapeDtypeStruct(x3.shape, f32),
            jax.ShapeDtypeStruct((n_seq, N_HEADS, RET_DK, RET_DV), f32),
            jax.ShapeDtypeStruct((n_s * TS, N_HEADS * RET_DV), bf16),
            jax.ShapeDtypeStruct(s_ret.shape, f32),
        ),
        grid=(n_tiles + 1,),
        in_specs=[
            pl.BlockSpec(x_blk, tile(0)),
            pl.BlockSpec(x_blk, tile(1)),
            rope_spec,
            rope_spec,
            _const_spec((1, D_MODEL)),
            _const_spec((1, D_MODEL)),
            _const_spec(w_in.shape),
            _const_spec((1, RET_DV)),
            _const_spec(w_out.shape),
            pl.BlockSpec((NS * TS, ODD_P), lambda j: (short(j), 0)),
            _const_spec((TS, RET_DK // 2)),
            _const_spec((TS, RET_DK // 2)),
            state_s_spec,
        ],
        out_specs=(
            pl.BlockSpec(x_blk, tile(1)),
            pl.BlockSpec((1, N_HEADS, RET_DK, RET_DV), lambda j: (jnp.maximum(j - 1, 0) // nc, 0, 0, 0)),
            pl.BlockSpec((NS * TS, N_HEADS * RET_DV), lambda j: (short(j), 0)),
            state_s_spec,
        ),
        scratch_shapes=[
            pltpu.VMEM((C, ODD_P), f32),
            pltpu.VMEM((C, D_MODEL), bf16),
            pltpu.VMEM((C, N_HEADS * RET_DV), bf16),
            pltpu.VMEM((N_HEADS, C, C), f32),
        ],
        compiler_params=_params("arbitrary"),
        name="odd_layer",
    )(x3, x3, cos, sin, nw2, nw3, w_in, rnw, w_out, ps, cos_s, sin_s, s_ret)


def _pack_even_w_in(w):
    lr0 = 2 * N_HEADS * GLA_DK + 2 * N_HEADS * EVEN_DV
    lr = jnp.pad(w[:, lr0:lr0 + GLA_LOWRANK], ((0, 0), (0, LANES - GLA_LOWRANK)))
    return jnp.concatenate([w[:, :lr0], w[:, lr0 + GLA_LOWRANK:], lr], axis=1)


def _mixer_even_prompt(x, l, w, row, n_seq, seq_len):
    e = l // 2
    x3, sg, sh = _even_layer_prompt(x.reshape(n_seq, seq_len, D_MODEL), row(2), row(3), w["even_in"][e], w["gla_w2"][e],
                                    w["gla_bg"][e], w["lb_table"], w["gla_nw"][e], w["hgrn_nw"][e], w["even_out"][e],
                                    layer=l)
    return x3.reshape(n_seq * seq_len, D_MODEL), sg, sh


def _mixer_even_sample(x, l, w, row, s_gla, s_hgrn, n_seq, seq_len):
    e = l // 2
    p = _proj(x, row(2), w["even_in"][e], tm=512, tn=EVEN_P)
    y, sg, sh = _even_attn(p, w["gla_w2"][e], w["gla_bg"][e], w["lb_table"], w["gla_nw"][e], w["hgrn_nw"][e],
                           s_gla[e], s_hgrn[e], n_seq=n_seq, seq_len=seq_len, C=seq_len, NB=8, layer=l)
    return _out_block(y, x, w["even_out"][e], row(3)), sg, sh


def _mixer_odd(xp, xs, l, w, row, s_ret, n_seq, seq_len):
    o = l // 2
    ps = _proj(xs, row(2), w["odd_in"][o], tm=256, tn=12 * LANES)
    x3, sr_p, ys, sr_s = _odd_layer(xp.reshape(n_seq, seq_len, D_MODEL), ps, s_ret[o], row(2), row(3), w["odd_in"][o],
                                    w["ret_nw"][o], w["odd_out"][o])
    return x3.reshape(n_seq * seq_len, D_MODEL), _out_block(ys, xs, w["odd_out"][o], row(3)), sr_p, sr_s


def kernel(x_prompt, x_sample, state_gla, state_hgrn, state_ret, norm_w, ffn_w_gate, ffn_w_up, ffn_w_down, even_w_in, gla_w_gate2, gla_b_gate, gla_norm_w, hgrn_lb_table, hgrn_norm_w, even_w_out, odd_w_in, ret_norm_w, odd_w_out):
    bp, tp, _ = x_prompt.shape
    bs, ts, _ = x_sample.shape
    n_even = even_w_in.shape[0]
    nw = norm_w.astype(f32)
    ffn_w = [w_.astype(bf16) for w_ in (ffn_w_gate, ffn_w_up, ffn_w_down)]
    w = {
        "even_in": jnp.stack([_pack_even_w_in(even_w_in[e]) for e in range(n_even)]).astype(bf16),
        "gla_w2": jnp.pad(gla_w_gate2, ((0, 0), (0, LANES - GLA_LOWRANK), (0, 0))).astype(bf16),
        "gla_bg": gla_b_gate[:, None, :].astype(f32),
        "lb_table": hgrn_lb_table.astype(f32),
        "gla_nw": gla_norm_w[:, None, :].astype(f32),
        "hgrn_nw": hgrn_norm_w[:, None, :].astype(f32),
        "even_out": even_w_out.astype(bf16),
        "odd_in": odd_w_in.astype(bf16),
        "ret_nw": ret_norm_w[:, None, :].astype(f32),
        "odd_out": odd_w_out.astype(bf16),
    }
    xp = x_prompt.reshape(bp * tp, D_MODEL)
    xs = x_sample.reshape(bs * ts, D_MODEL)
    new_p, new_s = ([], [], []), ([], [], [])
    for l in range(nw.shape[0]):
        row = lambda i, l=l: nw[l, i][None, :]
        xp, xs = _ffn(xp, xs, row(0), row(1), *ffn_w, l, 0)
        if l % 2 == 0:
            xp, sg, sh = _mixer_even_prompt(xp, l, w, row, bp, tp)
            new_p[0].append(sg)
            new_p[1].append(sh)
            xs, sg, sh = _mixer_even_sample(xs, l, w, row, state_gla, state_hgrn, bs, ts)
            new_s[0].append(sg)
            new_s[1].append(sh)
        else:
            xp, xs, sr_p, sr_s = _mixer_odd(xp, xs, l, w, row, state_ret, bp, tp)
            new_p[2].append(sr_p)
            new_s[2].append(sr_s)
        xp, xs = _ffn(xp, xs, row(4), row(5), *ffn_w, l, 1)
    gla_p, hgrn_p, ret_p = (jnp.stack(v) for v in new_p)
    gla_s, hgrn_s, ret_s = (jnp.stack(v) for v in new_s)
    return (xp.reshape(bp, tp, D_MODEL), xs.reshape(bs, ts, D_MODEL), gla_p, hgrn_p, ret_p, gla_s, hgrn_s, ret_s)
```

```python
import functools
import math

import jax
import jax.numpy as jnp
import numpy as np
from jax import lax
from jax.experimental import pallas as pl
from jax.experimental.pallas import tpu as pltpu

f32 = jnp.float32
bf16 = jnp.bfloat16

D_MODEL = 1024
D_FF = 2816
EPS = 1e-6
PAST_LEN = 16384
ROPE_BASE = 10000.0
GLA_TAU = 16.0
N_HEADS = 4
GLA_DK = 64
EVEN_DK = 128
EVEN_DV = 128
GLA_LOWRANK = 16
RET_DK = 256
RET_DV = 512
LANES = 128
SUBLANES = 8
T_QA, T_KA, T_VA, T_RA, T_QB, T_FB, T_IB, T_GB, T_LR = 0, 2, 4, 8, 12, 16, 20, 24, 28
EVEN_P = 29 * LANES
ODD_P = 2 * N_HEADS * RET_DK + 2 * N_HEADS * RET_DV
VMEM_LIMIT = 58 * 1024 * 1024

_NT = (((1,), (1,)), ((), ()))
_TN = (((0,), (0,)), ((), ()))


def _rms(x, w):
    return x * lax.rsqrt(jnp.mean(x * x, axis=-1, keepdims=True) + EPS) * w


def _sigmoid(x):
    return 1.0 / (1.0 + jnp.exp(-x))


def _const_spec(shape):
    return pl.BlockSpec(shape, lambda *_: (0,) * len(shape), pipeline_mode=pl.Buffered(1))


def _params(*sem):
    return pltpu.CompilerParams(dimension_semantics=sem, vmem_limit_bytes=VMEM_LIMIT)


FFN_TF = 256
FFN_CHUNKS = D_FF // FFN_TF


def _ffn_tile(x_ref, nwa_ref, nwb_ref, wg_b, wu_b, wd_b, o_ref, acc_ref):
    x = x_ref[...]
    hn = _rms(x, nwa_ref[...]).astype(bf16)
    for j in range(FFN_CHUNKS):
        g = jnp.dot(hn, wg_b[j], preferred_element_type=f32)
        u = jnp.dot(hn, wu_b[j], preferred_element_type=f32)
        a = (g * _sigmoid(g) * u).astype(bf16)
        part = jnp.dot(a, wd_b[j], preferred_element_type=f32)
        if j == 0:
            acc_ref[...] = part
        else:
            acc_ref[...] += part
    o_ref[...] = x + 0.5 * _rms(acc_ref[...], nwb_ref[...])


def _ffn_body(xa_ref, xb_ref, nwa_ref, nwb_ref, wg_ref, wu_ref, wd_ref, oa_ref, ob_ref, wg_b, wu_b, wd_b, acc_ref, *, na):
    i = pl.program_id(0)

    @pl.when(i < FFN_CHUNKS)
    def _():
        wg_b[i] = wg_ref[...].astype(bf16)
        wu_b[i] = wu_ref[...].astype(bf16)
        wd_b[i] = wd_ref[...].astype(bf16)

    @pl.when((i >= FFN_CHUNKS) & (i < FFN_CHUNKS + na))
    def _():
        _ffn_tile(xa_ref, nwa_ref, nwb_ref, wg_b, wu_b, wd_b, oa_ref, acc_ref)

    @pl.when(i >= FFN_CHUNKS + na)
    def _():
        _ffn_tile(xb_ref, nwa_ref, nwb_ref, wg_b, wu_b, wd_b, ob_ref, acc_ref)


def _ffn(xa, xb, nwa, nwb, wg, wu, wd, l, pos, tm=512):
    na, nb = xa.shape[0] // tm, xb.shape[0] // tm
    chunk = lambda i: jnp.minimum(i, FFN_CHUNKS - 1)
    spec_a = pl.BlockSpec((tm, D_MODEL), lambda i: (jnp.clip(i - FFN_CHUNKS, 0, na - 1), 0))
    spec_b = pl.BlockSpec((tm, D_MODEL), lambda i: (jnp.maximum(i - FFN_CHUNKS - na, 0), 0))
    col_spec = pl.BlockSpec((None, None, D_MODEL, FFN_TF), lambda i: (l, pos, 0, chunk(i)))
    row_spec = pl.BlockSpec((None, None, FFN_TF, D_MODEL), lambda i: (l, pos, chunk(i), 0))
    return pl.pallas_call(
        functools.partial(_ffn_body, na=na),
        out_shape=(jax.ShapeDtypeStruct(xa.shape, f32), jax.ShapeDtypeStruct(xb.shape, f32)),
        grid=(FFN_CHUNKS + na + nb,),
        in_specs=[
            spec_a,
            spec_b,
            _const_spec((1, D_MODEL)),
            _const_spec((1, D_MODEL)),
            col_spec,
            col_spec,
            row_spec,
        ],
        out_specs=(spec_a, spec_b),
        scratch_shapes=[
            pltpu.VMEM((FFN_CHUNKS, D_MODEL, FFN_TF), bf16),
            pltpu.VMEM((FFN_CHUNKS, D_MODEL, FFN_TF), bf16),
            pltpu.VMEM((FFN_CHUNKS, FFN_TF, D_MODEL), bf16),
            pltpu.VMEM((tm, D_MODEL), f32),
        ],
        compiler_params=_params("arbitrary"),
        name="ffn",
    )(xa, xb, nwa, nwb, wg, wu, wd)


def _proj_body(x_ref, nw_ref, w_ref, o_ref, *, tn):
    hn = _rms(x_ref[...], nw_ref[...]).astype(bf16)
    for j in range(w_ref.shape[1] // tn):
        cols = slice(j * tn, (j + 1) * tn)
        o_ref[:, cols] = jnp.dot(hn, w_ref[:, cols], preferred_element_type=f32)


def _proj(x, nw, w, tm, tn):
    n, p = x.shape[0], w.shape[1]
    return pl.pallas_call(
        functools.partial(_proj_body, tn=tn),
        out_shape=jax.ShapeDtypeStruct((n, p), f32),
        grid=(n // tm,),
        in_specs=[
            pl.BlockSpec((tm, D_MODEL), lambda i: (i, 0)),
            _const_spec((1, D_MODEL)),
            _const_spec((D_MODEL, p)),
        ],
        out_specs=pl.BlockSpec((tm, p), lambda i: (i, 0)),
        compiler_params=_params("arbitrary"),
        name="proj",
    )(x, nw, w)


def _out_body(y_ref, x_ref, w_ref, nw_ref, o_ref):
    m = jnp.dot(y_ref[...], w_ref[...], preferred_element_type=f32)
    o_ref[...] = x_ref[...] + _rms(m, nw_ref[...])


def _out_block(y, x, w, nw, tm=512):
    n, dy = y.shape
    return pl.pallas_call(
        _out_body,
        out_shape=jax.ShapeDtypeStruct((n, D_MODEL), f32),
        grid=(n // tm,),
        in_specs=[
            pl.BlockSpec((tm, dy), lambda i: (i, 0)),
            pl.BlockSpec((tm, D_MODEL), lambda i: (i, 0)),
            _const_spec((dy, D_MODEL)),
            _const_spec((1, D_MODEL)),
        ],
        out_specs=pl.BlockSpec((tm, D_MODEL), lambda i: (i, 0)),
        compiler_params=_params("arbitrary"),
        name="mixer_out",
    )(y, x, w, nw)


def _split3(g):
    hi = g.astype(bf16)
    r1 = g - hi.astype(f32)
    mid = r1.astype(bf16)
    lo = (r1 - mid.astype(f32)).astype(bf16)
    return jnp.concatenate([hi, mid, lo], axis=1)


def _roll_in_groups(x, d):
    r = x.shape[0]
    return pltpu.roll(x.reshape(r // SUBLANES, SUBLANES, LANES), d, 1).reshape(r, LANES)


def _near_diagonal(q, k, b, v, ones):
    r = q.shape[0]
    sub = lax.broadcasted_iota(jnp.int32, (r, LANES), 0) % SUBLANES
    prods = [q * k]
    for d in range(1, SUBLANES):
        decay = jnp.exp(jnp.minimum(b - _roll_in_groups(b, d), 0.0))
        prods.append(q * _roll_in_groups(k, d) * decay)
    rs = jnp.dot(jnp.concatenate(prods, axis=0).astype(bf16), ones, preferred_element_type=f32)
    o = rs[:r] * v
    for d in range(1, SUBLANES):
        o = o + jnp.where(sub >= d, rs[d * r:(d + 1) * r], 0.0) * _roll_in_groups(v, d)
    return o


def _even_attn_body(p_ref, w2_ref, bg_ref, lbt_ref, gnw_ref, hnw_ref, sg_in, sh_in, y_ref, sg_out, sh_out, *, C, NB, layer):
    R = NB * C
    ri = lax.broadcasted_iota(jnp.int32, (R, R), 0)
    ci = lax.broadcasted_iota(jnp.int32, (R, R), 1)
    sum_mat = jnp.where(((ri // C) == (ci // C)) & (ci <= ri), 1.0, 0.0).astype(bf16)
    ones = jnp.ones((LANES, LANES), bf16)
    ones_c = jnp.ones((C, LANES), bf16)
    zeros_half = jnp.zeros((GLA_DK, EVEN_DV), f32)

    lr = p_ref[:, T_LR * LANES:(T_LR + 1) * LANES].astype(bf16)
    xg = jnp.dot(lr, w2_ref[...], preferred_element_type=f32) + bg_ref[...]
    log_alpha = (jnp.minimum(xg, 0.0) - jnp.log1p(jnp.exp(-jnp.abs(xg)))) * (1.0 / GLA_TAU)
    low_half = lax.broadcasted_iota(jnp.int32, (R, LANES), 1) < GLA_DK

    lbt = lbt_ref[...]
    lbe = jnp.exp(lbt - jnp.max(lbt, axis=0, keepdims=True))
    lb_all = jnp.sum(lbe[:layer + 1], axis=0, keepdims=True) / jnp.sum(lbe, axis=0, keepdims=True)

    def col(tile):
        return p_ref[:, tile * LANES:(tile + 1) * LANES]

    for h in range(2 * N_HEADS):
        hh = h % N_HEADS
        if h < N_HEADS:
            mine = low_half if h % 2 == 0 else jnp.logical_not(low_half)
            q = jnp.where(mine, col(T_QA + h // 2), 0.0) * (GLA_DK ** -0.5)
            k = jnp.where(mine, col(T_KA + h // 2), 0.0)
            v = col(T_VA + h)
            gate = col(T_RA + h)
            g = log_alpha[:, (h // 2) * LANES:(h // 2 + 1) * LANES]
            nw = gnw_ref[...]
        else:
            qb = col(T_QB + hh)
            q = qb * _sigmoid(qb)
            lb = lb_all[:, hh * LANES:(hh + 1) * LANES]
            f = lb + (1.0 - lb) * _sigmoid(col(T_FB + hh))
            k = 1.0 - f
            g = jnp.log(f)
            v = col(T_IB + hh)
            gate = col(T_GB + hh)
            nw = hnw_ref[...]

        g3 = _split3(g)
        sums = jnp.dot(sum_mat, g3, preferred_element_type=f32)
        b = sums[:, :LANES] + sums[:, LANES:2 * LANES] + sums[:, 2 * LANES:]
        o = _near_diagonal(q, k, b, v, ones)
        q_dec = (q * jnp.exp(b)).astype(bf16)
        vb = v.astype(bf16)
        parts = []
        for n in range(NB):
            rows = slice(n * C, (n + 1) * C)
            b_last = b[(n + 1) * C - 1:(n + 1) * C, :]
            k_dec = (k[rows] * jnp.exp(b_last - b[rows])).astype(bf16)
            tot = lax.dot_general(g3[rows], ones_c, _TN, preferred_element_type=f32)
            decay = jnp.exp(tot[:LANES] + tot[LANES:2 * LANES] + tot[2 * LANES:])
            upd = lax.dot_general(k_dec, vb[rows], _TN, preferred_element_type=f32)
            if h < N_HEADS:
                lo, hi = (h % 2) * GLA_DK, (h % 2 + 1) * GLA_DK
                st = sg_in[n, h]
                st_pad = jnp.concatenate([st, zeros_half] if h % 2 == 0 else [zeros_half, st], axis=0)
                sg_out[n, h] = st * decay[lo:hi] + upd[lo:hi]
            else:
                st_pad = sh_in[n, hh]
                sh_out[n, hh] = st_pad * decay + upd
            parts.append(jnp.dot(q_dec[rows], st_pad.astype(bf16), preferred_element_type=f32))
        o = o + (parts[0] if NB == 1 else jnp.concatenate(parts, axis=0))

        y = _rms(o, nw) * (gate * _sigmoid(gate))
        y_ref[:, h * LANES:(h + 1) * LANES] = y.astype(bf16)


def _even_attn(p, w2, bg, lbt, gnw, hnw, sg, sh, *, n_seq, seq_len, C, NB, layer):
    assert C == SUBLANES, "pairs further apart than one 8-row group are not handled here"
    R = NB * C
    nc = seq_len // C
    assert nc == 1, "one chunk per sequence"
    grid = (n_seq // NB, nc)
    row_map = lambda i, c: (i * nc + c, 0)
    in_specs = [
        pl.BlockSpec((R, EVEN_P), row_map),
        _const_spec(w2.shape),
        _const_spec(bg.shape),
        _const_spec(lbt.shape),
        _const_spec((1, EVEN_DV)),
        _const_spec((1, EVEN_DV)),
    ]
    sg_spec = pl.BlockSpec((NB, N_HEADS, GLA_DK, EVEN_DV), lambda i, c: (i, 0, 0, 0))
    sh_spec = pl.BlockSpec((NB, N_HEADS, EVEN_DK, EVEN_DV), lambda i, c: (i, 0, 0, 0))
    in_specs += [sg_spec, sh_spec]
    return pl.pallas_call(
        functools.partial(_even_attn_body, C=C, NB=NB, layer=layer),
        out_shape=(
            jax.ShapeDtypeStruct((n_seq * seq_len, 2 * N_HEADS * EVEN_DV), bf16),
            jax.ShapeDtypeStruct((n_seq, N_HEADS, GLA_DK, EVEN_DV), f32),
            jax.ShapeDtypeStruct((n_seq, N_HEADS, EVEN_DK, EVEN_DV), f32),
        ),
        grid=grid,
        in_specs=in_specs,
        out_specs=(pl.BlockSpec((R, 2 * N_HEADS * EVEN_DV), row_map), sg_spec, sh_spec),
        compiler_params=_params("arbitrary", "arbitrary"),
        name="even_attn",
    )(p, w2, bg, lbt, gnw, hnw, sg, sh)


CHUNK = 64
N_LEVELS = 6
N_SUM_LEVELS = 3
LOG2E = 1.4426950408889634
EVEN_PW = (T_LR + 2) * LANES


def _chunk_tables():
    t = np.arange(CHUNK)[:, None]
    j = np.arange(CHUNK)[None, :]
    mats = []
    for l in range(N_SUM_LEVELS):
        hs = 1 << l
        m = (t // (2 * hs)) * (2 * hs) + hs
        mats.append(np.where(t >= m, (j > m) & (j <= t), (j > t) & (j <= m)))
    mats.append(j <= t)
    a = np.concatenate(mats, axis=0).astype(np.float32)
    x = t ^ j
    lvl = np.where(j > t, N_LEVELS + 1, np.where(j == t, 0, np.floor(np.log2(np.maximum(x, 1))).astype(np.int64) + 1))
    return np.concatenate([a, a], axis=1), lvl.astype(np.int32)


def _log_decay_sums(g2_list, sum_mat):
    hi = [g.astype(bf16) for g in g2_list]
    lo = [(g - h.astype(f32)).astype(bf16) for g, h in zip(g2_list, hi)]
    rhs = jnp.concatenate([jnp.concatenate(hi, axis=1), jnp.concatenate(lo, axis=1)], axis=0)
    z = jnp.dot(sum_mat, rhs, preferred_element_type=f32)
    return [z[:, i * LANES:(i + 1) * LANES] for i in range(len(g2_list))]


def _level_log_decay(z, b, l):
    if l < N_SUM_LEVELS:
        return z[l * CHUNK:(l + 1) * CHUNK]
    hs = 1 << l
    ref = jnp.concatenate([jnp.broadcast_to(b[m:m + 1], (2 * hs, LANES)) for m in range(hs, CHUNK, 2 * hs)], axis=0)
    d = b - ref
    return jnp.minimum(d, -d)


def _decay_units(qs, ks, vs, sts, zs, level, fillers):
    n = len(qs)
    fillers = list(fillers)
    vbs = [v.astype(bf16) for v in vs]
    bs = [z[N_SUM_LEVELS * CHUNK:] for z in zs]
    scores = [0.0] * n
    for l in range(N_LEVELS):
        es = [jnp.exp2(_level_log_decay(zs[i], bs[i], l)) for i in range(n)]
        ss = [lax.dot_general((qs[i] * es[i]).astype(bf16), (ks[i] * es[i]).astype(bf16), _NT,
                              preferred_element_type=f32) for i in range(n)]
        scores = [jnp.where(level == l + 1, ss[i], scores[i]) for i in range(n)]
        if fillers:
            fillers.pop(0)()
    b_lasts = [b[CHUNK - 1:] for b in bs]
    o_in = [jnp.dot(scores[i].astype(bf16), vbs[i], preferred_element_type=f32) for i in range(n)]
    o_st = [lax.dot_general((qs[i] * jnp.exp2(bs[i])).astype(bf16), sts[i].astype(bf16), _NT,
                            preferred_element_type=f32) for i in range(n)]
    upd = [lax.dot_general(vbs[i], (ks[i] * jnp.exp2(b_lasts[i] - bs[i])).astype(bf16), _TN,
                           preferred_element_type=f32) for i in range(n)]
    while fillers:
        fillers.pop(0)()
    outs = [jnp.sum(qs[i] * ks[i], axis=-1, keepdims=True) * vs[i] + o_in[i] + o_st[i] for i in range(n)]
    return outs, [sts[i] * jnp.exp2(b_lasts[i]) + upd[i] for i in range(n)]


def _even_layer_body(xa_ref, xc_ref, nw2_ref, nw3_ref, win_ref, w2_ref, bg_ref, lbt_ref, gnw_ref, hnw_ref, wout_ref,
                     smat_ref, lvl_ref, o_ref, sg_out, sh_out, p_buf, hn_scr, y_scr, st_ref, *, NB, nc, layer):
    R = NB * CHUNK
    j = pl.program_id(0)
    cb = jnp.maximum(j - 1, 0) % nc
    xg0 = T_LR * LANES

    @pl.when(j == 0)
    def _():
        p_buf[...] = jnp.zeros(p_buf.shape, f32)

    @pl.when(cb == 0)
    def _():
        st_ref[...] = jnp.zeros(st_ref.shape, f32)

    hn_scr[...] = _rms(xa_ref[...].reshape(R, D_MODEL), nw2_ref[...]).astype(bf16)

    lbt = lbt_ref[...]
    lbe = jnp.exp(lbt - jnp.max(lbt, axis=0, keepdims=True))
    lb_all = jnp.sum(lbe[:layer + 1], axis=0, keepdims=True) / jnp.sum(lbe, axis=0, keepdims=True)
    sum_mat = smat_ref[...]
    level = lvl_ref[...]

    low_half = lax.broadcasted_iota(jnp.int32, (CHUNK, LANES), 1) < GLA_DK

    def proj_piece(tile):
        def run():
            cols = slice(tile * LANES, (tile + 2) * LANES)
            p_buf[:, cols] = jnp.dot(hn_scr[...], win_ref[:, cols], preferred_element_type=f32)
        return run

    def gate_piece():
        lr = jnp.dot(hn_scr[...], win_ref[:, xg0:], preferred_element_type=f32).astype(bf16)
        p_buf[:, xg0:] = jnp.dot(lr, w2_ref[...], preferred_element_type=f32) + bg_ref[...]

    def tile(rows, t):
        return p_buf[rows, t * LANES:(t + 1) * LANES]

    for h0 in range(0, 2 * N_HEADS, 2):
        nw = gnw_ref[...] if h0 < N_HEADS else hnw_ref[...]
        qs, ks, g2s, vs, gates, ids = [], [], [], [], [], []
        for h in (h0, h0 + 1):
            hh = h % N_HEADS
            for n in range(NB):
                rows = slice(n * CHUNK, (n + 1) * CHUNK)
                if h < N_HEADS:
                    mine = low_half if h % 2 == 0 else jnp.logical_not(low_half)
                    xg = tile(rows, T_LR + h // 2)
                    qs.append(jnp.where(mine, tile(rows, T_QA + h // 2), 0.0) * (GLA_DK ** -0.5))
                    ks.append(jnp.where(mine, tile(rows, T_KA + h // 2), 0.0))
                    g2s.append((jnp.minimum(xg, 0.0) - jnp.log1p(jnp.exp(-jnp.abs(xg)))) * (LOG2E / GLA_TAU))
                    vs.append(tile(rows, T_VA + h))
                    gates.append(tile(rows, T_RA + h))
                else:
                    lb = lb_all[:, hh * LANES:(hh + 1) * LANES]
                    f = lb + (1.0 - lb) * _sigmoid(tile(rows, T_FB + hh))
                    q = tile(rows, T_QB + hh)
                    qs.append(q * _sigmoid(q))
                    ks.append(1.0 - f)
                    g2s.append(jnp.log2(f))
                    vs.append(tile(rows, T_IB + hh))
                    gates.append(tile(rows, T_GB + hh))
                ids.append((n, h))
        zs = []
        for i in range(0, len(g2s), 2):
            zs += _log_decay_sums(g2s[i:i + 2], sum_mat)
        if h0 == 0:
            fillers = [proj_piece(T_VA), proj_piece(T_RA)]
        elif h0 == 2:
            fillers = [proj_piece(T_VA + 2), proj_piece(T_RA + 2), proj_piece(T_QA), proj_piece(T_KA), gate_piece]
        else:
            fillers = [proj_piece(t + h0 - N_HEADS) for t in (T_QB, T_FB, T_IB, T_GB)]
        outs, sts = _decay_units(qs, ks, vs, [st_ref[n, h] for n, h in ids], zs, level, fillers)
        for i, (n, h) in enumerate(ids):
            rows = slice(n * CHUNK, (n + 1) * CHUNK)
            st_ref[n, h] = sts[i]
            y = _rms(outs[i], nw) * (gates[i] * _sigmoid(gates[i]))
            y_scr[rows, h * LANES:(h + 1) * LANES] = y.astype(bf16)

    m = jnp.dot(y_scr[...], wout_ref[...], preferred_element_type=f32)
    o_ref[...] = (xc_ref[...].reshape(R, D_MODEL) + _rms(m, nw3_ref[...])).reshape(NB, CHUNK, D_MODEL)

    @pl.when((cb == nc - 1) & (j >= 1))
    def _():
        for n in range(NB):
            for h in range(N_HEADS):
                sg_out[n, h] = st_ref[n, h].T[(h % 2) * GLA_DK:(h % 2 + 1) * GLA_DK, :]
                sh_out[n, h] = st_ref[n, N_HEADS + h].T


def _even_layer_prompt(x3, nw2, nw3, w_in, w2, bg, lbt, gnw, hnw, w_out, *, layer, NB=4):
    n_seq, seq_len, _ = x3.shape
    nc = seq_len // CHUNK
    n_tiles = (n_seq // NB) * nc
    R = NB * CHUNK
    smat, lvl = _chunk_tables()
    x_blk = (NB, CHUNK, D_MODEL)

    def tile(lag):
        def index(j):
            t = jnp.clip(j - lag, 0, n_tiles - 1)
            return (t // nc, t % nc, 0)
        return index

    group = lambda j: (jnp.maximum(j - 1, 0) // nc, 0, 0, 0)
    return pl.pallas_call(
        functools.partial(_even_layer_body, NB=NB, nc=nc, layer=layer),
        out_shape=(
            jax.ShapeDtypeStruct(x3.shape, f32),
            jax.ShapeDtypeStruct((n_seq, N_HEADS, GLA_DK, EVEN_DV), f32),
            jax.ShapeDtypeStruct((n_seq, N_HEADS, EVEN_DK, EVEN_DV), f32),
        ),
        grid=(n_tiles + 1,),
        in_specs=[
            pl.BlockSpec(x_blk, tile(0)),
            pl.BlockSpec(x_blk, tile(1)),
            _const_spec((1, D_MODEL)),
            _const_spec((1, D_MODEL)),
            _const_spec(w_in.shape),
            _const_spec(w2.shape),
            _const_spec(bg.shape),
            _const_spec(lbt.shape),
            _const_spec((1, EVEN_DV)),
            _const_spec((1, EVEN_DV)),
            _const_spec(w_out.shape),
            _const_spec(smat.shape),
            _const_spec(lvl.shape),
        ],
        out_specs=(
            pl.BlockSpec(x_blk, tile(1)),
            pl.BlockSpec((NB, N_HEADS, GLA_DK, EVEN_DV), group),
            pl.BlockSpec((NB, N_HEADS, EVEN_DK, EVEN_DV), group),
        ),
        scratch_shapes=[
            pltpu.VMEM((R, EVEN_PW), f32),
            pltpu.VMEM((R, D_MODEL), bf16),
            pltpu.VMEM((R, 2 * N_HEADS * EVEN_DV), bf16),
            pltpu.VMEM((NB, 2 * N_HEADS, EVEN_DV, EVEN_DK), f32),
        ],
        compiler_params=_params("arbitrary"),
        name="even_layer",
    )(x3, x3, nw2, nw3, w_in, w2, bg, lbt, gnw, hnw, w_out, jnp.asarray(smat, bf16), jnp.asarray(lvl))


RET_CHUNK = 256


def _rope_body(cos_ref, sin_ref, *, pos0):
    n, half = cos_ref.shape
    pos = (pos0 + lax.broadcasted_iota(jnp.int32, (n, half), 0)).astype(f32)
    lane = lax.broadcasted_iota(jnp.int32, (n, half), 1).astype(f32)
    ang = pos * jnp.power(ROPE_BASE, -lane / half)
    cos_ref[...] = jnp.cos(ang)
    sin_ref[...] = jnp.sin(ang)


def _rope_tables(n, pos0):
    shape = jax.ShapeDtypeStruct((n, RET_DK // 2), f32)
    return pl.pallas_call(functools.partial(_rope_body, pos0=pos0), out_shape=(shape, shape), name="rope_tables")()


def _ret_cols(h):
    return [slice(off + h * w, off + (h + 1) * w) for off, w in (
        (0, RET_DK), (N_HEADS * RET_DK, RET_DK), (2 * N_HEADS * RET_DK, RET_DV),
        (2 * N_HEADS * RET_DK + N_HEADS * RET_DV, RET_DV))]


def _rotary(x, cos, sin):
    half = x.shape[1] // 2
    x1, x2 = x[:, :half], x[:, half:]
    return jnp.concatenate([x1 * cos - x2 * sin, x1 * sin + x2 * cos], axis=1)


def _odd_layer_body(xa_ref, xc_ref, cos_ref, sin_ref, nw2_ref, nw3_ref, win_ref, rnw_ref, wout_ref,
                    ps_ref, cos_s_ref, sin_s_ref, s_in, o_ref, s_out, ys_ref, s_new,
                    p_buf, hn_scr, y_scr, dec_scr, *, nc, NS, TS):
    C = RET_CHUNK
    j = pl.program_id(0)
    cb = jnp.maximum(j - 1, 0) % nc

    @pl.when(j == 0)
    def _():
        p_buf[...] = jnp.zeros(p_buf.shape, f32)
        ri = lax.broadcasted_iota(jnp.int32, (C, C), 0)
        ci = lax.broadcasted_iota(jnp.int32, (C, C), 1)
        dist = jnp.maximum(ri - ci, 0).astype(f32)
        for h in range(N_HEADS):
            dec_scr[h] = jnp.where(ci <= ri, jnp.exp2(dist * math.log2(1.0 - 2.0 ** (-5.0 - h))), 0.0)

    @pl.when(cb == 0)
    def _():
        s_out[...] = jnp.zeros(s_out.shape, f32)

    hn_scr[...] = _rms(xa_ref[0], nw2_ref[...]).astype(bf16)
    cos, sin = cos_ref[...], sin_ref[...]
    t_k = lax.broadcasted_iota(jnp.int32, (C, RET_DK), 0).astype(f32)

    RS = NS * TS
    cos_s = jnp.concatenate([cos_s_ref[...]] * NS, axis=0)
    sin_s = jnp.concatenate([sin_s_ref[...]] * NS, axis=0)
    ri = lax.broadcasted_iota(jnp.int32, (RS, RS), 0)
    ci = lax.broadcasted_iota(jnp.int32, (RS, RS), 1)
    visible = ((ri // TS) == (ci // TS)) & (ci <= ri)
    dist_s = jnp.where(visible, ri - ci, 0).astype(f32)
    t_s = (lax.broadcasted_iota(jnp.int32, (RS, RET_DK), 0) % TS).astype(f32)

    lg = [math.log2(1.0 - 2.0 ** (-5.0 - h)) for h in range(N_HEADS)]
    colsl = [_ret_cols(h) for h in range(N_HEADS)]
    for h in range(N_HEADS):
        log2_gamma = lg[h]
        cols = colsl[h]
        q = _rotary(ps_ref[:, cols[0]], cos_s, sin_s)
        k = _rotary(ps_ref[:, cols[1]], cos_s, sin_s) * (RET_DK ** -0.5)
        vb = ps_ref[:, cols[2]].astype(bf16)
        gate = ps_ref[:, cols[3]]
        decay = jnp.where(visible, jnp.exp2(dist_s * log2_gamma), 0.0)
        s = lax.dot_general(q.astype(bf16), k.astype(bf16), _NT, preferred_element_type=f32) * decay
        o = jnp.dot(s.astype(bf16), vb, preferred_element_type=f32)
        q_dec = (q * jnp.exp2((t_s + 1.0) * log2_gamma)).astype(bf16)
        k_dec = (k * jnp.exp2((TS - 1.0 - t_s) * log2_gamma)).astype(bf16)
        parts = []
        for n in range(NS):
            rows = slice(n * TS, (n + 1) * TS)
            st = s_in[n, h]
            parts.append(jnp.dot(q_dec[rows], st.astype(bf16), preferred_element_type=f32))
            upd = lax.dot_general(k_dec[rows], vb[rows], _TN, preferred_element_type=f32)
            s_new[n, h] = st * (2.0 ** (TS * log2_gamma)) + upd
        o = o + jnp.concatenate(parts, axis=0)
        y = _rms(o, rnw_ref[...]) * (gate * _sigmoid(gate))
        ys_ref[:, h * RET_DV:(h + 1) * RET_DV] = y.astype(bf16)

    for h in range(N_HEADS):
        log2_gamma = lg[h]
        cols = colsl[h]
        q = _rotary(p_buf[:, cols[0]], cos, sin)
        k = _rotary(p_buf[:, cols[1]], cos, sin) * (RET_DK ** -0.5)
        vb = p_buf[:, cols[2]].astype(bf16)
        gate = p_buf[:, cols[3]]
        s = lax.dot_general(q.astype(bf16), k.astype(bf16), _NT, preferred_element_type=f32) * dec_scr[h]
        o = jnp.dot(s.astype(bf16), vb, preferred_element_type=f32)
        st = s_out[0, h]
        q_dec = (q * jnp.exp2((t_k + 1.0) * log2_gamma)).astype(bf16)
        o = o + jnp.dot(q_dec, st.astype(bf16), preferred_element_type=f32)
        k_dec = (k * jnp.exp2((C - 1.0 - t_k) * log2_gamma)).astype(bf16)
        upd = lax.dot_general(k_dec, vb, _TN, preferred_element_type=f32)
        s_out[0, h] = st * (2.0 ** (C * log2_gamma)) + upd
        y = _rms(o, rnw_ref[...]) * (gate * _sigmoid(gate))
        y_scr[:, h * RET_DV:(h + 1) * RET_DV] = y.astype(bf16)
        for cs in cols:
            p_buf[:, cs] = jnp.dot(hn_scr[...], win_ref[:, cs], preferred_element_type=f32)

    m = jnp.dot(y_scr[...], wout_ref[...], preferred_element_type=f32)
    o_ref[0] = xc_ref[0] + _rms(m, nw3_ref[...])


def _odd_layer(x3, ps, s_ret, nw2, nw3, w_in, rnw, w_out):
    n_seq, seq_len, _ = x3.shape
    n_s = s_ret.shape[0]
    TS = ps.shape[0] // n_s
    C = RET_CHUNK
    nc = seq_len // C
    n_tiles = n_seq * nc
    NS = n_s // n_tiles
    assert NS * n_tiles == n_s and (NS * TS) % 16 == 0
    cos, sin = _rope_tables(seq_len, 0)
    cos_s, sin_s = _rope_tables(TS, PAST_LEN)
    x_blk = (1, C, D_MODEL)

    def tile(lag):
        def index(j):
            t = jnp.clip(j - lag, 0, n_tiles - 1)
            return (t // nc, t % nc, 0)
        return index

    short = lambda j: jnp.minimum(j, n_tiles - 1)
    rope_spec = pl.BlockSpec((C, RET_DK // 2), lambda j: (jnp.maximum(j - 1, 0) % nc, 0))
    state_s_spec = pl.BlockSpec((NS, N_HEADS, RET_DK, RET_DV), lambda j: (short(j), 0, 0, 0))
    return pl.pallas_call(
        functools.partial(_odd_layer_body, nc=nc, NS=NS, TS=TS),
        out_shape=(
            jax.ShapeDtypeStruct(x3.shape, f32),
            jax.ShapeDtypeStruct((n_seq, N_HEADS, RET_DK, RET_DV), f32),
            jax.ShapeDtypeStruct((n_s * TS, N_HEADS * RET_DV), bf16),
            jax.ShapeDtypeStruct(s_ret.shape, f32),
        ),
        grid=(n_tiles + 1,),
        in_specs=[
            pl.BlockSpec(x_blk, tile(0)),
            pl.BlockSpec(x_blk, tile(1)),
            rope_spec,
            rope_spec,
            _const_spec((1, D_MODEL)),
            _const_spec((1, D_MODEL)),
            _const_spec(w_in.shape),
            _const_spec((1, RET_DV)),
            _const_spec(w_out.shape),
            pl.BlockSpec((NS * TS, ODD_P), lambda j: (short(j), 0)),
            _const_spec((TS, RET_DK // 2)),
            _const_spec((TS, RET_DK // 2)),
            state_s_spec,
        ],
        out_specs=(
            pl.BlockSpec(x_blk, tile(1)),
            pl.BlockSpec((1, N_HEADS, RET_DK, RET_DV), lambda j: (jnp.maximum(j - 1, 0) // nc, 0, 0, 0)),
            pl.BlockSpec((NS * TS, N_HEADS * RET_DV), lambda j: (short(j), 0)),
            state_s_spec,
        ),
        scratch_shapes=[
            pltpu.VMEM((C, ODD_P), f32),
            pltpu.VMEM((C, D_MODEL), bf16),
            pltpu.VMEM((C, N_HEADS * RET_DV), bf16),
            pltpu.VMEM((N_HEADS, C, C), f32),
        ],
        compiler_params=_params("arbitrary"),
        name="odd_layer",
    )(x3, x3, cos, sin, nw2, nw3, w_in, rnw, w_out, ps, cos_s, sin_s, s_ret)


def _pack_even_w_in(w):
    lr0 = 2 * N_HEADS * GLA_DK + 2 * N_HEADS * EVEN_DV
    lr = jnp.pad(w[:, lr0:lr0 + GLA_LOWRANK], ((0, 0), (0, LANES - GLA_LOWRANK)))
    return jnp.concatenate([w[:, :lr0], w[:, lr0 + GLA_LOWRANK:], lr], axis=1)


def _mixer_even_prompt(x, l, w, row, n_seq, seq_len):
    e = l // 2
    x3, sg, sh = _even_layer_prompt(x.reshape(n_seq, seq_len, D_MODEL), row(2), row(3), w["even_in"][e], w["gla_w2"][e],
                                    w["gla_bg"][e], w["lb_table"], w["gla_nw"][e], w["hgrn_nw"][e], w["even_out"][e],
                                    layer=l)
    return x3.reshape(n_seq * seq_len, D_MODEL), sg, sh


def _mixer_even_sample(x, l, w, row, s_gla, s_hgrn, n_seq, seq_len):
    e = l // 2
    p = _proj(x, row(2), w["even_in"][e], tm=512, tn=EVEN_P)
    y, sg, sh = _even_attn(p, w["gla_w2"][e], w["gla_bg"][e], w["lb_table"], w["gla_nw"][e], w["hgrn_nw"][e],
                           s_gla[e], s_hgrn[e], n_seq=n_seq, seq_len=seq_len, C=seq_len, NB=16, layer=l)
    return _out_block(y, x, w["even_out"][e], row(3)), sg, sh


def _mixer_odd(xp, xs, l, w, row, s_ret, n_seq, seq_len):
    o = l // 2
    ps = _proj(xs, row(2), w["odd_in"][o], tm=256, tn=12 * LANES)
    x3, sr_p, ys, sr_s = _odd_layer(xp.reshape(n_seq, seq_len, D_MODEL), ps, s_ret[o], row(2), row(3), w["odd_in"][o],
                                    w["ret_nw"][o], w["odd_out"][o])
    return x3.reshape(n_seq * seq_len, D_MODEL), _out_block(ys, xs, w["odd_out"][o], row(3)), sr_p, sr_s


def kernel(x_prompt, x_sample, state_gla, state_hgrn, state_ret, norm_w, ffn_w_gate, ffn_w_up, ffn_w_down, even_w_in, gla_w_gate2, gla_b_gate, gla_norm_w, hgrn_lb_table, hgrn_norm_w, even_w_out, odd_w_in, ret_norm_w, odd_w_out):
    bp, tp, _ = x_prompt.shape
    bs, ts, _ = x_sample.shape
    n_even = even_w_in.shape[0]
    nw = norm_w.astype(f32)
    ffn_w = [w_.astype(f32) for w_ in (ffn_w_gate, ffn_w_up, ffn_w_down)]
    w = {
        "even_in": jnp.stack([_pack_even_w_in(even_w_in[e]) for e in range(n_even)]).astype(bf16),
        "gla_w2": jnp.pad(gla_w_gate2, ((0, 0), (0, LANES - GLA_LOWRANK), (0, 0))).astype(bf16),
        "gla_bg": gla_b_gate[:, None, :].astype(f32),
        "lb_table": hgrn_lb_table.astype(f32),
        "gla_nw": gla_norm_w[:, None, :].astype(f32),
        "hgrn_nw": hgrn_norm_w[:, None, :].astype(f32),
        "even_out": even_w_out.astype(bf16),
        "odd_in": odd_w_in.astype(bf16),
        "ret_nw": ret_norm_w[:, None, :].astype(f32),
        "odd_out": odd_w_out.astype(bf16),
    }
    xp = x_prompt.reshape(bp * tp, D_MODEL)
    xs = x_sample.reshape(bs * ts, D_MODEL)
    new_p, new_s = ([], [], []), ([], [], [])
    for l in range(nw.shape[0]):
        row = lambda i, l=l: nw[l, i][None, :]
        xp, xs = _ffn(xp, xs, row(0), row(1), *ffn_w, l, 0)
        if l % 2 == 0:
            xp, sg, sh = _mixer_even_prompt(xp, l, w, row, bp, tp)
            new_p[0].append(sg)
            new_p[1].append(sh)
            xs, sg, sh = _mixer_even_sample(xs, l, w, row, state_gla, state_hgrn, bs, ts)
            new_s[0].append(sg)
            new_s[1].append(sh)
        else:
            xp, xs, sr_p, sr_s = _mixer_odd(xp, xs, l, w, row, state_ret, bp, tp)
            new_p[2].append(sr_p)
            new_s[2].append(sr_s)
        xp, xs = _ffn(xp, xs, row(4), row(5), *ffn_w, l, 1)
    gla_p, hgrn_p, ret_p = (jnp.stack(v) for v in new_p)
    gla_s, hgrn_s, ret_s = (jnp.stack(v) for v in new_s)
    return (xp.reshape(bp, tp, D_MODEL), xs.reshape(bs, ts, D_MODEL), gla_p, hgrn_p, ret_p, gla_s, hgrn_s, ret_s)
```

```python
import functools
import math

import jax
import jax.numpy as jnp
import numpy as np
from jax import lax
from jax.experimental import pallas as pl
from jax.experimental.pallas import tpu as pltpu

f32 = jnp.float32
bf16 = jnp.bfloat16

D_MODEL = 1024
D_FF = 2816
EPS = 1e-6
PAST_LEN = 16384
ROPE_BASE = 10000.0
GLA_TAU = 16.0
N_HEADS = 4
GLA_DK = 64
EVEN_DK = 128
EVEN_DV = 128
GLA_LOWRANK = 16
RET_DK = 256
RET_DV = 512
LANES = 128
SUBLANES = 8
T_QA, T_KA, T_VA, T_RA, T_QB, T_FB, T_IB, T_GB, T_LR = 0, 2, 4, 8, 12, 16, 20, 24, 28
EVEN_P = 29 * LANES
ODD_P = 2 * N_HEADS * RET_DK + 2 * N_HEADS * RET_DV
VMEM_LIMIT = 58 * 1024 * 1024

_NT = (((1,), (1,)), ((), ()))
_TN = (((0,), (0,)), ((), ()))


def _rms(x, w):
    return x * lax.rsqrt(jnp.mean(x * x, axis=-1, keepdims=True) + EPS) * w


def _sigmoid(x):
    return 1.0 / (1.0 + jnp.exp(-x))


def _const_spec(shape):
    return pl.BlockSpec(shape, lambda *_: (0,) * len(shape), pipeline_mode=pl.Buffered(1))


def _params(*sem):
    return pltpu.CompilerParams(dimension_semantics=sem, vmem_limit_bytes=VMEM_LIMIT)


FFN_TF = 256
FFN_CHUNKS = D_FF // FFN_TF


def _ffn_tile(x_ref, nwa_ref, nwb_ref, wg_b, wu_b, wd_b, o_ref, acc_ref):
    x = x_ref[...]
    hn = _rms(x, nwa_ref[...]).astype(bf16)
    for j in range(FFN_CHUNKS):
        g = jnp.dot(hn, wg_b[j], preferred_element_type=f32)
        u = jnp.dot(hn, wu_b[j], preferred_element_type=f32)
        a = (g * _sigmoid(g) * u).astype(bf16)
        part = jnp.dot(a, wd_b[j], preferred_element_type=f32)
        if j == 0:
            acc_ref[...] = part
        else:
            acc_ref[...] += part
    o_ref[...] = x + 0.5 * _rms(acc_ref[...], nwb_ref[...])


def _ffn_body(xa_ref, xb_ref, nwa_ref, nwb_ref, wg_ref, wu_ref, wd_ref, oa_ref, ob_ref, wg_b, wu_b, wd_b, acc_ref,
              hn_scr, *, na):
    i = pl.program_id(0)

    @pl.when(i == 0)
    def _():
        hn_scr[...] = _rms(xa_ref[...], nwa_ref[...]).astype(bf16)

    @pl.when(i < FFN_CHUNKS)
    def _():
        wg_b[i] = wg_ref[...].astype(bf16)
        wu_b[i] = wu_ref[...].astype(bf16)
        wd_b[i] = wd_ref[...].astype(bf16)
        hn = hn_scr[...]
        g = jnp.dot(hn, wg_b[i], preferred_element_type=f32)
        u = jnp.dot(hn, wu_b[i], preferred_element_type=f32)
        a = (g * _sigmoid(g) * u).astype(bf16)
        part = jnp.dot(a, wd_b[i], preferred_element_type=f32)

        @pl.when(i == 0)
        def _():
            acc_ref[...] = part

        @pl.when(i > 0)
        def _():
            acc_ref[...] += part

    @pl.when(i == FFN_CHUNKS - 1)
    def _():
        oa_ref[...] = xa_ref[...] + 0.5 * _rms(acc_ref[...], nwb_ref[...])

    @pl.when((i >= FFN_CHUNKS) & (i < FFN_CHUNKS + na - 1))
    def _():
        _ffn_tile(xa_ref, nwa_ref, nwb_ref, wg_b, wu_b, wd_b, oa_ref, acc_ref)

    @pl.when(i >= FFN_CHUNKS + na - 1)
    def _():
        _ffn_tile(xb_ref, nwa_ref, nwb_ref, wg_b, wu_b, wd_b, ob_ref, acc_ref)


def _ffn(xa, xb, nwa, nwb, wg, wu, wd, l, pos, tm=512):
    na, nb = xa.shape[0] // tm, xb.shape[0] // tm
    chunk = lambda i: jnp.minimum(i, FFN_CHUNKS - 1)
    spec_a = pl.BlockSpec((tm, D_MODEL), lambda i: (jnp.clip(i - FFN_CHUNKS + 1, 0, na - 1), 0))
    spec_b = pl.BlockSpec((tm, D_MODEL), lambda i: (jnp.maximum(i - FFN_CHUNKS - na + 1, 0), 0))
    col_spec = pl.BlockSpec((None, None, D_MODEL, FFN_TF), lambda i: (l, pos, 0, chunk(i)))
    row_spec = pl.BlockSpec((None, None, FFN_TF, D_MODEL), lambda i: (l, pos, chunk(i), 0))
    return pl.pallas_call(
        functools.partial(_ffn_body, na=na),
        out_shape=(jax.ShapeDtypeStruct(xa.shape, f32), jax.ShapeDtypeStruct(xb.shape, f32)),
        grid=(FFN_CHUNKS + na - 1 + nb,),
        in_specs=[
            spec_a,
            spec_b,
            _const_spec((1, D_MODEL)),
            _const_spec((1, D_MODEL)),
            col_spec,
            col_spec,
            row_spec,
        ],
        out_specs=(spec_a, spec_b),
        scratch_shapes=[
            pltpu.VMEM((FFN_CHUNKS, D_MODEL, FFN_TF), bf16),
            pltpu.VMEM((FFN_CHUNKS, D_MODEL, FFN_TF), bf16),
            pltpu.VMEM((FFN_CHUNKS, FFN_TF, D_MODEL), bf16),
            pltpu.VMEM((tm, D_MODEL), f32),
            pltpu.VMEM((tm, D_MODEL), bf16),
        ],
        compiler_params=_params("arbitrary"),
        name="ffn",
    )(xa, xb, nwa, nwb, wg, wu, wd)


def _proj_body(x_ref, nw_ref, w_ref, o_ref, *, tn):
    hn = _rms(x_ref[...], nw_ref[...]).astype(bf16)
    for j in range(w_ref.shape[1] // tn):
        cols = slice(j * tn, (j + 1) * tn)
        o_ref[:, cols] = jnp.dot(hn, w_ref[:, cols], preferred_element_type=f32)


def _proj(x, nw, w, tm, tn):
    n, p = x.shape[0], w.shape[1]
    return pl.pallas_call(
        functools.partial(_proj_body, tn=tn),
        out_shape=jax.ShapeDtypeStruct((n, p), f32),
        grid=(n // tm,),
        in_specs=[
            pl.BlockSpec((tm, D_MODEL), lambda i: (i, 0)),
            _const_spec((1, D_MODEL)),
            _const_spec((D_MODEL, p)),
        ],
        out_specs=pl.BlockSpec((tm, p), lambda i: (i, 0)),
        compiler_params=_params("arbitrary"),
        name="proj",
    )(x, nw, w)


def _out_body(y_ref, x_ref, w_ref, nw_ref, o_ref):
    m = jnp.dot(y_ref[...], w_ref[...], preferred_element_type=f32)
    o_ref[...] = x_ref[...] + _rms(m, nw_ref[...])


def _out_block(y, x, w, nw, tm=512):
    n, dy = y.shape
    return pl.pallas_call(
        _out_body,
        out_shape=jax.ShapeDtypeStruct((n, D_MODEL), f32),
        grid=(n // tm,),
        in_specs=[
            pl.BlockSpec((tm, dy), lambda i: (i, 0)),
            pl.BlockSpec((tm, D_MODEL), lambda i: (i, 0)),
            _const_spec((dy, D_MODEL)),
            _const_spec((1, D_MODEL)),
        ],
        out_specs=pl.BlockSpec((tm, D_MODEL), lambda i: (i, 0)),
        compiler_params=_params("arbitrary"),
        name="mixer_out",
    )(y, x, w, nw)


def _split3(g):
    hi = g.astype(bf16)
    r1 = g - hi.astype(f32)
    mid = r1.astype(bf16)
    lo = (r1 - mid.astype(f32)).astype(bf16)
    return jnp.concatenate([hi, mid, lo], axis=1)


def _roll_in_groups(x, d):
    r = x.shape[0]
    return pltpu.roll(x.reshape(r // SUBLANES, SUBLANES, LANES), d, 1).reshape(r, LANES)


def _near_diagonal(q, k, b, v, ones):
    r = q.shape[0]
    sub = lax.broadcasted_iota(jnp.int32, (r, LANES), 0) % SUBLANES
    prods = [q * k]
    for d in range(1, SUBLANES):
        decay = jnp.exp(jnp.minimum(b - _roll_in_groups(b, d), 0.0))
        prods.append(q * _roll_in_groups(k, d) * decay)
    rs = jnp.dot(jnp.concatenate(prods, axis=0).astype(bf16), ones, preferred_element_type=f32)
    o = rs[:r] * v
    for d in range(1, SUBLANES):
        o = o + jnp.where(sub >= d, rs[d * r:(d + 1) * r], 0.0) * _roll_in_groups(v, d)
    return o


def _even_attn_body(p_ref, w2_ref, bg_ref, lbt_ref, gnw_ref, hnw_ref, sg_in, sh_in, y_ref, sg_out, sh_out, *, C, NB, layer):
    R = NB * C
    ri = lax.broadcasted_iota(jnp.int32, (R, R), 0)
    ci = lax.broadcasted_iota(jnp.int32, (R, R), 1)
    sum_mat = jnp.where(((ri // C) == (ci // C)) & (ci <= ri), 1.0, 0.0).astype(bf16)
    ones = jnp.ones((LANES, LANES), bf16)
    ones_c = jnp.ones((C, LANES), bf16)
    zeros_half = jnp.zeros((GLA_DK, EVEN_DV), f32)

    lr = p_ref[:, T_LR * LANES:(T_LR + 1) * LANES].astype(bf16)
    xg = jnp.dot(lr, w2_ref[...], preferred_element_type=f32) + bg_ref[...]
    log_alpha = (jnp.minimum(xg, 0.0) - jnp.log1p(jnp.exp(-jnp.abs(xg)))) * (1.0 / GLA_TAU)
    low_half = lax.broadcasted_iota(jnp.int32, (R, LANES), 1) < GLA_DK

    lbt = lbt_ref[...]
    lbe = jnp.exp(lbt - jnp.max(lbt, axis=0, keepdims=True))
    lb_all = jnp.sum(lbe[:layer + 1], axis=0, keepdims=True) / jnp.sum(lbe, axis=0, keepdims=True)

    def col(tile):
        return p_ref[:, tile * LANES:(tile + 1) * LANES]

    for h in range(2 * N_HEADS):
        hh = h % N_HEADS
        if h < N_HEADS:
            mine = low_half if h % 2 == 0 else jnp.logical_not(low_half)
            q = jnp.where(mine, col(T_QA + h // 2), 0.0) * (GLA_DK ** -0.5)
            k = jnp.where(mine, col(T_KA + h // 2), 0.0)
            v = col(T_VA + h)
            gate = col(T_RA + h)
            g = log_alpha[:, (h // 2) * LANES:(h // 2 + 1) * LANES]
            nw = gnw_ref[...]
        else:
            qb = col(T_QB + hh)
            q = qb * _sigmoid(qb)
            lb = lb_all[:, hh * LANES:(hh + 1) * LANES]
            f = lb + (1.0 - lb) * _sigmoid(col(T_FB + hh))
            k = 1.0 - f
            g = jnp.log(f)
            v = col(T_IB + hh)
            gate = col(T_GB + hh)
            nw = hnw_ref[...]

        g3 = _split3(g)
        sums = jnp.dot(sum_mat, g3, preferred_element_type=f32)
        b = sums[:, :LANES] + sums[:, LANES:2 * LANES] + sums[:, 2 * LANES:]
        o = _near_diagonal(q, k, b, v, ones)
        q_dec = (q * jnp.exp(b)).astype(bf16)
        vb = v.astype(bf16)
        parts = []
        for n in range(NB):
            rows = slice(n * C, (n + 1) * C)
            b_last = b[(n + 1) * C - 1:(n + 1) * C, :]
            k_dec = (k[rows] * jnp.exp(b_last - b[rows])).astype(bf16)
            tot = lax.dot_general(g3[rows], ones_c, _TN, preferred_element_type=f32)
            decay = jnp.exp(tot[:LANES] + tot[LANES:2 * LANES] + tot[2 * LANES:])
            upd = lax.dot_general(k_dec, vb[rows], _TN, preferred_element_type=f32)
            if h < N_HEADS:
                lo, hi = (h % 2) * GLA_DK, (h % 2 + 1) * GLA_DK
                st = sg_in[n, h]
                st_pad = jnp.concatenate([st, zeros_half] if h % 2 == 0 else [zeros_half, st], axis=0)
                sg_out[n, h] = st * decay[lo:hi] + upd[lo:hi]
            else:
                st_pad = sh_in[n, hh]
                sh_out[n, hh] = st_pad * decay + upd
            parts.append(jnp.dot(q_dec[rows], st_pad.astype(bf16), preferred_element_type=f32))
        o = o + (parts[0] if NB == 1 else jnp.concatenate(parts, axis=0))

        y = _rms(o, nw) * (gate * _sigmoid(gate))
        y_ref[:, h * LANES:(h + 1) * LANES] = y.astype(bf16)


def _even_attn(p, w2, bg, lbt, gnw, hnw, sg, sh, *, n_seq, seq_len, C, NB, layer):
    assert C == SUBLANES, "pairs further apart than one 8-row group are not handled here"
    R = NB * C
    nc = seq_len // C
    assert nc == 1, "one chunk per sequence"
    grid = (n_seq // NB, nc)
    row_map = lambda i, c: (i * nc + c, 0)
    in_specs = [
        pl.BlockSpec((R, EVEN_P), row_map),
        _const_spec(w2.shape),
        _const_spec(bg.shape),
        _const_spec(lbt.shape),
        _const_spec((1, EVEN_DV)),
        _const_spec((1, EVEN_DV)),
    ]
    sg_spec = pl.BlockSpec((NB, N_HEADS, GLA_DK, EVEN_DV), lambda i, c: (i, 0, 0, 0))
    sh_spec = pl.BlockSpec((NB, N_HEADS, EVEN_DK, EVEN_DV), lambda i, c: (i, 0, 0, 0))
    in_specs += [sg_spec, sh_spec]
    return pl.pallas_call(
        functools.partial(_even_attn_body, C=C, NB=NB, layer=layer),
        out_shape=(
            jax.ShapeDtypeStruct((n_seq * seq_len, 2 * N_HEADS * EVEN_DV), bf16),
            jax.ShapeDtypeStruct((n_seq, N_HEADS, GLA_DK, EVEN_DV), f32),
            jax.ShapeDtypeStruct((n_seq, N_HEADS, EVEN_DK, EVEN_DV), f32),
        ),
        grid=grid,
        in_specs=in_specs,
        out_specs=(pl.BlockSpec((R, 2 * N_HEADS * EVEN_DV), row_map), sg_spec, sh_spec),
        compiler_params=_params("arbitrary", "arbitrary"),
        name="even_attn",
    )(p, w2, bg, lbt, gnw, hnw, sg, sh)


CHUNK = 64
N_LEVELS = 6
N_SUM_LEVELS = 3
LOG2E = 1.4426950408889634
EVEN_PW = (T_LR + 2) * LANES


def _chunk_tables():
    t = np.arange(CHUNK)[:, None]
    j = np.arange(CHUNK)[None, :]
    mats = []
    for l in range(N_SUM_LEVELS):
        hs = 1 << l
        m = (t // (2 * hs)) * (2 * hs) + hs
        mats.append(np.where(t >= m, (j > m) & (j <= t), (j > t) & (j <= m)))
    mats.append(j <= t)
    a = np.concatenate(mats, axis=0).astype(np.float32)
    x = t ^ j
    lvl = np.where(j > t, N_LEVELS + 1, np.where(j == t, 0, np.floor(np.log2(np.maximum(x, 1))).astype(np.int64) + 1))
    return np.concatenate([a, a], axis=1), lvl.astype(np.int32)


def _log_decay_sums(g2_list, sum_mat):
    hi = [g.astype(bf16) for g in g2_list]
    lo = [(g - h.astype(f32)).astype(bf16) for g, h in zip(g2_list, hi)]
    rhs = jnp.concatenate([jnp.concatenate(hi, axis=1), jnp.concatenate(lo, axis=1)], axis=0)
    z = jnp.dot(sum_mat, rhs, preferred_element_type=f32)
    return [z[:, i * LANES:(i + 1) * LANES] for i in range(len(g2_list))]


def _level_log_decay(z, b, l):
    if l < N_SUM_LEVELS:
        return z[l * CHUNK:(l + 1) * CHUNK]
    hs = 1 << l
    ref = jnp.concatenate([jnp.broadcast_to(b[m:m + 1], (2 * hs, LANES)) for m in range(hs, CHUNK, 2 * hs)], axis=0)
    d = b - ref
    return jnp.minimum(d, -d)


def _decay_units(qs, ks, vs, sts, zs, level, fillers):
    n = len(qs)
    fillers = list(fillers)
    vbs = [v.astype(bf16) for v in vs]
    bs = [z[N_SUM_LEVELS * CHUNK:] for z in zs]
    scores = [0.0] * n
    for l in range(N_LEVELS):
        es = [jnp.exp2(_level_log_decay(zs[i], bs[i], l)) for i in range(n)]
        ss = [lax.dot_general((qs[i] * es[i]).astype(bf16), (ks[i] * es[i]).astype(bf16), _NT,
                              preferred_element_type=f32) for i in range(n)]
        scores = [jnp.where(level == l + 1, ss[i], scores[i]) for i in range(n)]
        if fillers:
            fillers.pop(0)()
    b_lasts = [b[CHUNK - 1:] for b in bs]
    o_in = [jnp.dot(scores[i].astype(bf16), vbs[i], preferred_element_type=f32) for i in range(n)]
    o_st = [lax.dot_general((qs[i] * jnp.exp2(bs[i])).astype(bf16), sts[i].astype(bf16), _NT,
                            preferred_element_type=f32) for i in range(n)]
    upd = [lax.dot_general(vbs[i], (ks[i] * jnp.exp2(b_lasts[i] - bs[i])).astype(bf16), _TN,
                           preferred_element_type=f32) for i in range(n)]
    while fillers:
        fillers.pop(0)()
    outs = [jnp.sum(qs[i] * ks[i], axis=-1, keepdims=True) * vs[i] + o_in[i] + o_st[i] for i in range(n)]
    return outs, [sts[i] * jnp.exp2(b_lasts[i]) + upd[i] for i in range(n)]


def _even_layer_body(xa_ref, xc_ref, nw2_ref, nw3_ref, win_ref, w2_ref, bg_ref, lbt_ref, gnw_ref, hnw_ref, wout_ref,
                     smat_ref, lvl_ref, o_ref, sg_out, sh_out, p_buf, hn_scr, y_scr, st_ref, *, NB, nc, layer):
    R = NB * CHUNK
    j = pl.program_id(0)
    cb = jnp.maximum(j - 1, 0) % nc
    xg0 = T_LR * LANES

    @pl.when(j == 0)
    def _():
        p_buf[...] = jnp.zeros(p_buf.shape, f32)

    @pl.when(cb == 0)
    def _():
        st_ref[...] = jnp.zeros(st_ref.shape, f32)

    hn_scr[...] = _rms(xa_ref[...].reshape(R, D_MODEL), nw2_ref[...]).astype(bf16)

    lbt = lbt_ref[...]
    lbe = jnp.exp(lbt - jnp.max(lbt, axis=0, keepdims=True))
    lb_all = jnp.sum(lbe[:layer + 1], axis=0, keepdims=True) / jnp.sum(lbe, axis=0, keepdims=True)
    sum_mat = smat_ref[...]
    level = lvl_ref[...]

    low_half = lax.broadcasted_iota(jnp.int32, (CHUNK, LANES), 1) < GLA_DK

    def proj_piece(tile):
        def run():
            cols = slice(tile * LANES, (tile + 2) * LANES)
            p_buf[:, cols] = jnp.dot(hn_scr[...], win_ref[:, cols], preferred_element_type=f32)
        return run

    def gate_piece():
        lr = jnp.dot(hn_scr[...], win_ref[:, xg0:], preferred_element_type=f32).astype(bf16)
        p_buf[:, xg0:] = jnp.dot(lr, w2_ref[...], preferred_element_type=f32) + bg_ref[...]

    def tile(rows, t):
        return p_buf[rows, t * LANES:(t + 1) * LANES]

    for h0 in range(0, 2 * N_HEADS, 2):
        nw = gnw_ref[...] if h0 < N_HEADS else hnw_ref[...]
        qs, ks, g2s, vs, gates, ids = [], [], [], [], [], []
        for h in (h0, h0 + 1):
            hh = h % N_HEADS
            for n in range(NB):
                rows = slice(n * CHUNK, (n + 1) * CHUNK)
                if h < N_HEADS:
                    mine = low_half if h % 2 == 0 else jnp.logical_not(low_half)
                    xg = tile(rows, T_LR + h // 2)
                    qs.append(jnp.where(mine, tile(rows, T_QA + h // 2), 0.0) * (GLA_DK ** -0.5))
                    ks.append(jnp.where(mine, tile(rows, T_KA + h // 2), 0.0))
                    g2s.append((jnp.minimum(xg, 0.0) - jnp.log1p(jnp.exp(-jnp.abs(xg)))) * (LOG2E / GLA_TAU))
                    vs.append(tile(rows, T_VA + h))
                    gates.append(tile(rows, T_RA + h))
                else:
                    lb = lb_all[:, hh * LANES:(hh + 1) * LANES]
                    f = lb + (1.0 - lb) * _sigmoid(tile(rows, T_FB + hh))
                    q = tile(rows, T_QB + hh)
                    qs.append(q * _sigmoid(q))
                    ks.append(1.0 - f)
                    g2s.append(jnp.log2(f))
                    vs.append(tile(rows, T_IB + hh))
                    gates.append(tile(rows, T_GB + hh))
                ids.append((n, h))
        zs = []
        for i in range(0, len(g2s), 2):
            zs += _log_decay_sums(g2s[i:i + 2], sum_mat)
        if h0 == 0:
            fillers = [proj_piece(T_VA), proj_piece(T_RA)]
        elif h0 == 2:
            fillers = [proj_piece(T_VA + 2), proj_piece(T_RA + 2), proj_piece(T_QA), proj_piece(T_KA), gate_piece]
        else:
            fillers = [proj_piece(t + h0 - N_HEADS) for t in (T_QB, T_FB, T_IB, T_GB)]
        outs, sts = _decay_units(qs, ks, vs, [st_ref[n, h] for n, h in ids], zs, level, fillers)
        for i, (n, h) in enumerate(ids):
            rows = slice(n * CHUNK, (n + 1) * CHUNK)
            st_ref[n, h] = sts[i]
            y = _rms(outs[i], nw) * (gates[i] * _sigmoid(gates[i]))
            y_scr[rows, h * LANES:(h + 1) * LANES] = y.astype(bf16)

    m = jnp.dot(y_scr[...], wout_ref[...], preferred_element_type=f32)
    o_ref[...] = (xc_ref[...].reshape(R, D_MODEL) + _rms(m, nw3_ref[...])).reshape(NB, CHUNK, D_MODEL)

    @pl.when((cb == nc - 1) & (j >= 1))
    def _():
        for n in range(NB):
            for h in range(N_HEADS):
                sg_out[n, h] = st_ref[n, h].T[(h % 2) * GLA_DK:(h % 2 + 1) * GLA_DK, :]
                sh_out[n, h] = st_ref[n, N_HEADS + h].T


def _even_layer_prompt(x3, nw2, nw3, w_in, w2, bg, lbt, gnw, hnw, w_out, *, layer, NB=4):
    n_seq, seq_len, _ = x3.shape
    nc = seq_len // CHUNK
    n_tiles = (n_seq // NB) * nc
    R = NB * CHUNK
    smat, lvl = _chunk_tables()
    x_blk = (NB, CHUNK, D_MODEL)

    def tile(lag):
        def index(j):
            t = jnp.clip(j - lag, 0, n_tiles - 1)
            return (t // nc, t % nc, 0)
        return index

    group = lambda j: (jnp.maximum(j - 1, 0) // nc, 0, 0, 0)
    return pl.pallas_call(
        functools.partial(_even_layer_body, NB=NB, nc=nc, layer=layer),
        out_shape=(
            jax.ShapeDtypeStruct(x3.shape, f32),
            jax.ShapeDtypeStruct((n_seq, N_HEADS, GLA_DK, EVEN_DV), f32),
            jax.ShapeDtypeStruct((n_seq, N_HEADS, EVEN_DK, EVEN_DV), f32),
        ),
        grid=(n_tiles + 1,),
        in_specs=[
            pl.BlockSpec(x_blk, tile(0)),
            pl.BlockSpec(x_blk, tile(1)),
            _const_spec((1, D_MODEL)),
            _const_spec((1, D_MODEL)),
            _const_spec(w_in.shape),
            _const_spec(w2.shape),
            _const_spec(bg.shape),
            _const_spec(lbt.shape),
            _const_spec((1, EVEN_DV)),
            _const_spec((1, EVEN_DV)),
            _const_spec(w_out.shape),
            _const_spec(smat.shape),
            _const_spec(lvl.shape),
        ],
        out_specs=(
            pl.BlockSpec(x_blk, tile(1)),
            pl.BlockSpec((NB, N_HEADS, GLA_DK, EVEN_DV), group),
            pl.BlockSpec((NB, N_HEADS, EVEN_DK, EVEN_DV), group),
        ),
        scratch_shapes=[
            pltpu.VMEM((R, EVEN_PW), f32),
            pltpu.VMEM((R, D_MODEL), bf16),
            pltpu.VMEM((R, 2 * N_HEADS * EVEN_DV), bf16),
            pltpu.VMEM((NB, 2 * N_HEADS, EVEN_DV, EVEN_DK), f32),
        ],
        compiler_params=_params("arbitrary"),
        name="even_layer",
    )(x3, x3, nw2, nw3, w_in, w2, bg, lbt, gnw, hnw, w_out, jnp.asarray(smat, bf16), jnp.asarray(lvl))


RET_CHUNK = 256


def _rope_body(cos_ref, sin_ref, *, pos0):
    n, half = cos_ref.shape
    pos = (pos0 + lax.broadcasted_iota(jnp.int32, (n, half), 0)).astype(f32)
    lane = lax.broadcasted_iota(jnp.int32, (n, half), 1).astype(f32)
    ang = pos * jnp.power(ROPE_BASE, -lane / half)
    cos_ref[...] = jnp.cos(ang)
    sin_ref[...] = jnp.sin(ang)


def _rope_tables(n, pos0):
    shape = jax.ShapeDtypeStruct((n, RET_DK // 2), f32)
    return pl.pallas_call(functools.partial(_rope_body, pos0=pos0), out_shape=(shape, shape), name="rope_tables")()


def _ret_cols(h):
    return [slice(off + h * w, off + (h + 1) * w) for off, w in (
        (0, RET_DK), (N_HEADS * RET_DK, RET_DK), (2 * N_HEADS * RET_DK, RET_DV),
        (2 * N_HEADS * RET_DK + N_HEADS * RET_DV, RET_DV))]


def _rotary(x, cos, sin):
    half = x.shape[1] // 2
    x1, x2 = x[:, :half], x[:, half:]
    return jnp.concatenate([x1 * cos - x2 * sin, x1 * sin + x2 * cos], axis=1)


def _odd_layer_body(xa_ref, xc_ref, cos_ref, sin_ref, nw2_ref, nw3_ref, win_ref, rnw_ref, wout_ref,
                    ps_ref, cos_s_ref, sin_s_ref, s_in, o_ref, s_out, ys_ref, s_new,
                    p_buf, hn_scr, y_scr, dec_scr, *, nc, NS, TS):
    C = RET_CHUNK
    j = pl.program_id(0)
    cb = jnp.maximum(j - 1, 0) % nc

    @pl.when(j == 0)
    def _():
        p_buf[...] = jnp.zeros(p_buf.shape, f32)
        ri = lax.broadcasted_iota(jnp.int32, (C, C), 0)
        ci = lax.broadcasted_iota(jnp.int32, (C, C), 1)
        dist = jnp.maximum(ri - ci, 0).astype(f32)
        for h in range(N_HEADS):
            dec_scr[h] = jnp.where(ci <= ri, jnp.exp2(dist * math.log2(1.0 - 2.0 ** (-5.0 - h))), 0.0)

    @pl.when(cb == 0)
    def _():
        s_out[...] = jnp.zeros(s_out.shape, f32)

    hn_scr[...] = _rms(xa_ref[0], nw2_ref[...]).astype(bf16)
    cos, sin = cos_ref[...], sin_ref[...]
    t_k = lax.broadcasted_iota(jnp.int32, (C, RET_DK), 0).astype(f32)

    RS = NS * TS
    cos_s = jnp.concatenate([cos_s_ref[...]] * NS, axis=0)
    sin_s = jnp.concatenate([sin_s_ref[...]] * NS, axis=0)
    ri = lax.broadcasted_iota(jnp.int32, (RS, RS), 0)
    ci = lax.broadcasted_iota(jnp.int32, (RS, RS), 1)
    visible = ((ri // TS) == (ci // TS)) & (ci <= ri)
    dist_s = jnp.where(visible, ri - ci, 0).astype(f32)
    t_s = (lax.broadcasted_iota(jnp.int32, (RS, RET_DK), 0) % TS).astype(f32)

    lg = [math.log2(1.0 - 2.0 ** (-5.0 - h)) for h in range(N_HEADS)]
    colsl = [_ret_cols(h) for h in range(N_HEADS)]
    for h in range(N_HEADS):
        log2_gamma = lg[h]
        cols = colsl[h]
        q = _rotary(ps_ref[:, cols[0]], cos_s, sin_s)
        k = _rotary(ps_ref[:, cols[1]], cos_s, sin_s) * (RET_DK ** -0.5)
        vb = ps_ref[:, cols[2]].astype(bf16)
        gate = ps_ref[:, cols[3]]
        decay = jnp.where(visible, jnp.exp2(dist_s * log2_gamma), 0.0)
        s = lax.dot_general(q.astype(bf16), k.astype(bf16), _NT, preferred_element_type=f32) * decay
        o = jnp.dot(s.astype(bf16), vb, preferred_element_type=f32)
        q_dec = (q * jnp.exp2((t_s + 1.0) * log2_gamma)).astype(bf16)
        k_dec = (k * jnp.exp2((TS - 1.0 - t_s) * log2_gamma)).astype(bf16)
        parts = []
        for n in range(NS):
            rows = slice(n * TS, (n + 1) * TS)
            st = s_in[n, h]
            parts.append(jnp.dot(q_dec[rows], st.astype(bf16), preferred_element_type=f32))
            upd = lax.dot_general(k_dec[rows], vb[rows], _TN, preferred_element_type=f32)
            s_new[n, h] = st * (2.0 ** (TS * log2_gamma)) + upd
        o = o + jnp.concatenate(parts, axis=0)
        y = _rms(o, rnw_ref[...]) * (gate * _sigmoid(gate))
        ys_ref[:, h * RET_DV:(h + 1) * RET_DV] = y.astype(bf16)

    for h in range(N_HEADS):
        log2_gamma = lg[h]
        cols = colsl[h]
        q = _rotary(p_buf[:, cols[0]], cos, sin)
        k = _rotary(p_buf[:, cols[1]], cos, sin) * (RET_DK ** -0.5)
        vb = p_buf[:, cols[2]].astype(bf16)
        gate = p_buf[:, cols[3]]
        s = lax.dot_general(q.astype(bf16), k.astype(bf16), _NT, preferred_element_type=f32) * dec_scr[h]
        o = jnp.dot(s.astype(bf16), vb, preferred_element_type=f32)
        st = s_out[0, h]
        q_dec = (q * jnp.exp2((t_k + 1.0) * log2_gamma)).astype(bf16)
        o = o + jnp.dot(q_dec, st.astype(bf16), preferred_element_type=f32)
        k_dec = (k * jnp.exp2((C - 1.0 - t_k) * log2_gamma)).astype(bf16)
        upd = lax.dot_general(k_dec, vb, _TN, preferred_element_type=f32)
        s_out[0, h] = st * (2.0 ** (C * log2_gamma)) + upd
        y = _rms(o, rnw_ref[...]) * (gate * _sigmoid(gate))
        y_scr[:, h * RET_DV:(h + 1) * RET_DV] = y.astype(bf16)
        for cs in cols:
            p_buf[:, cs] = jnp.dot(hn_scr[...], win_ref[:, cs], preferred_element_type=f32)

    m = jnp.dot(y_scr[...], wout_ref[...], preferred_element_type=f32)
    o_ref[0] = xc_ref[0] + _rms(m, nw3_ref[...])


def _odd_layer(x3, ps, s_ret, nw2, nw3, w_in, rnw, w_out):
    n_seq, seq_len, _ = x3.shape
    n_s = s_ret.shape[0]
    TS = ps.shape[0] // n_s
    C = RET_CHUNK
    nc = seq_len // C
    n_tiles = n_seq * nc
    NS = n_s // n_tiles
    assert NS * n_tiles == n_s and (NS * TS) % 16 == 0
    cos, sin = _rope_tables(seq_len, 0)
    cos_s, sin_s = _rope_tables(TS, PAST_LEN)
    x_blk = (1, C, D_MODEL)

    def tile(lag):
        def index(j):
            t = jnp.clip(j - lag, 0, n_tiles - 1)
            return (t // nc, t % nc, 0)
        return index

    short = lambda j: jnp.minimum(j, n_tiles - 1)
    rope_spec = pl.BlockSpec((C, RET_DK // 2), lambda j: (jnp.maximum(j - 1, 0) % nc, 0))
    state_s_spec = pl.BlockSpec((NS, N_HEADS, RET_DK, RET_DV), lambda j: (short(j), 0, 0, 0))
    return pl.pallas_call(
        functools.partial(_odd_layer_body, nc=nc, NS=NS, TS=TS),
        out_shape=(
            jax.ShapeDtypeStruct(x3.shape, f32),
            jax.ShapeDtypeStruct((n_seq, N_HEADS, RET_DK, RET_DV), f32),
            jax.ShapeDtypeStruct((n_s * TS, N_HEADS * RET_DV), bf16),
            jax.ShapeDtypeStruct(s_ret.shape, f32),
        ),
        grid=(n_tiles + 1,),
        in_specs=[
            pl.BlockSpec(x_blk, tile(0)),
            pl.BlockSpec(x_blk, tile(1)),
            rope_spec,
            rope_spec,
            _const_spec((1, D_MODEL)),
            _const_spec((1, D_MODEL)),
            _const_spec(w_in.shape),
            _const_spec((1, RET_DV)),
            _const_spec(w_out.shape),
            pl.BlockSpec((NS * TS, ODD_P), lambda j: (short(j), 0)),
            _const_spec((TS, RET_DK // 2)),
            _const_spec((TS, RET_DK // 2)),
            state_s_spec,
        ],
        out_specs=(
            pl.BlockSpec(x_blk, tile(1)),
            pl.BlockSpec((1, N_HEADS, RET_DK, RET_DV), lambda j: (jnp.maximum(j - 1, 0) // nc, 0, 0, 0)),
            pl.BlockSpec((NS * TS, N_HEADS * RET_DV), lambda j: (short(j), 0)),
            state_s_spec,
        ),
        scratch_shapes=[
            pltpu.VMEM((C, ODD_P), f32),
            pltpu.VMEM((C, D_MODEL), bf16),
            pltpu.VMEM((C, N_HEADS * RET_DV), bf16),
            pltpu.VMEM((N_HEADS, C, C), f32),
        ],
        compiler_params=_params("arbitrary"),
        name="odd_layer",
    )(x3, x3, cos, sin, nw2, nw3, w_in, rnw, w_out, ps, cos_s, sin_s, s_ret)


def _pack_even_w_in(w):
    lr0 = 2 * N_HEADS * GLA_DK + 2 * N_HEADS * EVEN_DV
    lr = jnp.pad(w[:, lr0:lr0 + GLA_LOWRANK], ((0, 0), (0, LANES - GLA_LOWRANK)))
    return jnp.concatenate([w[:, :lr0], w[:, lr0 + GLA_LOWRANK:], lr], axis=1)


def _mixer_even_prompt(x, l, w, row, n_seq, seq_len):
    e = l // 2
    x3, sg, sh = _even_layer_prompt(x.reshape(n_seq, seq_len, D_MODEL), row(2), row(3), w["even_in"][e], w["gla_w2"][e],
                                    w["gla_bg"][e], w["lb_table"], w["gla_nw"][e], w["hgrn_nw"][e], w["even_out"][e],
                                    layer=l)
    return x3.reshape(n_seq * seq_len, D_MODEL), sg, sh


def _mixer_even_sample(x, l, w, row, s_gla, s_hgrn, n_seq, seq_len):
    e = l // 2
    p = _proj(x, row(2), w["even_in"][e], tm=512, tn=EVEN_P)
    y, sg, sh = _even_attn(p, w["gla_w2"][e], w["gla_bg"][e], w["lb_table"], w["gla_nw"][e], w["hgrn_nw"][e],
                           s_gla[e], s_hgrn[e], n_seq=n_seq, seq_len=seq_len, C=seq_len, NB=16, layer=l)
    return _out_block(y, x, w["even_out"][e], row(3)), sg, sh


def _mixer_odd(xp, xs, l, w, row, s_ret, n_seq, seq_len):
    o = l // 2
    ps = _proj(xs, row(2), w["odd_in"][o], tm=256, tn=12 * LANES)
    x3, sr_p, ys, sr_s = _odd_layer(xp.reshape(n_seq, seq_len, D_MODEL), ps, s_ret[o], row(2), row(3), w["odd_in"][o],
                                    w["ret_nw"][o], w["odd_out"][o])
    return x3.reshape(n_seq * seq_len, D_MODEL), _out_block(ys, xs, w["odd_out"][o], row(3)), sr_p, sr_s


def kernel(x_prompt, x_sample, state_gla, state_hgrn, state_ret, norm_w, ffn_w_gate, ffn_w_up, ffn_w_down, even_w_in, gla_w_gate2, gla_b_gate, gla_norm_w, hgrn_lb_table, hgrn_norm_w, even_w_out, odd_w_in, ret_norm_w, odd_w_out):
    bp, tp, _ = x_prompt.shape
    bs, ts, _ = x_sample.shape
    n_even = even_w_in.shape[0]
    nw = norm_w.astype(f32)
    ffn_w = [w_.astype(f32) for w_ in (ffn_w_gate, ffn_w_up, ffn_w_down)]
    w = {
        "even_in": jnp.stack([_pack_even_w_in(even_w_in[e]) for e in range(n_even)]).astype(bf16),
        "gla_w2": jnp.pad(gla_w_gate2, ((0, 0), (0, LANES - GLA_LOWRANK), (0, 0))).astype(bf16),
        "gla_bg": gla_b_gate[:, None, :].astype(f32),
        "lb_table": hgrn_lb_table.astype(f32),
        "gla_nw": gla_norm_w[:, None, :].astype(f32),
        "hgrn_nw": hgrn_norm_w[:, None, :].astype(f32),
        "even_out": even_w_out.astype(bf16),
        "odd_in": odd_w_in.astype(bf16),
        "ret_nw": ret_norm_w[:, None, :].astype(f32),
        "odd_out": odd_w_out.astype(bf16),
    }
    xp = x_prompt.reshape(bp * tp, D_MODEL)
    xs = x_sample.reshape(bs * ts, D_MODEL)
    new_p, new_s = ([], [], []), ([], [], [])
    for l in range(nw.shape[0]):
        row = lambda i, l=l: nw[l, i][None, :]
        xp, xs = _ffn(xp, xs, row(0), row(1), *ffn_w, l, 0)
        if l % 2 == 0:
            xp, sg, sh = _mixer_even_prompt(xp, l, w, row, bp, tp)
            new_p[0].append(sg)
            new_p[1].append(sh)
            xs, sg, sh = _mixer_even_sample(xs, l, w, row, state_gla, state_hgrn, bs, ts)
            new_s[0].append(sg)
            new_s[1].append(sh)
        else:
            xp, xs, sr_p, sr_s = _mixer_odd(xp, xs, l, w, row, state_ret, bp, tp)
            new_p[2].append(sr_p)
            new_s[2].append(sr_s)
        xp, xs = _ffn(xp, xs, row(4), row(5), *ffn_w, l, 1)
    gla_p, hgrn_p, ret_p = (jnp.stack(v) for v in new_p)
    gla_s, hgrn_s, ret_s = (jnp.stack(v) for v in new_s)
    return (xp.reshape(bp, tp, D_MODEL), xs.reshape(bs, ts, D_MODEL), gla_p, hgrn_p, ret_p, gla_s, hgrn_s, ret_s)
```

```python
import functools
import math

import jax
import jax.numpy as jnp
import numpy as np
from jax import lax
from jax.experimental import pallas as pl
from jax.experimental.pallas import tpu as pltpu

f32 = jnp.float32
bf16 = jnp.bfloat16

D_MODEL = 1024
D_FF = 2816
EPS = 1e-6
PAST_LEN = 16384
ROPE_BASE = 10000.0
GLA_TAU = 16.0
N_HEADS = 4
GLA_DK = 64
EVEN_DK = 128
EVEN_DV = 128
GLA_LOWRANK = 16
RET_DK = 256
RET_DV = 512
LANES = 128
SUBLANES = 8
T_QA, T_KA, T_VA, T_RA, T_QB, T_FB, T_IB, T_GB, T_LR = 0, 2, 4, 8, 12, 16, 20, 24, 28
EVEN_P = 29 * LANES
ODD_P = 2 * N_HEADS * RET_DK + 2 * N_HEADS * RET_DV
VMEM_LIMIT = 58 * 1024 * 1024

_NT = (((1,), (1,)), ((), ()))
_TN = (((0,), (0,)), ((), ()))


def _rms(x, w):
    return x * lax.rsqrt(jnp.mean(x * x, axis=-1, keepdims=True) + EPS) * w


def _sigmoid(x):
    return 1.0 / (1.0 + jnp.exp(-x))


def _const_spec(shape):
    return pl.BlockSpec(shape, lambda *_: (0,) * len(shape), pipeline_mode=pl.Buffered(1))


def _params(*sem):
    return pltpu.CompilerParams(dimension_semantics=sem, vmem_limit_bytes=VMEM_LIMIT)


FFN_TF = 256
FFN_CHUNKS = D_FF // FFN_TF


def _ffn_tile(x_ref, nwa_ref, nwb_ref, wg_b, wu_b, wd_b, o_ref, acc_ref):
    x = x_ref[...]
    hn = _rms(x, nwa_ref[...]).astype(bf16)
    for j in range(FFN_CHUNKS):
        g = jnp.dot(hn, wg_b[j], preferred_element_type=f32)
        u = jnp.dot(hn, wu_b[j], preferred_element_type=f32)
        a = (g * _sigmoid(g) * u).astype(bf16)
        part = jnp.dot(a, wd_b[j], preferred_element_type=f32)
        if j == 0:
            acc_ref[...] = part
        else:
            acc_ref[...] += part
    o_ref[...] = x + 0.5 * _rms(acc_ref[...], nwb_ref[...])


def _ffn_body(xa_ref, xb_ref, nwa_ref, nwb_ref, wg_ref, wu_ref, wd_ref, oa_ref, ob_ref, wg_b, wu_b, wd_b, acc_ref,
              hn_scr, *, na):
    i = pl.program_id(0)

    @pl.when(i == 0)
    def _():
        hn_scr[...] = _rms(xa_ref[...], nwa_ref[...]).astype(bf16)

    @pl.when(i < FFN_CHUNKS)
    def _():
        wg_b[i] = wg_ref[...].astype(bf16)
        wu_b[i] = wu_ref[...].astype(bf16)
        wd_b[i] = wd_ref[...].astype(bf16)
        hn = hn_scr[...]
        g = jnp.dot(hn, wg_b[i], preferred_element_type=f32)
        u = jnp.dot(hn, wu_b[i], preferred_element_type=f32)
        a = (g * _sigmoid(g) * u).astype(bf16)
        part = jnp.dot(a, wd_b[i], preferred_element_type=f32)

        @pl.when(i == 0)
        def _():
            acc_ref[...] = part

        @pl.when(i > 0)
        def _():
            acc_ref[...] += part

    @pl.when(i == FFN_CHUNKS - 1)
    def _():
        oa_ref[...] = xa_ref[...] + 0.5 * _rms(acc_ref[...], nwb_ref[...])

    @pl.when((i >= FFN_CHUNKS) & (i < FFN_CHUNKS + na - 1))
    def _():
        _ffn_tile(xa_ref, nwa_ref, nwb_ref, wg_b, wu_b, wd_b, oa_ref, acc_ref)

    @pl.when(i >= FFN_CHUNKS + na - 1)
    def _():
        _ffn_tile(xb_ref, nwa_ref, nwb_ref, wg_b, wu_b, wd_b, ob_ref, acc_ref)


def _ffn(xa, xb, nwa, nwb, wg, wu, wd, l, pos, tm=512):
    na, nb = xa.shape[0] // tm, xb.shape[0] // tm
    chunk = lambda i: jnp.minimum(i, FFN_CHUNKS - 1)
    spec_a = pl.BlockSpec((tm, D_MODEL), lambda i: (jnp.clip(i - FFN_CHUNKS + 1, 0, na - 1), 0))
    spec_b = pl.BlockSpec((tm, D_MODEL), lambda i: (jnp.maximum(i - FFN_CHUNKS - na + 1, 0), 0))
    col_spec = pl.BlockSpec((None, None, D_MODEL, FFN_TF), lambda i: (l, pos, 0, chunk(i)))
    row_spec = pl.BlockSpec((None, None, FFN_TF, D_MODEL), lambda i: (l, pos, chunk(i), 0))
    return pl.pallas_call(
        functools.partial(_ffn_body, na=na),
        out_shape=(jax.ShapeDtypeStruct(xa.shape, f32), jax.ShapeDtypeStruct(xb.shape, f32)),
        grid=(FFN_CHUNKS + na - 1 + nb,),
        in_specs=[
            spec_a,
            spec_b,
            _const_spec((1, D_MODEL)),
            _const_spec((1, D_MODEL)),
            col_spec,
            col_spec,
            row_spec,
        ],
        out_specs=(spec_a, spec_b),
        scratch_shapes=[
            pltpu.VMEM((FFN_CHUNKS, D_MODEL, FFN_TF), bf16),
            pltpu.VMEM((FFN_CHUNKS, D_MODEL, FFN_TF), bf16),
            pltpu.VMEM((FFN_CHUNKS, FFN_TF, D_MODEL), bf16),
            pltpu.VMEM((tm, D_MODEL), f32),
            pltpu.VMEM((tm, D_MODEL), bf16),
        ],
        compiler_params=_params("arbitrary"),
        name="ffn",
    )(xa, xb, nwa, nwb, wg, wu, wd)


def _proj_body(x_ref, nw_ref, w_ref, o_ref, *, tn):
    hn = _rms(x_ref[...], nw_ref[...]).astype(bf16)
    for j in range(w_ref.shape[1] // tn):
        cols = slice(j * tn, (j + 1) * tn)
        o_ref[:, cols] = jnp.dot(hn, w_ref[:, cols], preferred_element_type=f32)


def _proj(x, nw, w, tm, tn):
    n, p = x.shape[0], w.shape[1]
    return pl.pallas_call(
        functools.partial(_proj_body, tn=tn),
        out_shape=jax.ShapeDtypeStruct((n, p), f32),
        grid=(n // tm,),
        in_specs=[
            pl.BlockSpec((tm, D_MODEL), lambda i: (i, 0)),
            _const_spec((1, D_MODEL)),
            _const_spec((D_MODEL, p)),
        ],
        out_specs=pl.BlockSpec((tm, p), lambda i: (i, 0)),
        compiler_params=_params("arbitrary"),
        name="proj",
    )(x, nw, w)


def _out_body(y_ref, x_ref, w_ref, nw_ref, o_ref):
    m = jnp.dot(y_ref[...], w_ref[...], preferred_element_type=f32)
    o_ref[...] = x_ref[...] + _rms(m, nw_ref[...])


def _out_block(y, x, w, nw, tm=512):
    n, dy = y.shape
    return pl.pallas_call(
        _out_body,
        out_shape=jax.ShapeDtypeStruct((n, D_MODEL), f32),
        grid=(n // tm,),
        in_specs=[
            pl.BlockSpec((tm, dy), lambda i: (i, 0)),
            pl.BlockSpec((tm, D_MODEL), lambda i: (i, 0)),
            _const_spec((dy, D_MODEL)),
            _const_spec((1, D_MODEL)),
        ],
        out_specs=pl.BlockSpec((tm, D_MODEL), lambda i: (i, 0)),
        compiler_params=_params("arbitrary"),
        name="mixer_out",
    )(y, x, w, nw)


def _split3(g):
    hi = g.astype(bf16)
    r1 = g - hi.astype(f32)
    mid = r1.astype(bf16)
    lo = (r1 - mid.astype(f32)).astype(bf16)
    return jnp.concatenate([hi, mid, lo], axis=1)


def _roll_in_groups(x, d):
    r = x.shape[0]
    return pltpu.roll(x.reshape(r // SUBLANES, SUBLANES, LANES), d, 1).reshape(r, LANES)


def _near_diagonal(q, k, b, v, ones):
    r = q.shape[0]
    sub = lax.broadcasted_iota(jnp.int32, (r, LANES), 0) % SUBLANES
    prods = [q * k]
    for d in range(1, SUBLANES):
        decay = jnp.exp(jnp.minimum(b - _roll_in_groups(b, d), 0.0))
        prods.append(q * _roll_in_groups(k, d) * decay)
    rs = jnp.dot(jnp.concatenate(prods, axis=0).astype(bf16), ones, preferred_element_type=f32)
    o = rs[:r] * v
    for d in range(1, SUBLANES):
        o = o + jnp.where(sub >= d, rs[d * r:(d + 1) * r], 0.0) * _roll_in_groups(v, d)
    return o


def _even_attn_body(p_ref, w2_ref, bg_ref, lbt_ref, gnw_ref, hnw_ref, sg_in, sh_in, y_ref, sg_out, sh_out, *, C, NB, layer):
    R = NB * C
    ri = lax.broadcasted_iota(jnp.int32, (R, R), 0)
    ci = lax.broadcasted_iota(jnp.int32, (R, R), 1)
    sum_mat = jnp.where(((ri // C) == (ci // C)) & (ci <= ri), 1.0, 0.0).astype(bf16)
    ones = jnp.ones((LANES, LANES), bf16)
    ones_c = jnp.ones((C, LANES), bf16)
    zeros_half = jnp.zeros((GLA_DK, EVEN_DV), f32)

    lr = p_ref[:, T_LR * LANES:(T_LR + 1) * LANES].astype(bf16)
    xg = jnp.dot(lr, w2_ref[...], preferred_element_type=f32) + bg_ref[...]
    log_alpha = (jnp.minimum(xg, 0.0) - jnp.log1p(jnp.exp(-jnp.abs(xg)))) * (1.0 / GLA_TAU)
    low_half = lax.broadcasted_iota(jnp.int32, (R, LANES), 1) < GLA_DK

    lbt = lbt_ref[...]
    lbe = jnp.exp(lbt - jnp.max(lbt, axis=0, keepdims=True))
    lb_all = jnp.sum(lbe[:layer + 1], axis=0, keepdims=True) / jnp.sum(lbe, axis=0, keepdims=True)

    def col(tile):
        return p_ref[:, tile * LANES:(tile + 1) * LANES]

    for h in range(2 * N_HEADS):
        hh = h % N_HEADS
        if h < N_HEADS:
            mine = low_half if h % 2 == 0 else jnp.logical_not(low_half)
            q = jnp.where(mine, col(T_QA + h // 2), 0.0) * (GLA_DK ** -0.5)
            k = jnp.where(mine, col(T_KA + h // 2), 0.0)
            v = col(T_VA + h)
            gate = col(T_RA + h)
            g = log_alpha[:, (h // 2) * LANES:(h // 2 + 1) * LANES]
            nw = gnw_ref[...]
        else:
            qb = col(T_QB + hh)
            q = qb * _sigmoid(qb)
            lb = lb_all[:, hh * LANES:(hh + 1) * LANES]
            f = lb + (1.0 - lb) * _sigmoid(col(T_FB + hh))
            k = 1.0 - f
            g = jnp.log(f)
            v = col(T_IB + hh)
            gate = col(T_GB + hh)
            nw = hnw_ref[...]

        g3 = _split3(g)
        sums = jnp.dot(sum_mat, g3, preferred_element_type=f32)
        b = sums[:, :LANES] + sums[:, LANES:2 * LANES] + sums[:, 2 * LANES:]
        o = _near_diagonal(q, k, b, v, ones)
        q_dec = (q * jnp.exp(b)).astype(bf16)
        vb = v.astype(bf16)
        parts = []
        for n in range(NB):
            rows = slice(n * C, (n + 1) * C)
            b_last = b[(n + 1) * C - 1:(n + 1) * C, :]
            k_dec = (k[rows] * jnp.exp(b_last - b[rows])).astype(bf16)
            tot = lax.dot_general(g3[rows], ones_c, _TN, preferred_element_type=f32)
            decay = jnp.exp(tot[:LANES] + tot[LANES:2 * LANES] + tot[2 * LANES:])
            upd = lax.dot_general(k_dec, vb[rows], _TN, preferred_element_type=f32)
            if h < N_HEADS:
                lo, hi = (h % 2) * GLA_DK, (h % 2 + 1) * GLA_DK
                st = sg_in[n, h]
                st_pad = jnp.concatenate([st, zeros_half] if h % 2 == 0 else [zeros_half, st], axis=0)
                sg_out[n, h] = st * decay[lo:hi] + upd[lo:hi]
            else:
                st_pad = sh_in[n, hh]
                sh_out[n, hh] = st_pad * decay + upd
            parts.append(jnp.dot(q_dec[rows], st_pad.astype(bf16), preferred_element_type=f32))
        o = o + (parts[0] if NB == 1 else jnp.concatenate(parts, axis=0))

        y = _rms(o, nw) * (gate * _sigmoid(gate))
        y_ref[:, h * LANES:(h + 1) * LANES] = y.astype(bf16)


def _even_attn(p, w2, bg, lbt, gnw, hnw, sg, sh, *, n_seq, seq_len, C, NB, layer):
    assert C == SUBLANES, "pairs further apart than one 8-row group are not handled here"
    R = NB * C
    nc = seq_len // C
    assert nc == 1, "one chunk per sequence"
    grid = (n_seq // NB, nc)
    row_map = lambda i, c: (i * nc + c, 0)
    in_specs = [
        pl.BlockSpec((R, EVEN_P), row_map),
        _const_spec(w2.shape),
        _const_spec(bg.shape),
        _const_spec(lbt.shape),
        _const_spec((1, EVEN_DV)),
        _const_spec((1, EVEN_DV)),
    ]
    sg_spec = pl.BlockSpec((NB, N_HEADS, GLA_DK, EVEN_DV), lambda i, c: (i, 0, 0, 0))
    sh_spec = pl.BlockSpec((NB, N_HEADS, EVEN_DK, EVEN_DV), lambda i, c: (i, 0, 0, 0))
    in_specs += [sg_spec, sh_spec]
    return pl.pallas_call(
        functools.partial(_even_attn_body, C=C, NB=NB, layer=layer),
        out_shape=(
            jax.ShapeDtypeStruct((n_seq * seq_len, 2 * N_HEADS * EVEN_DV), bf16),
            jax.ShapeDtypeStruct((n_seq, N_HEADS, GLA_DK, EVEN_DV), f32),
            jax.ShapeDtypeStruct((n_seq, N_HEADS, EVEN_DK, EVEN_DV), f32),
        ),
        grid=grid,
        in_specs=in_specs,
        out_specs=(pl.BlockSpec((R, 2 * N_HEADS * EVEN_DV), row_map), sg_spec, sh_spec),
        compiler_params=_params("arbitrary", "arbitrary"),
        name="even_attn",
    )(p, w2, bg, lbt, gnw, hnw, sg, sh)


CHUNK = 64
N_LEVELS = 6
N_SUM_LEVELS = 3
LOG2E = 1.4426950408889634
EVEN_PW = (T_LR + 2) * LANES


def _chunk_tables():
    t = np.arange(CHUNK)[:, None]
    j = np.arange(CHUNK)[None, :]
    mats = []
    for l in range(N_SUM_LEVELS):
        hs = 1 << l
        m = (t // (2 * hs)) * (2 * hs) + hs
        mats.append(np.where(t >= m, (j > m) & (j <= t), (j > t) & (j <= m)))
    mats.append(j <= t)
    a = np.concatenate(mats, axis=0).astype(np.float32)
    x = t ^ j
    lvl = np.where(j > t, N_LEVELS + 1, np.where(j == t, 0, np.floor(np.log2(np.maximum(x, 1))).astype(np.int64) + 1))
    return np.concatenate([a, a], axis=1), lvl.astype(np.int32)


def _log_decay_sums(g2_list, sum_mat):
    hi = [g.astype(bf16) for g in g2_list]
    lo = [(g - h.astype(f32)).astype(bf16) for g, h in zip(g2_list, hi)]
    rhs = jnp.concatenate([jnp.concatenate(hi, axis=1), jnp.concatenate(lo, axis=1)], axis=0)
    z = jnp.dot(sum_mat, rhs, preferred_element_type=f32)
    return [z[:, i * LANES:(i + 1) * LANES] for i in range(len(g2_list))]


def _level_log_decay(z, b, l):
    if l < N_SUM_LEVELS:
        return z[l * CHUNK:(l + 1) * CHUNK]
    hs = 1 << l
    ref = jnp.concatenate([jnp.broadcast_to(b[m:m + 1], (2 * hs, LANES)) for m in range(hs, CHUNK, 2 * hs)], axis=0)
    d = b - ref
    return jnp.minimum(d, -d)


def _decay_units(qs, ks, vs, sts, zs, level, fillers):
    n = len(qs)
    fillers = list(fillers)
    vbs = [v.astype(bf16) for v in vs]
    bs = [z[N_SUM_LEVELS * CHUNK:] for z in zs]
    scores = [0.0] * n
    for l in range(N_LEVELS):
        es = [jnp.exp2(_level_log_decay(zs[i], bs[i], l)) for i in range(n)]
        ss = [lax.dot_general((qs[i] * es[i]).astype(bf16), (ks[i] * es[i]).astype(bf16), _NT,
                              preferred_element_type=f32) for i in range(n)]
        scores = [jnp.where(level == l + 1, ss[i], scores[i]) for i in range(n)]
        if fillers:
            fillers.pop(0)()
    b_lasts = [b[CHUNK - 1:] for b in bs]
    o_in = [jnp.dot(scores[i].astype(bf16), vbs[i], preferred_element_type=f32) for i in range(n)]
    o_st = [lax.dot_general((qs[i] * jnp.exp2(bs[i])).astype(bf16), sts[i].astype(bf16), _NT,
                            preferred_element_type=f32) for i in range(n)]
    upd = [lax.dot_general(vbs[i], (ks[i] * jnp.exp2(b_lasts[i] - bs[i])).astype(bf16), _TN,
                           preferred_element_type=f32) for i in range(n)]
    while fillers:
        fillers.pop(0)()
    outs = [jnp.sum(qs[i] * ks[i], axis=-1, keepdims=True) * vs[i] + o_in[i] + o_st[i] for i in range(n)]
    return outs, [sts[i] * jnp.exp2(b_lasts[i]) + upd[i] for i in range(n)]


def _even_layer_body(xa_ref, xc_ref, nw2_ref, nw3_ref, win_ref, w2_ref, bg_ref, lbt_ref, gnw_ref, hnw_ref, wout_ref,
                     smat_ref, lvl_ref, o_ref, sg_out, sh_out, p_buf, hn_scr, y_scr, st_ref, *, NB, nc, layer):
    R = NB * CHUNK
    j = pl.program_id(0)
    cb = jnp.maximum(j - 1, 0) % nc
    xg0 = T_LR * LANES

    @pl.when(j == 0)
    def _():
        p_buf[...] = jnp.zeros(p_buf.shape, f32)

    @pl.when(cb == 0)
    def _():
        st_ref[...] = jnp.zeros(st_ref.shape, f32)

    hn_scr[...] = _rms(xa_ref[...].reshape(R, D_MODEL), nw2_ref[...]).astype(bf16)

    lbt = lbt_ref[...]
    lbe = jnp.exp(lbt - jnp.max(lbt, axis=0, keepdims=True))
    lb_all = jnp.sum(lbe[:layer + 1], axis=0, keepdims=True) / jnp.sum(lbe, axis=0, keepdims=True)
    sum_mat = smat_ref[...]
    level = lvl_ref[...]

    low_half = lax.broadcasted_iota(jnp.int32, (CHUNK, LANES), 1) < GLA_DK

    def proj_piece(tile):
        def run():
            cols = slice(tile * LANES, (tile + 2) * LANES)
            p_buf[:, cols] = jnp.dot(hn_scr[...], win_ref[:, cols], preferred_element_type=f32)
        return run

    def gate_piece():
        lr = jnp.dot(hn_scr[...], win_ref[:, xg0:], preferred_element_type=f32).astype(bf16)
        p_buf[:, xg0:] = jnp.dot(lr, w2_ref[...], preferred_element_type=f32) + bg_ref[...]

    def tile(rows, t):
        return p_buf[rows, t * LANES:(t + 1) * LANES]

    def prepare(h0):
        qs, ks, g2s, vs, gates, ids = [], [], [], [], [], []
        for h in (h0, h0 + 1):
            hh = h % N_HEADS
            for n in range(NB):
                rows = slice(n * CHUNK, (n + 1) * CHUNK)
                if h < N_HEADS:
                    mine = low_half if h % 2 == 0 else jnp.logical_not(low_half)
                    xg = tile(rows, T_LR + h // 2)
                    qs.append(jnp.where(mine, tile(rows, T_QA + h // 2), 0.0) * (GLA_DK ** -0.5))
                    ks.append(jnp.where(mine, tile(rows, T_KA + h // 2), 0.0))
                    g2s.append((jnp.minimum(xg, 0.0) - jnp.log1p(jnp.exp(-jnp.abs(xg)))) * (LOG2E / GLA_TAU))
                    vs.append(tile(rows, T_VA + h))
                    gates.append(tile(rows, T_RA + h))
                else:
                    lb = lb_all[:, hh * LANES:(hh + 1) * LANES]
                    f = lb + (1.0 - lb) * _sigmoid(tile(rows, T_FB + hh))
                    q = tile(rows, T_QB + hh)
                    qs.append(q * _sigmoid(q))
                    ks.append(1.0 - f)
                    g2s.append(jnp.log2(f))
                    vs.append(tile(rows, T_IB + hh))
                    gates.append(tile(rows, T_GB + hh))
                ids.append((n, h))
        zs = []
        for i in range(0, len(g2s), 2):
            zs += _log_decay_sums(g2s[i:i + 2], sum_mat)
        return qs, ks, vs, gates, ids, zs

    ready = {0: prepare(0)}
    for h0 in range(0, 2 * N_HEADS, 2):
        nw = gnw_ref[...] if h0 < N_HEADS else hnw_ref[...]
        qs, ks, vs, gates, ids, zs = ready.pop(h0)
        if h0 == 0:
            fillers = [proj_piece(T_VA), proj_piece(T_RA)]
        elif h0 == 2:
            fillers = [proj_piece(T_VA + 2), proj_piece(T_RA + 2), proj_piece(T_QA), proj_piece(T_KA), gate_piece]
        else:
            fillers = [proj_piece(t + h0 - N_HEADS) for t in (T_QB, T_FB, T_IB, T_GB)]
        if h0 + 2 < 2 * N_HEADS:
            def prep_next(h1=h0 + 2):
                ready[h1] = prepare(h1)
            fillers.insert(1, prep_next)
        outs, sts = _decay_units(qs, ks, vs, [st_ref[n, h] for n, h in ids], zs, level, fillers)
        for i, (n, h) in enumerate(ids):
            rows = slice(n * CHUNK, (n + 1) * CHUNK)
            st_ref[n, h] = sts[i]
            y = _rms(outs[i], nw) * (gates[i] * _sigmoid(gates[i]))
            y_scr[rows, h * LANES:(h + 1) * LANES] = y.astype(bf16)

    m = jnp.dot(y_scr[...], wout_ref[...], preferred_element_type=f32)
    o_ref[...] = (xc_ref[...].reshape(R, D_MODEL) + _rms(m, nw3_ref[...])).reshape(NB, CHUNK, D_MODEL)

    @pl.when((cb == nc - 1) & (j >= 1))
    def _():
        for n in range(NB):
            for h in range(N_HEADS):
                sg_out[n, h] = st_ref[n, h].T[(h % 2) * GLA_DK:(h % 2 + 1) * GLA_DK, :]
                sh_out[n, h] = st_ref[n, N_HEADS + h].T


def _even_layer_prompt(x3, nw2, nw3, w_in, w2, bg, lbt, gnw, hnw, w_out, *, layer, NB=4):
    n_seq, seq_len, _ = x3.shape
    nc = seq_len // CHUNK
    n_tiles = (n_seq // NB) * nc
    R = NB * CHUNK
    smat, lvl = _chunk_tables()
    x_blk = (NB, CHUNK, D_MODEL)

    def tile(lag):
        def index(j):
            t = jnp.clip(j - lag, 0, n_tiles - 1)
            return (t // nc, t % nc, 0)
        return index

    group = lambda j: (jnp.maximum(j - 1, 0) // nc, 0, 0, 0)
    return pl.pallas_call(
        functools.partial(_even_layer_body, NB=NB, nc=nc, layer=layer),
        out_shape=(
            jax.ShapeDtypeStruct(x3.shape, f32),
            jax.ShapeDtypeStruct((n_seq, N_HEADS, GLA_DK, EVEN_DV), f32),
            jax.ShapeDtypeStruct((n_seq, N_HEADS, EVEN_DK, EVEN_DV), f32),
        ),
        grid=(n_tiles + 1,),
        in_specs=[
            pl.BlockSpec(x_blk, tile(0)),
            pl.BlockSpec(x_blk, tile(1)),
            _const_spec((1, D_MODEL)),
            _const_spec((1, D_MODEL)),
            _const_spec(w_in.shape),
            _const_spec(w2.shape),
            _const_spec(bg.shape),
            _const_spec(lbt.shape),
            _const_spec((1, EVEN_DV)),
            _const_spec((1, EVEN_DV)),
            _const_spec(w_out.shape),
            _const_spec(smat.shape),
            _const_spec(lvl.shape),
        ],
        out_specs=(
            pl.BlockSpec(x_blk, tile(1)),
            pl.BlockSpec((NB, N_HEADS, GLA_DK, EVEN_DV), group),
            pl.BlockSpec((NB, N_HEADS, EVEN_DK, EVEN_DV), group),
        ),
        scratch_shapes=[
            pltpu.VMEM((R, EVEN_PW), f32),
            pltpu.VMEM((R, D_MODEL), bf16),
            pltpu.VMEM((R, 2 * N_HEADS * EVEN_DV), bf16),
            pltpu.VMEM((NB, 2 * N_HEADS, EVEN_DV, EVEN_DK), f32),
        ],
        compiler_params=_params("arbitrary"),
        name="even_layer",
    )(x3, x3, nw2, nw3, w_in, w2, bg, lbt, gnw, hnw, w_out, jnp.asarray(smat, bf16), jnp.asarray(lvl))


RET_CHUNK = 256


def _rope_body(cos_ref, sin_ref, *, pos0):
    n, half = cos_ref.shape
    pos = (pos0 + lax.broadcasted_iota(jnp.int32, (n, half), 0)).astype(f32)
    lane = lax.broadcasted_iota(jnp.int32, (n, half), 1).astype(f32)
    ang = pos * jnp.power(ROPE_BASE, -lane / half)
    cos_ref[...] = jnp.cos(ang)
    sin_ref[...] = jnp.sin(ang)


def _rope_tables(n, pos0):
    shape = jax.ShapeDtypeStruct((n, RET_DK // 2), f32)
    return pl.pallas_call(functools.partial(_rope_body, pos0=pos0), out_shape=(shape, shape), name="rope_tables")()


def _ret_cols(h):
    return [slice(off + h * w, off + (h + 1) * w) for off, w in (
        (0, RET_DK), (N_HEADS * RET_DK, RET_DK), (2 * N_HEADS * RET_DK, RET_DV),
        (2 * N_HEADS * RET_DK + N_HEADS * RET_DV, RET_DV))]


def _rotary(x, cos, sin):
    half = x.shape[1] // 2
    x1, x2 = x[:, :half], x[:, half:]
    return jnp.concatenate([x1 * cos - x2 * sin, x1 * sin + x2 * cos], axis=1)


def _odd_layer_body(xa_ref, xc_ref, cos_ref, sin_ref, nw2_ref, nw3_ref, win_ref, rnw_ref, wout_ref,
                    ps_ref, cos_s_ref, sin_s_ref, s_in, o_ref, s_out, ys_ref, s_new,
                    p_buf, hn_scr, y_scr, dec_scr, *, nc, NS, TS):
    C = RET_CHUNK
    j = pl.program_id(0)
    cb = jnp.maximum(j - 1, 0) % nc

    @pl.when(j == 0)
    def _():
        p_buf[...] = jnp.zeros(p_buf.shape, f32)
        ri = lax.broadcasted_iota(jnp.int32, (C, C), 0)
        ci = lax.broadcasted_iota(jnp.int32, (C, C), 1)
        dist = jnp.maximum(ri - ci, 0).astype(f32)
        for h in range(N_HEADS):
            dec_scr[h] = jnp.where(ci <= ri, jnp.exp2(dist * math.log2(1.0 - 2.0 ** (-5.0 - h))), 0.0)

    @pl.when(cb == 0)
    def _():
        s_out[...] = jnp.zeros(s_out.shape, f32)

    hn_scr[...] = _rms(xa_ref[0], nw2_ref[...]).astype(bf16)
    cos, sin = cos_ref[...], sin_ref[...]
    t_k = lax.broadcasted_iota(jnp.int32, (C, RET_DK), 0).astype(f32)

    RS = NS * TS
    cos_s = jnp.concatenate([cos_s_ref[...]] * NS, axis=0)
    sin_s = jnp.concatenate([sin_s_ref[...]] * NS, axis=0)
    ri = lax.broadcasted_iota(jnp.int32, (RS, RS), 0)
    ci = lax.broadcasted_iota(jnp.int32, (RS, RS), 1)
    visible = ((ri // TS) == (ci // TS)) & (ci <= ri)
    dist_s = jnp.where(visible, ri - ci, 0).astype(f32)
    t_s = (lax.broadcasted_iota(jnp.int32, (RS, RET_DK), 0) % TS).astype(f32)

    lg = [math.log2(1.0 - 2.0 ** (-5.0 - h)) for h in range(N_HEADS)]
    colsl = [_ret_cols(h) for h in range(N_HEADS)]
    for h in range(N_HEADS):
        log2_gamma = lg[h]
        cols = colsl[h]
        q = _rotary(ps_ref[:, cols[0]], cos_s, sin_s)
        k = _rotary(ps_ref[:, cols[1]], cos_s, sin_s) * (RET_DK ** -0.5)
        vb = ps_ref[:, cols[2]].astype(bf16)
        gate = ps_ref[:, cols[3]]
        decay = jnp.where(visible, jnp.exp2(dist_s * log2_gamma), 0.0)
        s = lax.dot_general(q.astype(bf16), k.astype(bf16), _NT, preferred_element_type=f32) * decay
        o = jnp.dot(s.astype(bf16), vb, preferred_element_type=f32)
        q_dec = (q * jnp.exp2((t_s + 1.0) * log2_gamma)).astype(bf16)
        k_dec = (k * jnp.exp2((TS - 1.0 - t_s) * log2_gamma)).astype(bf16)
        parts = []
        for n in range(NS):
            rows = slice(n * TS, (n + 1) * TS)
            st = s_in[n, h]
            parts.append(jnp.dot(q_dec[rows], st.astype(bf16), preferred_element_type=f32))
            upd = lax.dot_general(k_dec[rows], vb[rows], _TN, preferred_element_type=f32)
            s_new[n, h] = st * (2.0 ** (TS * log2_gamma)) + upd
        o = o + jnp.concatenate(parts, axis=0)
        y = _rms(o, rnw_ref[...]) * (gate * _sigmoid(gate))
        ys_ref[:, h * RET_DV:(h + 1) * RET_DV] = y.astype(bf16)

    for h in range(N_HEADS):
        log2_gamma = lg[h]
        cols = colsl[h]
        q = _rotary(p_buf[:, cols[0]], cos, sin)
        k = _rotary(p_buf[:, cols[1]], cos, sin) * (RET_DK ** -0.5)
        vb = p_buf[:, cols[2]].astype(bf16)
        gate = p_buf[:, cols[3]]
        s = lax.dot_general(q.astype(bf16), k.astype(bf16), _NT, preferred_element_type=f32) * dec_scr[h]
        o = jnp.dot(s.astype(bf16), vb, preferred_element_type=f32)
        st = s_out[0, h]
        q_dec = (q * jnp.exp2((t_k + 1.0) * log2_gamma)).astype(bf16)
        o = o + jnp.dot(q_dec, st.astype(bf16), preferred_element_type=f32)
        k_dec = (k * jnp.exp2((C - 1.0 - t_k) * log2_gamma)).astype(bf16)
        upd = lax.dot_general(k_dec, vb, _TN, preferred_element_type=f32)
        s_out[0, h] = st * (2.0 ** (C * log2_gamma)) + upd
        y = _rms(o, rnw_ref[...]) * (gate * _sigmoid(gate))
        y_scr[:, h * RET_DV:(h + 1) * RET_DV] = y.astype(bf16)
        for cs in cols:
            p_buf[:, cs] = jnp.dot(hn_scr[...], win_ref[:, cs], preferred_element_type=f32)

    m = jnp.dot(y_scr[...], wout_ref[...], preferred_element_type=f32)
    o_ref[0] = xc_ref[0] + _rms(m, nw3_ref[...])


def _odd_layer(x3, ps, s_ret, nw2, nw3, w_in, rnw, w_out):
    n_seq, seq_len, _ = x3.shape
    n_s = s_ret.shape[0]
    TS = ps.shape[0] // n_s
    C = RET_CHUNK
    nc = seq_len // C
    n_tiles = n_seq * nc
    NS = n_s // n_tiles
    assert NS * n_tiles == n_s and (NS * TS) % 16 == 0
    cos, sin = _rope_tables(seq_len, 0)
    cos_s, sin_s = _rope_tables(TS, PAST_LEN)
    x_blk = (1, C, D_MODEL)

    def tile(lag):
        def index(j):
            t = jnp.clip(j - lag, 0, n_tiles - 1)
            return (t // nc, t % nc, 0)
        return index

    short = lambda j: jnp.minimum(j, n_tiles - 1)
    rope_spec = pl.BlockSpec((C, RET_DK // 2), lambda j: (jnp.maximum(j - 1, 0) % nc, 0))
    state_s_spec = pl.BlockSpec((NS, N_HEADS, RET_DK, RET_DV), lambda j: (short(j), 0, 0, 0))
    return pl.pallas_call(
        functools.partial(_odd_layer_body, nc=nc, NS=NS, TS=TS),
        out_shape=(
            jax.ShapeDtypeStruct(x3.shape, f32),
            jax.ShapeDtypeStruct((n_seq, N_HEADS, RET_DK, RET_DV), f32),
            jax.ShapeDtypeStruct((n_s * TS, N_HEADS * RET_DV), bf16),
            jax.ShapeDtypeStruct(s_ret.shape, f32),
        ),
        grid=(n_tiles + 1,),
        in_specs=[
            pl.BlockSpec(x_blk, tile(0)),
            pl.BlockSpec(x_blk, tile(1)),
            rope_spec,
            rope_spec,
            _const_spec((1, D_MODEL)),
            _const_spec((1, D_MODEL)),
            _const_spec(w_in.shape),
            _const_spec((1, RET_DV)),
            _const_spec(w_out.shape),
            pl.BlockSpec((NS * TS, ODD_P), lambda j: (short(j), 0)),
            _const_spec((TS, RET_DK // 2)),
            _const_spec((TS, RET_DK // 2)),
            state_s_spec,
        ],
        out_specs=(
            pl.BlockSpec(x_blk, tile(1)),
            pl.BlockSpec((1, N_HEADS, RET_DK, RET_DV), lambda j: (jnp.maximum(j - 1, 0) // nc, 0, 0, 0)),
            pl.BlockSpec((NS * TS, N_HEADS * RET_DV), lambda j: (short(j), 0)),
            state_s_spec,
        ),
        scratch_shapes=[
            pltpu.VMEM((C, ODD_P), f32),
            pltpu.VMEM((C, D_MODEL), bf16),
            pltpu.VMEM((C, N_HEADS * RET_DV), bf16),
            pltpu.VMEM((N_HEADS, C, C), f32),
        ],
        compiler_params=_params("arbitrary"),
        name="odd_layer",
    )(x3, x3, cos, sin, nw2, nw3, w_in, rnw, w_out, ps, cos_s, sin_s, s_ret)


def _pack_even_w_in(w):
    lr0 = 2 * N_HEADS * GLA_DK + 2 * N_HEADS * EVEN_DV
    lr = jnp.pad(w[:, lr0:lr0 + GLA_LOWRANK], ((0, 0), (0, LANES - GLA_LOWRANK)))
    return jnp.concatenate([w[:, :lr0], w[:, lr0 + GLA_LOWRANK:], lr], axis=1)


def _mixer_even_prompt(x, l, w, row, n_seq, seq_len):
    e = l // 2
    x3, sg, sh = _even_layer_prompt(x.reshape(n_seq, seq_len, D_MODEL), row(2), row(3), w["even_in"][e], w["gla_w2"][e],
                                    w["gla_bg"][e], w["lb_table"], w["gla_nw"][e], w["hgrn_nw"][e], w["even_out"][e],
                                    layer=l)
    return x3.reshape(n_seq * seq_len, D_MODEL), sg, sh


def _mixer_even_sample(x, l, w, row, s_gla, s_hgrn, n_seq, seq_len):
    e = l // 2
    p = _proj(x, row(2), w["even_in"][e], tm=512, tn=EVEN_P)
    y, sg, sh = _even_attn(p, w["gla_w2"][e], w["gla_bg"][e], w["lb_table"], w["gla_nw"][e], w["hgrn_nw"][e],
                           s_gla[e], s_hgrn[e], n_seq=n_seq, seq_len=seq_len, C=seq_len, NB=16, layer=l)
    return _out_block(y, x, w["even_out"][e], row(3)), sg, sh


def _mixer_odd(xp, xs, l, w, row, s_ret, n_seq, seq_len):
    o = l // 2
    ps = _proj(xs, row(2), w["odd_in"][o], tm=256, tn=12 * LANES)
    x3, sr_p, ys, sr_s = _odd_layer(xp.reshape(n_seq, seq_len, D_MODEL), ps, s_ret[o], row(2), row(3), w["odd_in"][o],
                                    w["ret_nw"][o], w["odd_out"][o])
    return x3.reshape(n_seq * seq_len, D_MODEL), _out_block(ys, xs, w["odd_out"][o], row(3)), sr_p, sr_s


def kernel(x_prompt, x_sample, state_gla, state_hgrn, state_ret, norm_w, ffn_w_gate, ffn_w_up, ffn_w_down, even_w_in, gla_w_gate2, gla_b_gate, gla_norm_w, hgrn_lb_table, hgrn_norm_w, even_w_out, odd_w_in, ret_norm_w, odd_w_out):
    bp, tp, _ = x_prompt.shape
    bs, ts, _ = x_sample.shape
    n_even = even_w_in.shape[0]
    nw = norm_w.astype(f32)
    ffn_w = [w_.astype(f32) for w_ in (ffn_w_gate, ffn_w_up, ffn_w_down)]
    w = {
        "even_in": jnp.stack([_pack_even_w_in(even_w_in[e]) for e in range(n_even)]).astype(bf16),
        "gla_w2": jnp.pad(gla_w_gate2, ((0, 0), (0, LANES - GLA_LOWRANK), (0, 0))).astype(bf16),
        "gla_bg": gla_b_gate[:, None, :].astype(f32),
        "lb_table": hgrn_lb_table.astype(f32),
        "gla_nw": gla_norm_w[:, None, :].astype(f32),
        "hgrn_nw": hgrn_norm_w[:, None, :].astype(f32),
        "even_out": even_w_out.astype(bf16),
        "odd_in": odd_w_in.astype(bf16),
        "ret_nw": ret_norm_w[:, None, :].astype(f32),
        "odd_out": odd_w_out.astype(bf16),
    }
    xp = x_prompt.reshape(bp * tp, D_MODEL)
    xs = x_sample.reshape(bs * ts, D_MODEL)
    new_p, new_s = ([], [], []), ([], [], [])
    for l in range(nw.shape[0]):
        row = lambda i, l=l: nw[l, i][None, :]
        xp, xs = _ffn(xp, xs, row(0), row(1), *ffn_w, l, 0)
        if l % 2 == 0:
            xp, sg, sh = _mixer_even_prompt(xp, l, w, row, bp, tp)
            new_p[0].append(sg)
            new_p[1].append(sh)
            xs, sg, sh = _mixer_even_sample(xs, l, w, row, state_gla, state_hgrn, bs, ts)
            new_s[0].append(sg)
            new_s[1].append(sh)
        else:
            xp, xs, sr_p, sr_s = _mixer_odd(xp, xs, l, w, row, state_ret, bp, tp)
            new_p[2].append(sr_p)
            new_s[2].append(sr_s)
        xp, xs = _ffn(xp, xs, row(4), row(5), *ffn_w, l, 1)
    gla_p, hgrn_p, ret_p = (jnp.stack(v) for v in new_p)
    gla_s, hgrn_s, ret_s = (jnp.stack(v) for v in new_s)
    return (xp.reshape(bp, tp, D_MODEL), xs.reshape(bs, ts, D_MODEL), gla_p, hgrn_p, ret_p, gla_s, hgrn_s, ret_s)
```

```python
import functools
import math

import jax
import jax.numpy as jnp
import numpy as np
from jax import lax
from jax.experimental import pallas as pl
from jax.experimental.pallas import tpu as pltpu

f32 = jnp.float32
bf16 = jnp.bfloat16

D_MODEL = 1024
D_FF = 2816
EPS = 1e-6
PAST_LEN = 16384
ROPE_BASE = 10000.0
GLA_TAU = 16.0
N_HEADS = 4
GLA_DK = 64
EVEN_DK = 128
EVEN_DV = 128
GLA_LOWRANK = 16
RET_DK = 256
RET_DV = 512
LANES = 128
SUBLANES = 8
T_QA, T_KA, T_VA, T_RA, T_QB, T_FB, T_IB, T_GB, T_LR = 0, 2, 4, 8, 12, 16, 20, 24, 28
EVEN_P = 29 * LANES
ODD_P = 2 * N_HEADS * RET_DK + 2 * N_HEADS * RET_DV
VMEM_LIMIT = 58 * 1024 * 1024

_NT = (((1,), (1,)), ((), ()))
_TN = (((0,), (0,)), ((), ()))


def _rms(x, w):
    return x * lax.rsqrt(jnp.mean(x * x, axis=-1, keepdims=True) + EPS) * w


def _sigmoid(x):
    return 1.0 / (1.0 + jnp.exp(-x))


def _const_spec(shape):
    return pl.BlockSpec(shape, lambda *_: (0,) * len(shape), pipeline_mode=pl.Buffered(1))


def _params(*sem):
    return pltpu.CompilerParams(dimension_semantics=sem, vmem_limit_bytes=VMEM_LIMIT)


FFN_TF = 256
FFN_CHUNKS = D_FF // FFN_TF


def _ffn_tile(x_ref, nwa_ref, nwb_ref, wg_b, wu_b, wd_b, o_ref, acc_ref):
    x = x_ref[...]
    hn = _rms(x, nwa_ref[...]).astype(bf16)
    for j in range(FFN_CHUNKS):
        g = jnp.dot(hn, wg_b[j], preferred_element_type=f32)
        u = jnp.dot(hn, wu_b[j], preferred_element_type=f32)
        a = (g * _sigmoid(g) * u).astype(bf16)
        part = jnp.dot(a, wd_b[j], preferred_element_type=f32)
        if j == 0:
            acc_ref[...] = part
        else:
            acc_ref[...] += part
    o_ref[...] = x + 0.5 * _rms(acc_ref[...], nwb_ref[...])


def _ffn_body(xa_ref, xb_ref, nwa_ref, nwb_ref, wg_ref, wu_ref, wd_ref, oa_ref, ob_ref, wg_b, wu_b, wd_b, acc_ref,
              hn_scr, *, na):
    i = pl.program_id(0)

    @pl.when(i == 0)
    def _():
        hn_scr[...] = _rms(xa_ref[...], nwa_ref[...]).astype(bf16)

    @pl.when(i < FFN_CHUNKS)
    def _():
        wg_b[i] = wg_ref[...].astype(bf16)
        wu_b[i] = wu_ref[...].astype(bf16)
        wd_b[i] = wd_ref[...].astype(bf16)
        hn = hn_scr[...]
        g = jnp.dot(hn, wg_b[i], preferred_element_type=f32)
        u = jnp.dot(hn, wu_b[i], preferred_element_type=f32)
        a = (g * _sigmoid(g) * u).astype(bf16)
        part = jnp.dot(a, wd_b[i], preferred_element_type=f32)

        @pl.when(i == 0)
        def _():
            acc_ref[...] = part

        @pl.when(i > 0)
        def _():
            acc_ref[...] += part

    @pl.when(i == FFN_CHUNKS - 1)
    def _():
        oa_ref[...] = xa_ref[...] + 0.5 * _rms(acc_ref[...], nwb_ref[...])

    @pl.when((i >= FFN_CHUNKS) & (i < FFN_CHUNKS + na - 1))
    def _():
        _ffn_tile(xa_ref, nwa_ref, nwb_ref, wg_b, wu_b, wd_b, oa_ref, acc_ref)

    @pl.when(i >= FFN_CHUNKS + na - 1)
    def _():
        _ffn_tile(xb_ref, nwa_ref, nwb_ref, wg_b, wu_b, wd_b, ob_ref, acc_ref)


def _ffn(xa, xb, nwa, nwb, wg, wu, wd, l, pos, tm=512):
    na, nb = xa.shape[0] // tm, xb.shape[0] // tm
    chunk = lambda i: jnp.minimum(i, FFN_CHUNKS - 1)
    spec_a = pl.BlockSpec((tm, D_MODEL), lambda i: (jnp.clip(i - FFN_CHUNKS + 1, 0, na - 1), 0))
    spec_b = pl.BlockSpec((tm, D_MODEL), lambda i: (jnp.maximum(i - FFN_CHUNKS - na + 1, 0), 0))
    col_spec = pl.BlockSpec((None, None, D_MODEL, FFN_TF), lambda i: (l, pos, 0, chunk(i)))
    row_spec = pl.BlockSpec((None, None, FFN_TF, D_MODEL), lambda i: (l, pos, chunk(i), 0))
    return pl.pallas_call(
        functools.partial(_ffn_body, na=na),
        out_shape=(jax.ShapeDtypeStruct(xa.shape, f32), jax.ShapeDtypeStruct(xb.shape, f32)),
        grid=(FFN_CHUNKS + na - 1 + nb,),
        in_specs=[
            spec_a,
            spec_b,
            _const_spec((1, D_MODEL)),
            _const_spec((1, D_MODEL)),
            col_spec,
            col_spec,
            row_spec,
        ],
        out_specs=(spec_a, spec_b),
        scratch_shapes=[
            pltpu.VMEM((FFN_CHUNKS, D_MODEL, FFN_TF), bf16),
            pltpu.VMEM((FFN_CHUNKS, D_MODEL, FFN_TF), bf16),
            pltpu.VMEM((FFN_CHUNKS, FFN_TF, D_MODEL), bf16),
            pltpu.VMEM((tm, D_MODEL), f32),
            pltpu.VMEM((tm, D_MODEL), bf16),
        ],
        compiler_params=_params("arbitrary"),
        name="ffn",
    )(xa, xb, nwa, nwb, wg, wu, wd)


def _proj_body(x_ref, nw_ref, w_ref, o_ref, *, tn):
    hn = _rms(x_ref[...], nw_ref[...]).astype(bf16)
    for j in range(w_ref.shape[1] // tn):
        cols = slice(j * tn, (j + 1) * tn)
        o_ref[:, cols] = jnp.dot(hn, w_ref[:, cols], preferred_element_type=f32)


def _proj(x, nw, w, tm, tn):
    n, p = x.shape[0], w.shape[1]
    return pl.pallas_call(
        functools.partial(_proj_body, tn=tn),
        out_shape=jax.ShapeDtypeStruct((n, p), f32),
        grid=(n // tm,),
        in_specs=[
            pl.BlockSpec((tm, D_MODEL), lambda i: (i, 0)),
            _const_spec((1, D_MODEL)),
            _const_spec((D_MODEL, p)),
        ],
        out_specs=pl.BlockSpec((tm, p), lambda i: (i, 0)),
        compiler_params=_params("arbitrary"),
        name="proj",
    )(x, nw, w)


def _out_body(y_ref, x_ref, w_ref, nw_ref, o_ref):
    m = jnp.dot(y_ref[...], w_ref[...], preferred_element_type=f32)
    o_ref[...] = x_ref[...] + _rms(m, nw_ref[...])


def _out_block(y, x, w, nw, tm=512):
    n, dy = y.shape
    return pl.pallas_call(
        _out_body,
        out_shape=jax.ShapeDtypeStruct((n, D_MODEL), f32),
        grid=(n // tm,),
        in_specs=[
            pl.BlockSpec((tm, dy), lambda i: (i, 0)),
            pl.BlockSpec((tm, D_MODEL), lambda i: (i, 0)),
            _const_spec((dy, D_MODEL)),
            _const_spec((1, D_MODEL)),
        ],
        out_specs=pl.BlockSpec((tm, D_MODEL), lambda i: (i, 0)),
        compiler_params=_params("arbitrary"),
        name="mixer_out",
    )(y, x, w, nw)


def _split3(g):
    hi = g.astype(bf16)
    r1 = g - hi.astype(f32)
    mid = r1.astype(bf16)
    lo = (r1 - mid.astype(f32)).astype(bf16)
    return jnp.concatenate([hi, mid, lo], axis=1)


def _roll_in_groups(x, d):
    r = x.shape[0]
    return pltpu.roll(x.reshape(r // SUBLANES, SUBLANES, LANES), d, 1).reshape(r, LANES)


def _near_diagonal(q, k, b, v, ones):
    r = q.shape[0]
    sub = lax.broadcasted_iota(jnp.int32, (r, LANES), 0) % SUBLANES
    prods = [q * k]
    for d in range(1, SUBLANES):
        decay = jnp.exp(jnp.minimum(b - _roll_in_groups(b, d), 0.0))
        prods.append(q * _roll_in_groups(k, d) * decay)
    rs = jnp.dot(jnp.concatenate(prods, axis=0).astype(bf16), ones, preferred_element_type=f32)
    o = rs[:r] * v
    for d in range(1, SUBLANES):
        o = o + jnp.where(sub >= d, rs[d * r:(d + 1) * r], 0.0) * _roll_in_groups(v, d)
    return o


def _even_attn_body(p_ref, w2_ref, bg_ref, lbt_ref, gnw_ref, hnw_ref, sg_in, sh_in, y_ref, sg_out, sh_out, *, C, NB, layer):
    R = NB * C
    ri = lax.broadcasted_iota(jnp.int32, (R, R), 0)
    ci = lax.broadcasted_iota(jnp.int32, (R, R), 1)
    sum_mat = jnp.where(((ri // C) == (ci // C)) & (ci <= ri), 1.0, 0.0).astype(bf16)
    ones = jnp.ones((LANES, LANES), bf16)
    ones_c = jnp.ones((C, LANES), bf16)
    zeros_half = jnp.zeros((GLA_DK, EVEN_DV), f32)

    lr = p_ref[:, T_LR * LANES:(T_LR + 1) * LANES].astype(bf16)
    xg = jnp.dot(lr, w2_ref[...], preferred_element_type=f32) + bg_ref[...]
    log_alpha = (jnp.minimum(xg, 0.0) - jnp.log1p(jnp.exp(-jnp.abs(xg)))) * (1.0 / GLA_TAU)
    low_half = lax.broadcasted_iota(jnp.int32, (R, LANES), 1) < GLA_DK

    lbt = lbt_ref[...]
    lbe = jnp.exp(lbt - jnp.max(lbt, axis=0, keepdims=True))
    lb_all = jnp.sum(lbe[:layer + 1], axis=0, keepdims=True) / jnp.sum(lbe, axis=0, keepdims=True)

    def col(tile):
        return p_ref[:, tile * LANES:(tile + 1) * LANES]

    for h in range(2 * N_HEADS):
        hh = h % N_HEADS
        if h < N_HEADS:
            mine = low_half if h % 2 == 0 else jnp.logical_not(low_half)
            q = jnp.where(mine, col(T_QA + h // 2), 0.0) * (GLA_DK ** -0.5)
            k = jnp.where(mine, col(T_KA + h // 2), 0.0)
            v = col(T_VA + h)
            gate = col(T_RA + h)
            g = log_alpha[:, (h // 2) * LANES:(h // 2 + 1) * LANES]
            nw = gnw_ref[...]
        else:
            qb = col(T_QB + hh)
            q = qb * _sigmoid(qb)
            lb = lb_all[:, hh * LANES:(hh + 1) * LANES]
            f = lb + (1.0 - lb) * _sigmoid(col(T_FB + hh))
            k = 1.0 - f
            g = jnp.log(f)
            v = col(T_IB + hh)
            gate = col(T_GB + hh)
            nw = hnw_ref[...]

        g3 = _split3(g)
        sums = jnp.dot(sum_mat, g3, preferred_element_type=f32)
        b = sums[:, :LANES] + sums[:, LANES:2 * LANES] + sums[:, 2 * LANES:]
        o = _near_diagonal(q, k, b, v, ones)
        q_dec = (q * jnp.exp(b)).astype(bf16)
        vb = v.astype(bf16)
        parts = []
        for n in range(NB):
            rows = slice(n * C, (n + 1) * C)
            b_last = b[(n + 1) * C - 1:(n + 1) * C, :]
            k_dec = (k[rows] * jnp.exp(b_last - b[rows])).astype(bf16)
            tot = lax.dot_general(g3[rows], ones_c, _TN, preferred_element_type=f32)
            decay = jnp.exp(tot[:LANES] + tot[LANES:2 * LANES] + tot[2 * LANES:])
            upd = lax.dot_general(k_dec, vb[rows], _TN, preferred_element_type=f32)
            if h < N_HEADS:
                lo, hi = (h % 2) * GLA_DK, (h % 2 + 1) * GLA_DK
                st = sg_in[n, h]
                st_pad = jnp.concatenate([st, zeros_half] if h % 2 == 0 else [zeros_half, st], axis=0)
                sg_out[n, h] = st * decay[lo:hi] + upd[lo:hi]
            else:
                st_pad = sh_in[n, hh]
                sh_out[n, hh] = st_pad * decay + upd
            parts.append(jnp.dot(q_dec[rows], st_pad.astype(bf16), preferred_element_type=f32))
        o = o + (parts[0] if NB == 1 else jnp.concatenate(parts, axis=0))

        y = _rms(o, nw) * (gate * _sigmoid(gate))
        y_ref[:, h * LANES:(h + 1) * LANES] = y.astype(bf16)


def _even_attn(p, w2, bg, lbt, gnw, hnw, sg, sh, *, n_seq, seq_len, C, NB, layer):
    assert C == SUBLANES, "pairs further apart than one 8-row group are not handled here"
    R = NB * C
    nc = seq_len // C
    assert nc == 1, "one chunk per sequence"
    grid = (n_seq // NB, nc)
    row_map = lambda i, c: (i * nc + c, 0)
    in_specs = [
        pl.BlockSpec((R, EVEN_P), row_map),
        _const_spec(w2.shape),
        _const_spec(bg.shape),
        _const_spec(lbt.shape),
        _const_spec((1, EVEN_DV)),
        _const_spec((1, EVEN_DV)),
    ]
    sg_spec = pl.BlockSpec((NB, N_HEADS, GLA_DK, EVEN_DV), lambda i, c: (i, 0, 0, 0))
    sh_spec = pl.BlockSpec((NB, N_HEADS, EVEN_DK, EVEN_DV), lambda i, c: (i, 0, 0, 0))
    in_specs += [sg_spec, sh_spec]
    return pl.pallas_call(
        functools.partial(_even_attn_body, C=C, NB=NB, layer=layer),
        out_shape=(
            jax.ShapeDtypeStruct((n_seq * seq_len, 2 * N_HEADS * EVEN_DV), bf16),
            jax.ShapeDtypeStruct((n_seq, N_HEADS, GLA_DK, EVEN_DV), f32),
            jax.ShapeDtypeStruct((n_seq, N_HEADS, EVEN_DK, EVEN_DV), f32),
        ),
        grid=grid,
        in_specs=in_specs,
        out_specs=(pl.BlockSpec((R, 2 * N_HEADS * EVEN_DV), row_map), sg_spec, sh_spec),
        compiler_params=_params("arbitrary", "arbitrary"),
        name="even_attn",
    )(p, w2, bg, lbt, gnw, hnw, sg, sh)


CHUNK = 64
N_LEVELS = 6
N_SUM_LEVELS = 3
LOG2E = 1.4426950408889634
EVEN_PW = (T_LR + 2) * LANES


def _chunk_tables():
    t = np.arange(CHUNK)[:, None]
    j = np.arange(CHUNK)[None, :]
    mats = []
    for l in range(N_SUM_LEVELS):
        hs = 1 << l
        m = (t // (2 * hs)) * (2 * hs) + hs
        mats.append(np.where(t >= m, (j > m) & (j <= t), (j > t) & (j <= m)))
    mats.append(j <= t)
    a = np.concatenate(mats, axis=0).astype(np.float32)
    x = t ^ j
    lvl = np.where(j > t, N_LEVELS + 1, np.where(j == t, 0, np.floor(np.log2(np.maximum(x, 1))).astype(np.int64) + 1))
    return np.concatenate([a, a], axis=1), lvl.astype(np.int32)


def _log_decay_sums(g2_list, sum_mat):
    hi = [g.astype(bf16) for g in g2_list]
    lo = [(g - h.astype(f32)).astype(bf16) for g, h in zip(g2_list, hi)]
    rhs = jnp.concatenate([jnp.concatenate(hi, axis=1), jnp.concatenate(lo, axis=1)], axis=0)
    z = jnp.dot(sum_mat, rhs, preferred_element_type=f32)
    return [z[:, i * LANES:(i + 1) * LANES] for i in range(len(g2_list))]


def _level_log_decay(z, b, l):
    if l < N_SUM_LEVELS:
        return z[l * CHUNK:(l + 1) * CHUNK]
    hs = 1 << l
    ref = jnp.concatenate([jnp.broadcast_to(b[m:m + 1], (2 * hs, LANES)) for m in range(hs, CHUNK, 2 * hs)], axis=0)
    d = b - ref
    return jnp.minimum(d, -d)


def _decay_units(qs, ks, vs, sts, zs, level, fillers):
    n = len(qs)
    fillers = list(fillers)
    vbs = [v.astype(bf16) for v in vs]
    bs = [z[N_SUM_LEVELS * CHUNK:] for z in zs]
    scores = [0.0] * n
    for l in range(N_LEVELS):
        es = [jnp.exp2(_level_log_decay(zs[i], bs[i], l)) for i in range(n)]
        ss = [lax.dot_general((qs[i] * es[i]).astype(bf16), (ks[i] * es[i]).astype(bf16), _NT,
                              preferred_element_type=f32) for i in range(n)]
        scores = [jnp.where(level == l + 1, ss[i], scores[i]) for i in range(n)]
        if fillers:
            fillers.pop(0)()
    b_lasts = [b[CHUNK - 1:] for b in bs]
    o_in = [jnp.dot(scores[i].astype(bf16), vbs[i], preferred_element_type=f32) for i in range(n)]
    o_st = [lax.dot_general((qs[i] * jnp.exp2(bs[i])).astype(bf16), sts[i].astype(bf16), _NT,
                            preferred_element_type=f32) for i in range(n)]
    upd = [lax.dot_general(vbs[i], (ks[i] * jnp.exp2(b_lasts[i] - bs[i])).astype(bf16), _TN,
                           preferred_element_type=f32) for i in range(n)]
    while fillers:
        fillers.pop(0)()
    outs = [jnp.sum(qs[i] * ks[i], axis=-1, keepdims=True) * vs[i] + o_in[i] + o_st[i] for i in range(n)]
    return outs, [sts[i] * jnp.exp2(b_lasts[i]) + upd[i] for i in range(n)]


def _even_layer_body(xa_ref, xc_ref, nw2_ref, nw3_ref, win_ref, w2_ref, bg_ref, lbt_ref, gnw_ref, hnw_ref, wout_ref,
                     smat_ref, lvl_ref, o_ref, sg_out, sh_out, p_buf, hn_scr, y_scr, st_ref, *, NB, nc, layer):
    R = NB * CHUNK
    j = pl.program_id(0)
    cb = jnp.maximum(j - 1, 0) % nc
    xg0 = T_LR * LANES

    @pl.when(j == 0)
    def _():
        p_buf[...] = jnp.zeros(p_buf.shape, f32)

    @pl.when(cb == 0)
    def _():
        st_ref[...] = jnp.zeros(st_ref.shape, f32)

    hn_scr[...] = _rms(xa_ref[...].reshape(R, D_MODEL), nw2_ref[...]).astype(bf16)

    lbt = lbt_ref[...]
    lbe = jnp.exp(lbt - jnp.max(lbt, axis=0, keepdims=True))
    lb_all = jnp.sum(lbe[:layer + 1], axis=0, keepdims=True) / jnp.sum(lbe, axis=0, keepdims=True)
    sum_mat = smat_ref[...]
    level = lvl_ref[...]

    low_half = lax.broadcasted_iota(jnp.int32, (CHUNK, LANES), 1) < GLA_DK

    def proj_piece(tile):
        def run():
            cols = slice(tile * LANES, (tile + 2) * LANES)
            p_buf[:, cols] = jnp.dot(hn_scr[...], win_ref[:, cols], preferred_element_type=f32)
        return run

    def gate_piece():
        lr = jnp.dot(hn_scr[...], win_ref[:, xg0:], preferred_element_type=f32).astype(bf16)
        p_buf[:, xg0:] = jnp.dot(lr, w2_ref[...], preferred_element_type=f32) + bg_ref[...]

    def tile(rows, t):
        return p_buf[rows, t * LANES:(t + 1) * LANES]

    def prepare(h0):
        qs, ks, g2s, vs, gates, ids = [], [], [], [], [], []
        for h in (h0, h0 + 1):
            hh = h % N_HEADS
            for n in range(NB):
                rows = slice(n * CHUNK, (n + 1) * CHUNK)
                if h < N_HEADS:
                    mine = low_half if h % 2 == 0 else jnp.logical_not(low_half)
                    xg = tile(rows, T_LR + h // 2)
                    qs.append(jnp.where(mine, tile(rows, T_QA + h // 2), 0.0) * (GLA_DK ** -0.5))
                    ks.append(jnp.where(mine, tile(rows, T_KA + h // 2), 0.0))
                    g2s.append((jnp.minimum(xg, 0.0) - jnp.log1p(jnp.exp(-jnp.abs(xg)))) * (LOG2E / GLA_TAU))
                    vs.append(tile(rows, T_VA + h))
                    gates.append(tile(rows, T_RA + h))
                else:
                    lb = lb_all[:, hh * LANES:(hh + 1) * LANES]
                    f = lb + (1.0 - lb) * _sigmoid(tile(rows, T_FB + hh))
                    q = tile(rows, T_QB + hh)
                    qs.append(q * _sigmoid(q))
                    ks.append(1.0 - f)
                    g2s.append(jnp.log2(f))
                    vs.append(tile(rows, T_IB + hh))
                    gates.append(tile(rows, T_GB + hh))
                ids.append((n, h))
        zs = []
        for i in range(0, len(g2s), 2):
            zs += _log_decay_sums(g2s[i:i + 2], sum_mat)
        return qs, ks, vs, gates, ids, zs

    ready = {0: prepare(0)}
    for h0 in range(0, 2 * N_HEADS, 2):
        nw = gnw_ref[...] if h0 < N_HEADS else hnw_ref[...]
        qs, ks, vs, gates, ids, zs = ready.pop(h0)
        if h0 == 0:
            fillers = [proj_piece(T_VA), proj_piece(T_RA)]
        elif h0 == 2:
            fillers = [proj_piece(T_VA + 2), proj_piece(T_RA + 2), proj_piece(T_QA), proj_piece(T_KA), gate_piece]
        else:
            fillers = [proj_piece(t + h0 - N_HEADS) for t in (T_QB, T_FB, T_IB, T_GB)]
        if h0 + 2 < 2 * N_HEADS:
            def prep_next(h1=h0 + 2):
                ready[h1] = prepare(h1)
            fillers.insert(1, prep_next)
        outs, sts = _decay_units(qs, ks, vs, [st_ref[n, h] for n, h in ids], zs, level, fillers)
        for i, (n, h) in enumerate(ids):
            rows = slice(n * CHUNK, (n + 1) * CHUNK)
            st_ref[n, h] = sts[i]
            y = _rms(outs[i], nw) * (gates[i] * _sigmoid(gates[i]))
            y_scr[rows, h * LANES:(h + 1) * LANES] = y.astype(bf16)

    m = jnp.dot(y_scr[...], wout_ref[...], preferred_element_type=f32)
    o_ref[...] = (xc_ref[...].reshape(R, D_MODEL) + _rms(m, nw3_ref[...])).reshape(NB, CHUNK, D_MODEL)

    @pl.when((cb == nc - 1) & (j >= 1))
    def _():
        for n in range(NB):
            for h in range(N_HEADS):
                sg_out[n, h] = st_ref[n, h].T[(h % 2) * GLA_DK:(h % 2 + 1) * GLA_DK, :]
                sh_out[n, h] = st_ref[n, N_HEADS + h].T


def _even_layer_prompt(x3, nw2, nw3, w_in, w2, bg, lbt, gnw, hnw, w_out, *, layer, NB=8):
    n_seq, seq_len, _ = x3.shape
    nc = seq_len // CHUNK
    n_tiles = (n_seq // NB) * nc
    R = NB * CHUNK
    smat, lvl = _chunk_tables()
    x_blk = (NB, CHUNK, D_MODEL)

    def tile(lag):
        def index(j):
            t = jnp.clip(j - lag, 0, n_tiles - 1)
            return (t // nc, t % nc, 0)
        return index

    group = lambda j: (jnp.maximum(j - 1, 0) // nc, 0, 0, 0)
    return pl.pallas_call(
        functools.partial(_even_layer_body, NB=NB, nc=nc, layer=layer),
        out_shape=(
            jax.ShapeDtypeStruct(x3.shape, f32),
            jax.ShapeDtypeStruct((n_seq, N_HEADS, GLA_DK, EVEN_DV), f32),
            jax.ShapeDtypeStruct((n_seq, N_HEADS, EVEN_DK, EVEN_DV), f32),
        ),
        grid=(n_tiles + 1,),
        in_specs=[
            pl.BlockSpec(x_blk, tile(0)),
            pl.BlockSpec(x_blk, tile(1)),
            _const_spec((1, D_MODEL)),
            _const_spec((1, D_MODEL)),
            _const_spec(w_in.shape),
            _const_spec(w2.shape),
            _const_spec(bg.shape),
            _const_spec(lbt.shape),
            _const_spec((1, EVEN_DV)),
            _const_spec((1, EVEN_DV)),
            _const_spec(w_out.shape),
            _const_spec(smat.shape),
            _const_spec(lvl.shape),
        ],
        out_specs=(
            pl.BlockSpec(x_blk, tile(1)),
            pl.BlockSpec((NB, N_HEADS, GLA_DK, EVEN_DV), group),
            pl.BlockSpec((NB, N_HEADS, EVEN_DK, EVEN_DV), group),
        ),
        scratch_shapes=[
            pltpu.VMEM((R, EVEN_PW), f32),
            pltpu.VMEM((R, D_MODEL), bf16),
            pltpu.VMEM((R, 2 * N_HEADS * EVEN_DV), bf16),
            pltpu.VMEM((NB, 2 * N_HEADS, EVEN_DV, EVEN_DK), f32),
        ],
        compiler_params=_params("arbitrary"),
        name="even_layer",
    )(x3, x3, nw2, nw3, w_in, w2, bg, lbt, gnw, hnw, w_out, jnp.asarray(smat, bf16), jnp.asarray(lvl))


RET_CHUNK = 256


def _rope_body(cos_ref, sin_ref, *, pos0):
    n, half = cos_ref.shape
    pos = (pos0 + lax.broadcasted_iota(jnp.int32, (n, half), 0)).astype(f32)
    lane = lax.broadcasted_iota(jnp.int32, (n, half), 1).astype(f32)
    ang = pos * jnp.power(ROPE_BASE, -lane / half)
    cos_ref[...] = jnp.cos(ang)
    sin_ref[...] = jnp.sin(ang)


def _rope_tables(n, pos0):
    shape = jax.ShapeDtypeStruct((n, RET_DK // 2), f32)
    return pl.pallas_call(functools.partial(_rope_body, pos0=pos0), out_shape=(shape, shape), name="rope_tables")()


def _ret_cols(h):
    return [slice(off + h * w, off + (h + 1) * w) for off, w in (
        (0, RET_DK), (N_HEADS * RET_DK, RET_DK), (2 * N_HEADS * RET_DK, RET_DV),
        (2 * N_HEADS * RET_DK + N_HEADS * RET_DV, RET_DV))]


def _rotary(x, cos, sin):
    half = x.shape[1] // 2
    x1, x2 = x[:, :half], x[:, half:]
    return jnp.concatenate([x1 * cos - x2 * sin, x1 * sin + x2 * cos], axis=1)


def _odd_layer_body(xa_ref, xc_ref, cos_ref, sin_ref, nw2_ref, nw3_ref, win_ref, rnw_ref, wout_ref,
                    ps_ref, cos_s_ref, sin_s_ref, s_in, o_ref, s_out, ys_ref, s_new,
                    p_buf, hn_scr, y_scr, dec_scr, *, nc, NS, TS):
    C = RET_CHUNK
    j = pl.program_id(0)
    cb = jnp.maximum(j - 1, 0) % nc

    @pl.when(j == 0)
    def _():
        p_buf[...] = jnp.zeros(p_buf.shape, f32)
        ri = lax.broadcasted_iota(jnp.int32, (C, C), 0)
        ci = lax.broadcasted_iota(jnp.int32, (C, C), 1)
        dist = jnp.maximum(ri - ci, 0).astype(f32)
        for h in range(N_HEADS):
            dec_scr[h] = jnp.where(ci <= ri, jnp.exp2(dist * math.log2(1.0 - 2.0 ** (-5.0 - h))), 0.0)

    @pl.when(cb == 0)
    def _():
        s_out[...] = jnp.zeros(s_out.shape, f32)

    hn_scr[...] = _rms(xa_ref[0], nw2_ref[...]).astype(bf16)
    cos, sin = cos_ref[...], sin_ref[...]
    t_k = lax.broadcasted_iota(jnp.int32, (C, RET_DK), 0).astype(f32)

    RS = NS * TS
    cos_s = jnp.concatenate([cos_s_ref[...]] * NS, axis=0)
    sin_s = jnp.concatenate([sin_s_ref[...]] * NS, axis=0)
    ri = lax.broadcasted_iota(jnp.int32, (RS, RS), 0)
    ci = lax.broadcasted_iota(jnp.int32, (RS, RS), 1)
    visible = ((ri // TS) == (ci // TS)) & (ci <= ri)
    dist_s = jnp.where(visible, ri - ci, 0).astype(f32)
    t_s = (lax.broadcasted_iota(jnp.int32, (RS, RET_DK), 0) % TS).astype(f32)

    lg = [math.log2(1.0 - 2.0 ** (-5.0 - h)) for h in range(N_HEADS)]
    colsl = [_ret_cols(h) for h in range(N_HEADS)]
    for h in range(N_HEADS):
        log2_gamma = lg[h]
        cols = colsl[h]
        q = _rotary(ps_ref[:, cols[0]], cos_s, sin_s)
        k = _rotary(ps_ref[:, cols[1]], cos_s, sin_s) * (RET_DK ** -0.5)
        vb = ps_ref[:, cols[2]].astype(bf16)
        gate = ps_ref[:, cols[3]]
        decay = jnp.where(visible, jnp.exp2(dist_s * log2_gamma), 0.0)
        s = lax.dot_general(q.astype(bf16), k.astype(bf16), _NT, preferred_element_type=f32) * decay
        o = jnp.dot(s.astype(bf16), vb, preferred_element_type=f32)
        q_dec = (q * jnp.exp2((t_s + 1.0) * log2_gamma)).astype(bf16)
        k_dec = (k * jnp.exp2((TS - 1.0 - t_s) * log2_gamma)).astype(bf16)
        parts = []
        for n in range(NS):
            rows = slice(n * TS, (n + 1) * TS)
            st = s_in[n, h]
            parts.append(jnp.dot(q_dec[rows], st.astype(bf16), preferred_element_type=f32))
            upd = lax.dot_general(k_dec[rows], vb[rows], _TN, preferred_element_type=f32)
            s_new[n, h] = st * (2.0 ** (TS * log2_gamma)) + upd
        o = o + jnp.concatenate(parts, axis=0)
        y = _rms(o, rnw_ref[...]) * (gate * _sigmoid(gate))
        ys_ref[:, h * RET_DV:(h + 1) * RET_DV] = y.astype(bf16)

    for h in range(N_HEADS):
        log2_gamma = lg[h]
        cols = colsl[h]
        q = _rotary(p_buf[:, cols[0]], cos, sin)
        k = _rotary(p_buf[:, cols[1]], cos, sin) * (RET_DK ** -0.5)
        vb = p_buf[:, cols[2]].astype(bf16)
        gate = p_buf[:, cols[3]]
        s = lax.dot_general(q.astype(bf16), k.astype(bf16), _NT, preferred_element_type=f32) * dec_scr[h]
        o = jnp.dot(s.astype(bf16), vb, preferred_element_type=f32)
        st = s_out[0, h]
        q_dec = (q * jnp.exp2((t_k + 1.0) * log2_gamma)).astype(bf16)
        o = o + jnp.dot(q_dec, st.astype(bf16), preferred_element_type=f32)
        k_dec = (k * jnp.exp2((C - 1.0 - t_k) * log2_gamma)).astype(bf16)
        upd = lax.dot_general(k_dec, vb, _TN, preferred_element_type=f32)
        s_out[0, h] = st * (2.0 ** (C * log2_gamma)) + upd
        y = _rms(o, rnw_ref[...]) * (gate * _sigmoid(gate))
        y_scr[:, h * RET_DV:(h + 1) * RET_DV] = y.astype(bf16)
        for cs in cols:
            p_buf[:, cs] = jnp.dot(hn_scr[...], win_ref[:, cs], preferred_element_type=f32)

    m = jnp.dot(y_scr[...], wout_ref[...], preferred_element_type=f32)
    o_ref[0] = xc_ref[0] + _rms(m, nw3_ref[...])


def _odd_layer(x3, ps, s_ret, nw2, nw3, w_in, rnw, w_out):
    n_seq, seq_len, _ = x3.shape
    n_s = s_ret.shape[0]
    TS = ps.shape[0] // n_s
    C = RET_CHUNK
    nc = seq_len // C
    n_tiles = n_seq * nc
    NS = n_s // n_tiles
    assert NS * n_tiles == n_s and (NS * TS) % 16 == 0
    cos, sin = _rope_tables(seq_len, 0)
    cos_s, sin_s = _rope_tables(TS, PAST_LEN)
    x_blk = (1, C, D_MODEL)

    def tile(lag):
        def index(j):
            t = jnp.clip(j - lag, 0, n_tiles - 1)
            return (t // nc, t % nc, 0)
        return index

    short = lambda j: jnp.minimum(j, n_tiles - 1)
    rope_spec = pl.BlockSpec((C, RET_DK // 2), lambda j: (jnp.maximum(j - 1, 0) % nc, 0))
    state_s_spec = pl.BlockSpec((NS, N_HEADS, RET_DK, RET_DV), lambda j: (short(j), 0, 0, 0))
    return pl.pallas_call(
        functools.partial(_odd_layer_body, nc=nc, NS=NS, TS=TS),
        out_shape=(
            jax.ShapeDtypeStruct(x3.shape, f32),
            jax.ShapeDtypeStruct((n_seq, N_HEADS, RET_DK, RET_DV), f32),
            jax.ShapeDtypeStruct((n_s * TS, N_HEADS * RET_DV), bf16),
            jax.ShapeDtypeStruct(s_ret.shape, f32),
        ),
        grid=(n_tiles + 1,),
        in_specs=[
            pl.BlockSpec(x_blk, tile(0)),
            pl.BlockSpec(x_blk, tile(1)),
            rope_spec,
            rope_spec,
            _const_spec((1, D_MODEL)),
            _const_spec((1, D_MODEL)),
            _const_spec(w_in.shape),
            _const_spec((1, RET_DV)),
            _const_spec(w_out.shape),
            pl.BlockSpec((NS * TS, ODD_P), lambda j: (short(j), 0)),
            _const_spec((TS, RET_DK // 2)),
            _const_spec((TS, RET_DK // 2)),
            state_s_spec,
        ],
        out_specs=(
            pl.BlockSpec(x_blk, tile(1)),
            pl.BlockSpec((1, N_HEADS, RET_DK, RET_DV), lambda j: (jnp.maximum(j - 1, 0) // nc, 0, 0, 0)),
            pl.BlockSpec((NS * TS, N_HEADS * RET_DV), lambda j: (short(j), 0)),
            state_s_spec,
        ),
        scratch_shapes=[
            pltpu.VMEM((C, ODD_P), f32),
            pltpu.VMEM((C, D_MODEL), bf16),
            pltpu.VMEM((C, N_HEADS * RET_DV), bf16),
            pltpu.VMEM((N_HEADS, C, C), f32),
        ],
        compiler_params=_params("arbitrary"),
        name="odd_layer",
    )(x3, x3, cos, sin, nw2, nw3, w_in, rnw, w_out, ps, cos_s, sin_s, s_ret)


def _pack_even_w_in(w):
    lr0 = 2 * N_HEADS * GLA_DK + 2 * N_HEADS * EVEN_DV
    lr = jnp.pad(w[:, lr0:lr0 + GLA_LOWRANK], ((0, 0), (0, LANES - GLA_LOWRANK)))
    return jnp.concatenate([w[:, :lr0], w[:, lr0 + GLA_LOWRANK:], lr], axis=1)


def _mixer_even_prompt(x, l, w, row, n_seq, seq_len):
    e = l // 2
    x3, sg, sh = _even_layer_prompt(x.reshape(n_seq, seq_len, D_MODEL), row(2), row(3), w["even_in"][e], w["gla_w2"][e],
                                    w["gla_bg"][e], w["lb_table"], w["gla_nw"][e], w["hgrn_nw"][e], w["even_out"][e],
                                    layer=l)
    return x3.reshape(n_seq * seq_len, D_MODEL), sg, sh


def _mixer_even_sample(x, l, w, row, s_gla, s_hgrn, n_seq, seq_len):
    e = l // 2
    p = _proj(x, row(2), w["even_in"][e], tm=512, tn=EVEN_P)
    y, sg, sh = _even_attn(p, w["gla_w2"][e], w["gla_bg"][e], w["lb_table"], w["gla_nw"][e], w["hgrn_nw"][e],
                           s_gla[e], s_hgrn[e], n_seq=n_seq, seq_len=seq_len, C=seq_len, NB=16, layer=l)
    return _out_block(y, x, w["even_out"][e], row(3)), sg, sh


def _mixer_odd(xp, xs, l, w, row, s_ret, n_seq, seq_len):
    o = l // 2
    ps = _proj(xs, row(2), w["odd_in"][o], tm=256, tn=12 * LANES)
    x3, sr_p, ys, sr_s = _odd_layer(xp.reshape(n_seq, seq_len, D_MODEL), ps, s_ret[o], row(2), row(3), w["odd_in"][o],
                                    w["ret_nw"][o], w["odd_out"][o])
    return x3.reshape(n_seq * seq_len, D_MODEL), _out_block(ys, xs, w["odd_out"][o], row(3)), sr_p, sr_s


def kernel(x_prompt, x_sample, state_gla, state_hgrn, state_ret, norm_w, ffn_w_gate, ffn_w_up, ffn_w_down, even_w_in, gla_w_gate2, gla_b_gate, gla_norm_w, hgrn_lb_table, hgrn_norm_w, even_w_out, odd_w_in, ret_norm_w, odd_w_out):
    bp, tp, _ = x_prompt.shape
    bs, ts, _ = x_sample.shape
    n_even = even_w_in.shape[0]
    nw = norm_w.astype(f32)
    ffn_w = [w_.astype(f32) for w_ in (ffn_w_gate, ffn_w_up, ffn_w_down)]
    w = {
        "even_in": jnp.stack([_pack_even_w_in(even_w_in[e]) for e in range(n_even)]).astype(bf16),
        "gla_w2": jnp.pad(gla_w_gate2, ((0, 0), (0, LANES - GLA_LOWRANK), (0, 0))).astype(bf16),
        "gla_bg": gla_b_gate[:, None, :].astype(f32),
        "lb_table": hgrn_lb_table.astype(f32),
        "gla_nw": gla_norm_w[:, None, :].astype(f32),
        "hgrn_nw": hgrn_norm_w[:, None, :].astype(f32),
        "even_out": even_w_out.astype(bf16),
        "odd_in": odd_w_in.astype(bf16),
        "ret_nw": ret_norm_w[:, None, :].astype(f32),
        "odd_out": odd_w_out.astype(bf16),
    }
    xp = x_prompt.reshape(bp * tp, D_MODEL)
    xs = x_sample.reshape(bs * ts, D_MODEL)
    new_p, new_s = ([], [], []), ([], [], [])
    for l in range(nw.shape[0]):
        row = lambda i, l=l: nw[l, i][None, :]
        xp, xs = _ffn(xp, xs, row(0), row(1), *ffn_w, l, 0)
        if l % 2 == 0:
            xp, sg, sh = _mixer_even_prompt(xp, l, w, row, bp, tp)
            new_p[0].append(sg)
            new_p[1].append(sh)
            xs, sg, sh = _mixer_even_sample(xs, l, w, row, state_gla, state_hgrn, bs, ts)
            new_s[0].append(sg)
            new_s[1].append(sh)
        else:
            xp, xs, sr_p, sr_s = _mixer_odd(xp, xs, l, w, row, state_ret, bp, tp)
            new_p[2].append(sr_p)
            new_s[2].append(sr_s)
        xp, xs = _ffn(xp, xs, row(4), row(5), *ffn_w, l, 1)
    gla_p, hgrn_p, ret_p = (jnp.stack(v) for v in new_p)
    gla_s, hgrn_s, ret_s = (jnp.stack(v) for v in new_s)
    return (xp.reshape(bp, tp, D_MODEL), xs.reshape(bs, ts, D_MODEL), gla_p, hgrn_p, ret_p, gla_s, hgrn_s, ret_s)
```
